```python
import functools
import jax
import jax.numpy as jnp
from jax import lax
import numpy as np

D_MODEL = 2048
BATCH = 1
SEQ = 8192
DEPTH = 2
DEC_BATCH = 32
DEC_SEQ = 8
PAST_LEN = 8192
PAGE_SIZE = 128

HEAD_DIM = 128
N_HEADS_TOTAL = D_MODEL // HEAD_DIM
H_M = N_HEADS_TOTAL // 2
H_A = N_HEADS_TOTAL - H_M
DK_M = HEAD_DIM // 2
DV_M = HEAD_DIM
MLSTM_CHUNK = 64
DILATIONS = ((128, 1), (512, 4), (2048, 16))
MAX_WINDOW = 2048
ATT_BLOCK = 128
ROPE_THETA = 10000.0
D_FF = 5632
CONV_W = 3
ALPHA = (2 * DEPTH) ** 0.25
BETA = (8 * DEPTH) ** -0.25
LN_EPS = 1e-5
HEAD_NORM_EPS = 1e-6
IN_SIZES = (H_M * DK_M, H_M * DK_M, H_M * DV_M, H_M * DV_M, H_M, H_M,
            H_A * HEAD_DIM, H_A * HEAD_DIM, H_A * HEAD_DIM)
D_IN = sum(IN_SIZES)

kernel_name = 'hymba_mlstm_dilated_convffn_step'


def _layer_norm(x, g, b):
    xf = x.astype(jnp.float32)
    mu = jnp.mean(xf, -1, keepdims=True)
    var = jnp.mean(jnp.square(xf - mu), -1, keepdims=True)
    y = (xf - mu) * lax.rsqrt(var + LN_EPS) * g.astype(jnp.float32) + b.astype(jnp.float32)
    return y.astype(x.dtype)


def _rope(x, pos):
    half = HEAD_DIM // 2
    inv = ROPE_THETA ** (-jnp.arange(half, dtype=jnp.float32) / half)
    ang = pos.astype(jnp.float32)[:, None] * inv[None, :]
    cos = jnp.cos(ang)[None, :, None, :]
    sin = jnp.sin(ang)[None, :, None, :]
    xf = x.astype(jnp.float32)
    x1, x2 = xf[..., :half], xf[..., half:]
    return jnp.concatenate([x1 * cos - x2 * sin, x2 * cos + x1 * sin], -1).astype(x.dtype)


def _split_cols(p):
    outs, start = [], 0
    for n in IN_SIZES:
        outs.append(p[..., start:start + n])
        start += n
    return outs


def _project_in(u, w_in, b_gate, pos):
    B, T, _ = u.shape
    f32 = jnp.float32
    qm, km, vm, om, ig, fg, qa, ka, va = _split_cols(u @ w_in)
    qm = qm.reshape(B, T, H_M, DK_M).astype(f32)
    km = km.reshape(B, T, H_M, DK_M).astype(f32) * (DK_M ** -0.5)
    vm = vm.reshape(B, T, H_M, DV_M).astype(f32)
    ig = ig.astype(f32) + b_gate[0].astype(f32)
    lf = jax.nn.log_sigmoid(fg.astype(f32) + b_gate[1].astype(f32))
    qa = _rope(qa.reshape(B, T, H_A, HEAD_DIM), pos)
    ka = _rope(ka.reshape(B, T, H_A, HEAD_DIM), pos)
    va = va.reshape(B, T, H_A, HEAD_DIM)
    return qm, km, vm, om, ig, lf, qa, ka, va


def _mlstm_chunk(carry, inp):
    C0, n0, m0 = carry
    q, k, v, ig, lf = inp
    L = q.shape[1]
    b = jnp.cumsum(lf, axis=1)
    causal = jnp.arange(L)[:, None] >= jnp.arange(L)[None, :]
    dmat = b[:, :, None] - b[:, None] + ig[:, None]
    dmat = jnp.where(causal[None, :, :, None], dmat, -jnp.inf)
    m_inter = b + m0[:, None]
    m = jnp.maximum(m_inter, dmat.max(axis=2))
    w = jnp.exp(dmat - m[:, :, None])
    g = jnp.exp(m_inter - m)
    s = jnp.einsum('bthk,bshk->btsh', q, k) * w
    num = g[..., None] * jnp.einsum('bthk,bhkv->bthv', q, C0) + jnp.einsum('btsh,bshv->bthv', s, v)
    den = g * jnp.einsum('bthk,bhk->bth', q, n0) + s.sum(2)
    h = num / jnp.maximum(jnp.abs(den), jnp.exp(-m))[..., None]
    a = b[:, -1:] - b + ig
    m_end = jnp.maximum(b[:, -1] + m0, a.max(1))
    ws = jnp.exp(a - m_end[:, None])
    g_end = jnp.exp(b[:, -1] + m0 - m_end)
    C1 = g_end[..., None, None] * C0 + jnp.einsum('bsh,bshk,bshv->bhkv', ws, k, v)
    n1 = g_end[..., None] * n0 + jnp.einsum('bsh,bshk->bhk', ws, k)
    return (C1, n1, m_end), h


def _mlstm_prompt(q, k, v, ig, lf):
    B, S = q.shape[:2]
    L = MLSTM_CHUNK
    NC = S // L

    def chunks(x):
        return jnp.moveaxis(x.reshape((B, NC, L) + x.shape[2:]), 1, 0)

    f32 = jnp.float32
    init = (jnp.zeros((B, H_M, DK_M, DV_M), f32), jnp.zeros((B, H_M, DK_M), f32), jnp.zeros((B, H_M), f32))
    state, h = lax.scan(_mlstm_chunk, init, (chunks(q), chunks(k), chunks(v), chunks(ig), chunks(lf)))
    h = jnp.moveaxis(h, 0, 1).reshape(B, S, H_M, DV_M)
    return h, state


def _dilated_branch_prompt(q, k, v, window, dil):
    B, S, H, D = q.shape
    U = S // dil
    nb = -(-U // ATT_BLOCK)
    Up = nb * ATT_BLOCK

    def sub(x):
        return jnp.transpose(x.reshape(B, U, dil, H, D), (0, 2, 1, 3, 4))

    qs = jnp.pad(sub(q), ((0, 0), (0, 0), (0, Up - U), (0, 0), (0, 0))).reshape(B, dil, nb, ATT_BLOCK, H, D)

    def kwin(x):
        xp = jnp.pad(sub(x), ((0, 0), (0, 0), (ATT_BLOCK, Up - U), (0, 0), (0, 0)))
        xp = xp.reshape(B, dil, nb + 1, ATT_BLOCK, H, D)
        return jnp.concatenate([xp[:, :, :-1], xp[:, :, 1:]], axis=3)

    ks, vs = kwin(k), kwin(v)
    s = jnp.einsum('bgnqhd,bgnkhd->bgnhqk', qs, ks)
    qi = jnp.arange(ATT_BLOCK)[:, None]
    ki = jnp.arange(2 * ATT_BLOCK)[None, :]
    dist = qi + ATT_BLOCK - ki
    key_sub = jnp.arange(nb)[:, None, None] * ATT_BLOCK - ATT_BLOCK + ki[None]
    valid = (dist >= 0) & (dist <= window // dil) & (key_sub >= 0)
    s = jnp.where(valid[:, None], s, -jnp.inf)
    m = s.max(-1)
    p = jnp.exp(s - m[..., None])
    den = p.sum(-1)
    num = jnp.einsum('bgnhqk,bgnkhd->bgnqhd', p, vs)

    def unsub(y):
        y = y.reshape((B, dil, Up) + y.shape[4:])[:, :, :U]
        y = jnp.moveaxis(y, 1, 2)
        return y.reshape((B, S) + y.shape[3:])

    return unsub(num), unsub(jnp.swapaxes(m, -1, -2)), unsub(jnp.swapaxes(den, -1, -2))


def _dilated_branch_sample(q, k_all, v_all, window, dil):
    T = q.shape[1]
    n_buf = k_all.shape[1] - T
    J = window // dil + 1
    idx = n_buf + jnp.arange(T)[:, None] - dil * jnp.arange(J)[None, :]
    valid = idx >= 0
    idxc = jnp.maximum(idx, 0)
    kg = k_all[:, idxc]
    vg = v_all[:, idxc]
    s = jnp.einsum('bthd,btjhd->bthj', q, kg)
    s = jnp.where(valid[None, :, None, :], s, -jnp.inf)
    m = s.max(-1)
    p = jnp.exp(s - m[..., None])
    den = p.sum(-1)
    num = jnp.einsum('bthj,btjhd->bthd', p, vg)
    return num, m, den


def _combine(parts):
    M = parts[0][1]
    for _, m, _ in parts[1:]:
        M = jnp.maximum(M, m)
    num, m0, den0 = parts[0]
    w0 = jnp.exp(m0 - M)
    acc_num = w0[..., None] * num
    acc_den = w0 * den0
    for num, m, den in parts[1:]:
        w = jnp.exp(m - M)
        acc_num = acc_num + w[..., None] * num
        acc_den = acc_den + w * den
    return acc_num / acc_den[..., None]


def _mixer_merge(h, om, g_head, att, w_out):
    B, T = h.shape[:2]
    hn = h * lax.rsqrt(jnp.mean(h * h, -1, keepdims=True) + HEAD_NORM_EPS) * g_head.astype(jnp.float32)
    hm = hn.reshape(B, T, H_M * DV_M) * jax.nn.sigmoid(om.astype(jnp.float32))
    cat = jnp.concatenate([hm, att.reshape(B, T, H_A * HEAD_DIM)], -1).astype(w_out.dtype)
    return cat @ w_out


def _conv_ffn(u, buf, w_up, conv_w, conv_b, w_down):
    T = u.shape[1]
    gv = u @ w_up
    g, v = gv[..., :D_FF], gv[..., D_FF:]
    gp = jnp.concatenate([buf.astype(g.dtype), g], axis=1)
    acc = gp[:, 0:T] * conv_w[0]
    for j in range(1, CONV_W):
        acc = acc + gp[:, j:j + T] * conv_w[j]
    a = jax.nn.silu(acc + conv_b) * v
    return a @ w_down, gp[:, T:]


def _prompt_core(qm, km, vm, ig, lf, qa, ka, va):
    S = qm.shape[1]
    h, (C, n, m) = _mlstm_prompt(qm, km, vm, ig, lf)
    q = qa.astype(jnp.float32) * (HEAD_DIM ** -0.5)
    k = ka.astype(jnp.float32)
    v = va.astype(jnp.float32)
    att = _combine([_dilated_branch_prompt(q, k, v, w, d) for (w, d) in DILATIONS])
    wp = min(MAX_WINDOW, S)
    return h, att, (ka[:, S - wp:], va[:, S - wp:], C, n, m)


def _sample_core(qm, km, vm, ig, lf, qa, ka, va, cache_k, cache_v, C0, n0, m0):
    f32 = jnp.float32
    (C, n, m), h = _mlstm_chunk((C0.astype(f32), n0.astype(f32), m0.astype(f32)), (qm, km, vm, ig, lf))
    q = qa.astype(f32) * (HEAD_DIM ** -0.5)
    k_all = jnp.concatenate([cache_k.astype(f32), ka.astype(f32)], axis=1)
    v_all = jnp.concatenate([cache_v.astype(f32), va.astype(f32)], axis=1)
    att = _combine([_dilated_branch_sample(q, k_all, v_all, w, d) for (w, d) in DILATIONS])
    return h, att, (ka, va, C, n, m)


def _trunk_layer(x, c, pos, core, conv_buf, w_ada, b_ada, w_in, b_gate, g_head, w_out,
                 ln1_g, ln1_b, w_up, conv_w, conv_b, w_down, ln2_g, ln2_b):
    mod = (jax.nn.silu(c) @ w_ada + b_ada)[:, None, :]
    sh1, sc1, gt1, sh2, sc2, gt2 = jnp.split(mod, 6, axis=-1)
    u = x * (1 + sc1) + sh1
    qm, km, vm, om, ig, lf, qa, ka, va = _project_in(u, w_in, b_gate, pos)
    h, att, mix_state = core(qm, km, vm, ig, lf, qa, ka, va)
    mix = _mixer_merge(h, om, g_head, att, w_out)
    x = _layer_norm(ALPHA * x + (1 + gt1) * mix, ln1_g, ln1_b)
    u2 = x * (1 + sc2) + sh2
    f, new_buf = _conv_ffn(u2, conv_buf, w_up, conv_w, conv_b, w_down)
    x = _layer_norm(ALPHA * x + (1 + gt2) * f, ln2_g, ln2_b)
    return x, mix_state, new_buf


def setup_inputs(seed: int = 0) -> dict:
    key = jax.random.key(seed)
    ks = jax.random.split(key, 32)
    f32 = jnp.float32

    def nrm(k, shape, s):
        return s * jax.random.normal(k, shape, f32)

    wb = min(MAX_WINDOW, PAST_LEN)
    b_gate = jnp.stack([nrm(ks[0], (DEPTH, H_M), 0.1),
                        jnp.linspace(3.0, 6.0, H_M, dtype=f32)[None, :] + nrm(ks[1], (DEPTH, H_M), 0.1)], axis=1)
    return {
        'x_prompt': nrm(ks[2], (BATCH, SEQ, D_MODEL), 1.0),
        'x_sample': nrm(ks[3], (DEC_BATCH, DEC_SEQ, D_MODEL), 1.0),
        'cache_k_win': nrm(ks[4], (DEPTH, DEC_BATCH, wb, H_A, HEAD_DIM), 1.0),
        'cache_v_win': nrm(ks[5], (DEPTH, DEC_BATCH, wb, H_A, HEAD_DIM), 1.0),
        'state_C': nrm(ks[6], (DEPTH, DEC_BATCH, H_M, DK_M, DV_M), 1.0),
        'state_n': nrm(ks[7], (DEPTH, DEC_BATCH, H_M, DK_M), 1.0),
        'state_m': nrm(ks[8], (DEPTH, DEC_BATCH, H_M), 1.0),
        'state_conv': nrm(ks[9], (DEPTH, DEC_BATCH, CONV_W - 1, D_FF), 1.0),
        'c_prompt': nrm(ks[10], (BATCH, D_MODEL), 1.0),
        'c_sample': nrm(ks[11], (DEC_BATCH, D_MODEL), 1.0),
        'w_ada': nrm(ks[12], (DEPTH, D_MODEL, 6 * D_MODEL), 0.1 * D_MODEL ** -0.5),
        'b_ada': nrm(ks[13], (DEPTH, 6 * D_MODEL), 0.02),
        'w_in': nrm(ks[14], (DEPTH, D_MODEL, D_IN), D_MODEL ** -0.5),
        'b_gate': b_gate,
        'g_head': 1.0 + nrm(ks[15], (DEPTH, H_M, DV_M), 0.02),
        'w_out': nrm(ks[16], (DEPTH, D_MODEL, D_MODEL), BETA * D_MODEL ** -0.5),
        'ln1_g': 1.0 + nrm(ks[17], (DEPTH, D_MODEL), 0.02),
        'ln1_b': nrm(ks[18], (DEPTH, D_MODEL), 0.02),
        'w_up': nrm(ks[19], (DEPTH, D_MODEL, 2 * D_FF), D_MODEL ** -0.5),
        'conv_w': nrm(ks[20], (DEPTH, CONV_W, D_FF), CONV_W ** -0.5),
        'conv_b': nrm(ks[21], (DEPTH, D_FF), 0.02),
        'w_down': nrm(ks[22], (DEPTH, D_FF, D_MODEL), BETA * (2.0 / (D_FF + D_MODEL)) ** 0.5),
        'ln2_g': 1.0 + nrm(ks[23], (DEPTH, D_MODEL), 0.02),
        'ln2_b': nrm(ks[24], (DEPTH, D_MODEL), 0.02),
    }


def reference(x_prompt, x_sample, cache_k_win, cache_v_win, state_C, state_n, state_m, state_conv,
              c_prompt, c_sample, w_ada, b_ada, w_in, b_gate, g_head, w_out, ln1_g, ln1_b,
              w_up, conv_w, conv_b, w_down, ln2_g, ln2_b):
    B, S = x_prompt.shape[:2]
    T = x_sample.shape[1]
    pos_p = jnp.arange(S, dtype=jnp.int32)
    pos_s = PAST_LEN + jnp.arange(T, dtype=jnp.int32)
    xp, xs = x_prompt, x_sample
    kp_l, vp_l, Cp_l, np_l, mp_l, bp_l = [], [], [], [], [], []
    ks_l, vs_l, Cs_l, ns_l, ms_l, bs_l = [], [], [], [], [], []
    for l in range(DEPTH):
        wl = (w_ada[l], b_ada[l], w_in[l], b_gate[l], g_head[l], w_out[l], ln1_g[l], ln1_b[l],
              w_up[l], conv_w[l], conv_b[l], w_down[l], ln2_g[l], ln2_b[l])
        buf0 = jnp.zeros((B, CONV_W - 1, D_FF), xp.dtype)
        xp, (kp, vp, Cp, n_p, mp), bp = _trunk_layer(xp, c_prompt, pos_p, _prompt_core, buf0, *wl)
        core_s = functools.partial(_sample_core, cache_k=cache_k_win[l], cache_v=cache_v_win[l],
                                   C0=state_C[l], n0=state_n[l], m0=state_m[l])
        xs, (ksn, vsn, Cs, n_s, ms), bs = _trunk_layer(xs, c_sample, pos_s, core_s, state_conv[l], *wl)
        kp_l.append(kp); vp_l.append(vp); Cp_l.append(Cp); np_l.append(n_p); mp_l.append(mp); bp_l.append(bp)
        ks_l.append(ksn); vs_l.append(vsn); Cs_l.append(Cs); ns_l.append(n_s); ms_l.append(ms); bs_l.append(bs)
    return (xp, xs,
            jnp.stack(kp_l), jnp.stack(vp_l), jnp.stack(Cp_l), jnp.stack(np_l), jnp.stack(mp_l), jnp.stack(bp_l),
            jnp.stack(ks_l), jnp.stack(vs_l), jnp.stack(Cs_l), jnp.stack(ns_l), jnp.stack(ms_l), jnp.stack(bs_l))
```

```python
import functools

import jax
import jax.numpy as jnp
from jax import lax
from jax.experimental import pallas as pl
from jax.experimental.pallas import tpu as pltpu

F32 = jnp.float32
BF16 = jnp.bfloat16
HIGHEST = lax.Precision.HIGHEST

HEAD_DIM = 128
H_M = 8
H_A = 8
DK_M = 64
DV_M = 128
DILATIONS = ((128, 1), (512, 4), (2048, 16))
ATT_BLOCK = 128
PAST_LEN = 8192
ROPE_THETA = 10000.0
CONV_W = 3
LN_EPS = 1e-5
HEAD_NORM_EPS = 1e-6
N_GATE_PAD = 128
W_HEADS = H_A * HEAD_DIM

VMEM_LIMIT_BYTES = 56 * 1024 * 1024


def _cparams(*sem):
    return pltpu.CompilerParams(dimension_semantics=sem, vmem_limit_bytes=VMEM_LIMIT_BYTES)


def _dot(a, b):
    return jnp.dot(a, b, preferred_element_type=F32)


def _dot_nt(a, b, precision=None):
    return lax.dot_general(a, b, (((1,), (1,)), ((), ())), precision=precision,
                           preferred_element_type=F32)


def _dot_tn(a, b):
    return lax.dot_general(a, b, (((0,), (0,)), ((), ())), preferred_element_type=F32)


def _iota(shape, dim):
    return lax.broadcasted_iota(jnp.int32, shape, dim)


def _mod_kernel(c_ref, w_ref, b_ref, o_ref):
    c = c_ref[...]
    a = (c * jax.nn.sigmoid(c)).astype(BF16)
    o_ref[...] = _dot(a, w_ref[...].astype(BF16)) + b_ref[...]


def _modulation(c_all, w_ada, b_ada, tn=1024):
    depth, d, n = w_ada.shape
    rows = c_all.shape[0]
    return pl.pallas_call(
        _mod_kernel,
        grid=(depth, n // tn),
        in_specs=[
            pl.BlockSpec((rows, d), lambda l, j: (0, 0)),
            pl.BlockSpec((None, d, tn), lambda l, j: (l, 0, j)),
            pl.BlockSpec((None, 1, tn), lambda l, j: (l, 0, j)),
        ],
        out_specs=pl.BlockSpec((None, rows, tn), lambda l, j: (l, 0, j)),
        out_shape=jax.ShapeDtypeStruct((depth, rows, n), F32),
        compiler_params=_cparams("arbitrary", "arbitrary"),
        name="adaln_mod",
    )(c_all, w_ada, b_ada.reshape(depth, 1, n))


def _inproj_kernel(x_ref, sc_ref, sh_ref, w_ref, wg_ref, gb_ref, cc_ref, ss_ref,
                   main_ref, gate_ref, u_scr, *, tn, rope_lo, rope_hi):
    j = pl.program_id(1)

    @pl.when(j == 0)
    def _():
        ub = (x_ref[...] * (1.0 + sc_ref[...]) + sh_ref[...]).astype(BF16)
        u_scr[...] = ub
        z = _dot(ub, wg_ref[...]) + gb_ref[...]
        lane = _iota(z.shape, 1)
        log_sig = jnp.minimum(z, 0.0) - jnp.log1p(jnp.exp(-jnp.abs(z)))
        gate_ref[...] = jnp.where((lane >= H_M) & (lane < 2 * H_M), log_sig, z)

    acc = _dot(u_scr[...], w_ref[...])
    is_rope = (j >= rope_lo) & (j < rope_hi)

    @pl.when(is_rope)
    def _():
        cc = cc_ref[...]
        ss = ss_ref[...]
        for g in range(tn // HEAD_DIM):
            sl = slice(g * HEAD_DIM, (g + 1) * HEAD_DIM)
            a = acc[:, sl]
            main_ref[:, sl] = a * cc + pltpu.roll(a, HEAD_DIM // 2, 1) * ss

    @pl.when(jnp.logical_not(is_rope))
    def _():
        main_ref[...] = acc


def _inproj(x, sc, sh, w_main, w_gate, gate_bias, rope_cc, rope_ss, tm, tn=512):
    rows, d = x.shape
    n = w_main.shape[1]
    rm = sc.shape[0]
    mod_block = (1, d) if rm == 1 else (tm, d)
    mod_map = (lambda i, j: (0, 0)) if rm == 1 else (lambda i, j: (i, 0))
    rope_lo = (3 * W_HEADS) // tn
    rope_hi = (5 * W_HEADS) // tn
    kern = functools.partial(_inproj_kernel, tn=tn, rope_lo=rope_lo, rope_hi=rope_hi)
    return pl.pallas_call(
        kern,
        grid=(rows // tm, n // tn),
        in_specs=[
            pl.BlockSpec((tm, d), lambda i, j: (i, 0)),
            pl.BlockSpec(mod_block, mod_map),
            pl.BlockSpec(mod_block, mod_map),
            pl.BlockSpec((d, tn), lambda i, j: (0, j)),
            pl.BlockSpec((d, N_GATE_PAD), lambda i, j: (0, 0)),
            pl.BlockSpec((1, N_GATE_PAD), lambda i, j: (0, 0)),
            pl.BlockSpec((tm, HEAD_DIM), lambda i, j: (i, 0)),
            pl.BlockSpec((tm, HEAD_DIM), lambda i, j: (i, 0)),
        ],
        out_specs=[
            pl.BlockSpec((tm, tn), lambda i, j: (i, j)),
            pl.BlockSpec((tm, N_GATE_PAD), lambda i, j: (i, 0)),
        ],
        out_shape=[
            jax.ShapeDtypeStruct((rows, n), F32),
            jax.ShapeDtypeStruct((rows, N_GATE_PAD), F32),
        ],
        scratch_shapes=[pltpu.VMEM((tm, d), BF16)],
        compiler_params=_cparams("arbitrary", "arbitrary"),
        name="inproj",
    )(x, sc, sh, w_main, w_gate, gate_bias, rope_cc, rope_ss)


def _mlstm_kernel(q_ref, k_ref, v_ref, g_ref, c0_ref, n0_ref, m0_ref,
                  h_ref, c_ref, n_ref, m_ref, *, L):
    @pl.when(pl.program_id(1) == 0)
    def _():
        c_ref[...] = c0_ref[...]
        n_ref[...] = n0_ref[...]
        m_ref[...] = m0_ref[...]

    gates = g_ref[...]
    row = _iota((L, L), 0)
    col = _iota((L, L), 1)
    causal = row >= col
    tril = causal.astype(F32)
    triu = (row <= col).astype(F32)
    eye = (_iota((2 * H_M, N_GATE_PAD), 0) == _iota((2 * H_M, N_GATE_PAD), 1)).astype(F32)
    gates_t = _dot_nt(eye, gates, precision=HIGHEST)
    bcol_all = jnp.dot(tril, gates, precision=HIGHEST, preferred_element_type=F32)
    brow_all = jnp.dot(gates_t, triu, precision=HIGHEST, preferred_element_type=F32)

    for h in range(H_M):
        b_col = bcol_all[:, H_M + h:H_M + h + 1]
        b_row = brow_all[H_M + h:H_M + h + 1, :]
        ig_row = gates_t[h:h + 1, :]
        ig_col = gates[:, h:h + 1]
        m0 = m_ref[:, h:h + 1]
        c0 = c_ref[h]
        n0 = n_ref[h:h + 1, :]

        dmat = jnp.where(causal, b_col - b_row + ig_row, -jnp.inf)
        m_inter = b_col + m0
        m = jnp.maximum(m_inter, jnp.max(dmat, axis=1, keepdims=True))
        w = jnp.exp(dmat - m)
        g = jnp.exp(m_inter - m)

        q = q_ref[:, h * DK_M:(h + 1) * DK_M]
        k = k_ref[:, h * DK_M:(h + 1) * DK_M] * (DK_M ** -0.5)
        v = v_ref[:, h * DV_M:(h + 1) * DV_M]
        qb = q.astype(BF16)
        kb = k.astype(BF16)
        vb = v.astype(BF16)

        s = _dot_nt(qb, kb) * w
        num = g * _dot(qb, c0.astype(BF16)) + _dot(s.astype(BF16), vb)
        den = g * jnp.sum(q * n0, axis=1, keepdims=True) + jnp.sum(s, axis=1, keepdims=True)
        h_ref[:, h * DV_M:(h + 1) * DV_M] = num / jnp.maximum(jnp.abs(den), jnp.exp(-m))

        b_last = b_col[L - 1:L, :]
        a_col = b_last - b_col + ig_col
        m_end = jnp.maximum(b_last + m0, jnp.max(a_col, axis=0, keepdims=True))
        ws = jnp.exp(a_col - m_end)
        g_end = jnp.exp(b_last + m0 - m_end)
        kw = ws * k
        c_ref[h] = g_end * c0 + _dot_tn(kw.astype(BF16), vb)
        n_ref[h:h + 1, :] = g_end * n0 + jnp.sum(kw, axis=0, keepdims=True)
        m_ref[:, h:h + 1] = m_end


def _mlstm(main, gates, c0, n0, m0, batch, L):
    rows = main.shape[0]
    nc = rows // (batch * L)
    kern = functools.partial(_mlstm_kernel, L=L)
    state_map = lambda b, c: (b, 0, 0, 0)
    return pl.pallas_call(
        kern,
        grid=(batch, nc),
        in_specs=[
            pl.BlockSpec((L, H_M * DK_M), lambda b, c: (b * nc + c, 0)),
            pl.BlockSpec((L, H_M * DK_M), lambda b, c: (b * nc + c, 1)),
            pl.BlockSpec((L, H_M * DV_M), lambda b, c: (b * nc + c, 1)),
            pl.BlockSpec((L, N_GATE_PAD), lambda b, c: (b * nc + c, 0)),
            pl.BlockSpec((None, H_M, DK_M, DV_M), state_map),
            pl.BlockSpec((None, H_M, DK_M), lambda b, c: (b, 0, 0)),
            pl.BlockSpec((None, 1, H_M), lambda b, c: (b, 0, 0)),
        ],
        out_specs=[
            pl.BlockSpec((L, H_M * DV_M), lambda b, c: (b * nc + c, 0)),
            pl.BlockSpec((None, H_M, DK_M, DV_M), state_map),
            pl.BlockSpec((None, H_M, DK_M), lambda b, c: (b, 0, 0)),
            pl.BlockSpec((None, 1, H_M), lambda b, c: (b, 0, 0)),
        ],
        out_shape=[
            jax.ShapeDtypeStruct((rows, H_M * DV_M), F32),
            jax.ShapeDtypeStruct((batch, H_M, DK_M, DV_M), F32),
            jax.ShapeDtypeStruct((batch, H_M, DK_M), F32),
            jax.ShapeDtypeStruct((batch, 1, H_M), F32),
        ],
        compiler_params=_cparams("arbitrary", "arbitrary"),
        name="mlstm",
    )(main, main, main, gates, c0, n0, m0.reshape(batch, 1, H_M))


def _attn_branch_kernel(q_ref, kp_ref, kc_ref, vp_ref, vc_ref, num_ref, m_ref, den_ref):
    n = pl.program_id(1)
    shape = (ATT_BLOCK, ATT_BLOCK)
    qi = _iota(shape, 0)
    ki = _iota(shape, 1)
    mask_prev = (ki >= qi) & (n > 0)
    mask_cur = ki <= qi
    m_slab = jnp.zeros(shape, F32)
    d_slab = jnp.zeros(shape, F32)
    for h in range(H_A):
        sl = slice(h * HEAD_DIM, (h + 1) * HEAD_DIM)
        q = (q_ref[:, sl] * (HEAD_DIM ** -0.5)).astype(BF16)
        sp = jnp.where(mask_prev, _dot_nt(q, kp_ref[:, sl].astype(BF16)), -jnp.inf)
        sc = jnp.where(mask_cur, _dot_nt(q, kc_ref[:, sl].astype(BF16)), -jnp.inf)
        m = jnp.maximum(jnp.max(sp, axis=1, keepdims=True), jnp.max(sc, axis=1, keepdims=True))
        pp = jnp.exp(sp - m)
        pc = jnp.exp(sc - m)
        den = jnp.sum(pp, axis=1, keepdims=True) + jnp.sum(pc, axis=1, keepdims=True)
        num_ref[:, sl] = (_dot(pp.astype(BF16), vp_ref[:, sl].astype(BF16))
                          + _dot(pc.astype(BF16), vc_ref[:, sl].astype(BF16)))
        m_slab = jnp.where(ki == h, m, m_slab)
        d_slab = jnp.where(ki == h, den, d_slab)
    m_ref[...] = m_slab
    den_ref[...] = d_slab


def _attn_branch(main, dil):
    s, n_main = main.shape
    u = s // dil
    nb = u // ATT_BLOCK
    per_row = n_main // W_HEADS
    view = main.reshape(u, dil * n_main)
    blk = (ATT_BLOCK, W_HEADS)
    q_map = lambda r, n: (n, r * per_row + 3)
    kc_map = lambda r, n: (n, r * per_row + 4)
    kp_map = lambda r, n: (jnp.maximum(n - 1, 0), r * per_row + 4)
    vc_map = lambda r, n: (n, r * per_row + 5)
    vp_map = lambda r, n: (jnp.maximum(n - 1, 0), r * per_row + 5)
    num, m, den = pl.pallas_call(
        _attn_branch_kernel,
        grid=(dil, nb),
        in_specs=[
            pl.BlockSpec(blk, q_map),
            pl.BlockSpec(blk, kp_map),
            pl.BlockSpec(blk, kc_map),
            pl.BlockSpec(blk, vp_map),
            pl.BlockSpec(blk, vc_map),
        ],
        out_specs=[
            pl.BlockSpec(blk, lambda r, n: (n, r)),
            pl.BlockSpec((ATT_BLOCK, HEAD_DIM), lambda r, n: (n, r)),
            pl.BlockSpec((ATT_BLOCK, HEAD_DIM), lambda r, n: (n, r)),
        ],
        out_shape=[
            jax.ShapeDtypeStruct((u, dil * W_HEADS), F32),
            jax.ShapeDtypeStruct((u, dil * HEAD_DIM), F32),
            jax.ShapeDtypeStruct((u, dil * HEAD_DIM), F32),
        ],
        compiler_params=_cparams("arbitrary", "arbitrary"),
        name=f"attn_dil{dil}",
    )(view, view, view, view, view)
    return num.reshape(s, W_HEADS), m.reshape(s, HEAD_DIM), den.reshape(s, HEAD_DIM)


def _combine_kernel(*refs):
    nbr = len(DILATIONS)
    num_refs, m_refs, den_refs, o_ref = refs[:nbr], refs[nbr:2 * nbr], refs[2 * nbr:3 * nbr], refs[3 * nbr]
    for h in range(H_A):
        sl = slice(h * HEAD_DIM, (h + 1) * HEAD_DIM)
        ms = [r[:, h:h + 1] for r in m_refs]
        big = ms[0]
        for mm in ms[1:]:
            big = jnp.maximum(big, mm)
        w0 = jnp.exp(ms[0] - big)
        acc_num = w0 * num_refs[0][:, sl]
        acc_den = w0 * den_refs[0][:, h:h + 1]
        for b in range(1, nbr):
            w = jnp.exp(ms[b] - big)
            acc_num = acc_num + w * num_refs[b][:, sl]
            acc_den = acc_den + w * den_refs[b][:, h:h + 1]
        o_ref[:, sl] = acc_num / acc_den


def _combine(parts, tm=512):
    s = parts[0][0].shape[0]
    nums = [p[0] for p in parts]
    ms = [p[1] for p in parts]
    dens = [p[2] for p in parts]
    wide = pl.BlockSpec((tm, W_HEADS), lambda i: (i, 0))
    slab = pl.BlockSpec((tm, HEAD_DIM), lambda i: (i, 0))
    return pl.pallas_call(
        _combine_kernel,
        grid=(s // tm,),
        in_specs=[wide] * len(nums) + [slab] * (len(ms) + len(dens)),
        out_specs=wide,
        out_shape=jax.ShapeDtypeStruct((s, W_HEADS), F32),
        compiler_params=_cparams("arbitrary"),
        name="attn_combine",
    )(*nums, *ms, *dens)


def _branch_count(delta):
    cnt = jnp.zeros(delta.shape, F32)
    for window, dil in DILATIONS:
        hit = (delta >= 0) & (delta <= window) & ((delta & (dil - 1)) == 0)
        cnt = cnt + hit.astype(F32)
    return cnt


def _attn_sample_kernel(q_ref, kn_ref, vn_ref, ck_ref, cv_ref, o_ref, *, T, NB):
    cnt_c = _branch_count(NB + _iota((T, NB), 0) - _iota((T, NB), 1))
    cnt_n = _branch_count(_iota((T, T), 0) - _iota((T, T), 1))
    for h in range(H_A):
        sl = slice(h * HEAD_DIM, (h + 1) * HEAD_DIM)
        q = (q_ref[:, sl] * (HEAD_DIM ** -0.5)).astype(BF16)
        s_c = jnp.where(cnt_c > 0, _dot_nt(q, ck_ref[:, sl].astype(BF16)), -jnp.inf)
        s_n = jnp.where(cnt_n > 0, _dot_nt(q, kn_ref[:, sl].astype(BF16)), -jnp.inf)
        big = jnp.maximum(jnp.max(s_c, axis=1, keepdims=True), jnp.max(s_n, axis=1, keepdims=True))
        p_c = cnt_c * jnp.exp(s_c - big)
        p_n = cnt_n * jnp.exp(s_n - big)
        den = jnp.sum(p_c, axis=1, keepdims=True) + jnp.sum(p_n, axis=1, keepdims=True)
        num = (_dot(p_c.astype(BF16), cv_ref[:, sl].astype(BF16))
               + _dot(p_n.astype(BF16), vn_ref[:, sl].astype(BF16)))
        o_ref[:, sl] = num / den


def _attn_sample(main, cache_k, cache_v, batch, T):
    nb = cache_k.shape[1]
    ck = cache_k.reshape(batch, nb, W_HEADS)
    cv = cache_v.reshape(batch, nb, W_HEADS)
    kern = functools.partial(_attn_sample_kernel, T=T, NB=nb)
    return pl.pallas_call(
        kern,
        grid=(batch,),
        in_specs=[
            pl.BlockSpec((T, W_HEADS), lambda b: (b, 3)),
            pl.BlockSpec((T, W_HEADS), lambda b: (b, 4)),
            pl.BlockSpec((T, W_HEADS), lambda b: (b, 5)),
            pl.BlockSpec((None, nb, W_HEADS), lambda b: (b, 0, 0)),
            pl.BlockSpec((None, nb, W_HEADS), lambda b: (b, 0, 0)),
        ],
        out_specs=pl.BlockSpec((T, W_HEADS), lambda b: (b, 0)),
        out_shape=jax.ShapeDtypeStruct((batch * T, W_HEADS), F32),
        compiler_params=_cparams("arbitrary"),
        name="attn_sample",
    )(main, main, main, ck, cv)


def _layer_norm(y, g, b):
    mu = jnp.mean(y, axis=1, keepdims=True)
    yc = y - mu
    var = jnp.mean(yc * yc, axis=1, keepdims=True)
    return yc * lax.rsqrt(var + LN_EPS) * g + b


def _merge_kernel(h_ref, om_ref, att_ref, x_ref, gt_ref, gh_ref, w_ref, lg_ref, lb_ref,
                  o_ref, cat_scr, *, alpha):
    for h in range(H_M):
        sl = slice(h * DV_M, (h + 1) * DV_M)
        hh = h_ref[:, sl]
        hn = hh * lax.rsqrt(jnp.mean(hh * hh, axis=1, keepdims=True) + HEAD_NORM_EPS) * gh_ref[:, sl]
        cat_scr[:, sl] = (hn * jax.nn.sigmoid(om_ref[:, sl])).astype(BF16)
    cat_scr[:, H_M * DV_M:] = att_ref[...].astype(BF16)
    mix = _dot(cat_scr[...], w_ref[...])
    y = alpha * x_ref[...] + (1.0 + gt_ref[...]) * mix
    o_ref[...] = _layer_norm(y, lg_ref[...], lb_ref[...])


def _merge(h, main, att, x, gt, g_head, w_out, ln_g, ln_b, alpha, tm):
    rows, d = x.shape
    rm = gt.shape[0]
    mod_block = (1, d) if rm == 1 else (tm, d)
    mod_map = (lambda i: (0, 0)) if rm == 1 else (lambda i: (i, 0))
    wide = pl.BlockSpec((tm, W_HEADS), lambda i: (i, 0))
    const = lambda shape: pl.BlockSpec(shape, lambda i: (0, 0))
    return pl.pallas_call(
        functools.partial(_merge_kernel, alpha=alpha),
        grid=(rows // tm,),
        in_specs=[
            wide,
            pl.BlockSpec((tm, W_HEADS), lambda i: (i, 2)),
            wide,
            pl.BlockSpec((tm, d), lambda i: (i, 0)),
            pl.BlockSpec(mod_block, mod_map),
            const((1, W_HEADS)),
            const((d, d)),
            const((1, d)),
            const((1, d)),
        ],
        out_specs=pl.BlockSpec((tm, d), lambda i: (i, 0)),
        out_shape=jax.ShapeDtypeStruct((rows, d), F32),
        scratch_shapes=[pltpu.VMEM((tm, d), BF16)],
        compiler_params=_cparams("arbitrary"),
        name="mixer_merge",
    )(h, main, att, x, gt, g_head, w_out, ln_g, ln_b)


def _ffn_kernel(*refs, alpha, seq_len, n_chunks, carry):
    if carry:
        (x_ref, sc_ref, sh_ref, gt_ref, wg_ref, wv_ref, wd_ref, cw_ref, cb_ref, lg_ref, lb_ref,
         o_ref, tail_ref, u_scr, acc_scr, carry_scr) = refs
    else:
        (x_ref, sc_ref, sh_ref, gt_ref, wg_ref, wv_ref, wd_ref, cw_ref, cb_ref, lg_ref, lb_ref,
         h0_ref, h1_ref, o_ref, tail_ref, u_scr, acc_scr) = refs
    i = pl.program_id(0)
    j = pl.program_id(1)

    @pl.when(j == 0)
    def _():
        u_scr[...] = (x_ref[...] * (1.0 + sc_ref[...]) + sh_ref[...]).astype(BF16)
        acc_scr[...] = jnp.zeros_like(acc_scr)

    ub = u_scr[...]
    g = _dot(ub, wg_ref[...])
    v = _dot(ub, wv_ref[...])
    tm = g.shape[0]
    row = _iota(g.shape, 0)
    if carry:
        @pl.when(i == 0)
        def _():
            carry_scr[j] = jnp.zeros(carry_scr.shape[1:], F32)

        tail = carry_scr[j]
        hist0, hist1 = tail[6:7, :], tail[7:8, :]
        pos = row
        carry_scr[j] = g[tm - 8:, :]
        tail_ref[...] = g[tm - 8:, :]
    else:
        hist0, hist1 = h0_ref[...], h1_ref[...]
        pos = row & (seq_len - 1)
        tail_ref[...] = g
    prev1 = jnp.where(pos == 0, hist1, pltpu.roll(g, 1, 0))
    prev2 = jnp.where(pos == 0, hist0, jnp.where(pos == 1, hist1, pltpu.roll(g, 2, 0)))
    cw = cw_ref[...]
    a = prev2 * cw[0:1, :] + prev1 * cw[1:2, :] + g * cw[2:3, :] + cb_ref[...]
    a = a * jax.nn.sigmoid(a) * v
    acc_scr[...] += _dot(a.astype(BF16), wd_ref[...])

    @pl.when(j == n_chunks - 1)
    def _():
        y = alpha * x_ref[...] + (1.0 + gt_ref[...]) * acc_scr[...]
        o_ref[...] = _layer_norm(y, lg_ref[...], lb_ref[...])


def _ffn(x, sc, sh, gt, w_up, w_down, conv_w, conv_b, ln_g, ln_b, alpha, tm, hist=None, seq_len=None, tf=512):
    rows, d = x.shape
    d_ff = w_down.shape[0]
    nj = d_ff // tf
    rm = sc.shape[0]
    mod_block = (1, d) if rm == 1 else (tm, d)
    mod_map = (lambda i, j: (0, 0)) if rm == 1 else (lambda i, j: (i, 0))
    carry = hist is None
    in_specs = [
        pl.BlockSpec((tm, d), lambda i, j: (i, 0)),
        pl.BlockSpec(mod_block, mod_map),
        pl.BlockSpec(mod_block, mod_map),
        pl.BlockSpec(mod_block, mod_map),
        pl.BlockSpec((d, tf), lambda i, j: (0, j)),
        pl.BlockSpec((d, tf), lambda i, j: (0, nj + j)),
        pl.BlockSpec((tf, d), lambda i, j: (j, 0)),
        pl.BlockSpec((CONV_W, tf), lambda i, j: (0, j)),
        pl.BlockSpec((1, tf), lambda i, j: (0, j)),
        pl.BlockSpec((1, d), lambda i, j: (0, 0)),
        pl.BlockSpec((1, d), lambda i, j: (0, 0)),
    ]
    args = [x, sc, sh, gt, w_up, w_up, w_down, conv_w, conv_b, ln_g, ln_b]
    scratch = [pltpu.VMEM((tm, d), BF16), pltpu.VMEM((tm, d), F32)]
    if carry:
        tail_rows = 8
        scratch.append(pltpu.VMEM((nj, 8, tf), F32))
    else:
        tail_rows = tm
        in_specs += [pl.BlockSpec((tm, tf), lambda i, j: (i, j))] * 2
        args += list(hist)
    kern = functools.partial(_ffn_kernel, alpha=alpha, seq_len=seq_len, n_chunks=nj, carry=carry)
    return pl.pallas_call(
        kern,
        grid=(rows // tm, nj),
        in_specs=in_specs,
        out_specs=[
            pl.BlockSpec((tm, d), lambda i, j: (i, 0)),
            pl.BlockSpec((tail_rows, tf), lambda i, j: (i, j)),
        ],
        out_shape=[
            jax.ShapeDtypeStruct((rows, d), F32),
            jax.ShapeDtypeStruct((rows // tm * tail_rows, d_ff), F32),
        ],
        scratch_shapes=scratch,
        compiler_params=_cparams("arbitrary", "arbitrary"),
        name="conv_ffn",
    )(*args)


def _rope_tables(pos):
    half = HEAD_DIM // 2
    inv = ROPE_THETA ** (-jnp.arange(half, dtype=F32) / half)
    ang = pos.astype(F32)[:, None] * inv[None, :]
    cos, sin = jnp.cos(ang), jnp.sin(ang)
    return jnp.concatenate([cos, cos], -1), jnp.concatenate([-sin, sin], -1)


def kernel(x_prompt, x_sample, cache_k_win, cache_v_win, state_C, state_n, state_m, state_conv,
           c_prompt, c_sample, w_ada, b_ada, w_in, b_gate, g_head, w_out, ln1_g, ln1_b,
           w_up, conv_w, conv_b, w_down, ln2_g, ln2_b):
    bp, s, d = x_prompt.shape
    bs, t, _ = x_sample.shape
    depth = w_in.shape[0]
    d_ff = w_down.shape[1]
    alpha = (2 * depth) ** 0.25
    assert bp == 1 and d == (H_M + H_A) * HEAD_DIM

    n_c = bp + bs
    pad = (-n_c) % 8
    c_all = jnp.concatenate([c_prompt, c_sample, jnp.zeros((pad, d), F32)], 0)
    mod = _modulation(c_all, w_ada, b_ada)

    n_m = 2 * H_M * DK_M + 2 * H_M * DV_M
    w_main = jnp.concatenate([w_in[:, :, :n_m], w_in[:, :, n_m + 2 * H_M:]], -1).astype(BF16)
    w_gate = jnp.pad(w_in[:, :, n_m:n_m + 2 * H_M], ((0, 0), (0, 0), (0, N_GATE_PAD - 2 * H_M))).astype(BF16)
    gate_bias = jnp.pad(b_gate.reshape(depth, 1, 2 * H_M), ((0, 0), (0, 0), (0, N_GATE_PAD - 2 * H_M)))
    w_out_b = w_out.astype(BF16)
    w_up_b = w_up.astype(BF16)
    w_down_b = w_down.astype(BF16)

    cc_p, ss_p = _rope_tables(jnp.arange(s, dtype=jnp.int32))
    pos_s = PAST_LEN + jnp.arange(t, dtype=jnp.int32)
    cc_s, ss_s = (jnp.tile(a, (bs, 1)) for a in _rope_tables(pos_s))

    xp = x_prompt.reshape(s, d)
    xs = x_sample.reshape(bs * t, d)
    rows_s = bs * t
    tm_p = 512
    outs = [[] for _ in range(12)]
    zeros_c = jnp.zeros((bp, H_M, DK_M, DV_M), F32)
    zeros_n = jnp.zeros((bp, H_M, DK_M), F32)
    zeros_m = jnp.zeros((bp, H_M), F32)

    for l in range(depth):
        row2 = lambda v: v.reshape(1, -1)
        mod_p = [mod[l, 0:1, k * d:(k + 1) * d] for k in range(6)]
        mod_s = [jnp.repeat(mod[l, bp:bp + bs, k * d:(k + 1) * d], t, axis=0) for k in range(6)]

        main_p, gates_p = _inproj(xp, mod_p[1], mod_p[0], w_main[l], w_gate[l], gate_bias[l], cc_p, ss_p, tm=tm_p)
        h_p, c_p, n_p, m_p = _mlstm(main_p, gates_p, zeros_c, zeros_n, zeros_m, batch=bp, L=128)
        att_p = _combine([_attn_branch(main_p, dil) for _, dil in DILATIONS])
        x1_p = _merge(h_p, main_p, att_p, xp, mod_p[2], row2(g_head[l]), w_out_b[l],
                      row2(ln1_g[l]), row2(ln1_b[l]), alpha, tm=256)
        xp, tail_p = _ffn(x1_p, mod_p[4], mod_p[3], mod_p[5], w_up_b[l], w_down_b[l], conv_w[l],
                          row2(conv_b[l]), row2(ln2_g[l]), row2(ln2_b[l]), alpha, tm=tm_p)

        main_s, gates_s = _inproj(xs, mod_s[1], mod_s[0], w_main[l], w_gate[l], gate_bias[l], cc_s, ss_s, tm=rows_s)
        h_s, c_s, n_s, m_s = _mlstm(main_s, gates_s, state_C[l], state_n[l], state_m[l], batch=bs, L=t)
        att_s = _attn_sample(main_s, cache_k_win[l], cache_v_win[l], bs, t)
        x1_s = _merge(h_s, main_s, att_s, xs, mod_s[2], row2(g_head[l]), w_out_b[l],
                      row2(ln1_g[l]), row2(ln1_b[l]), alpha, tm=rows_s)
        hist = [jnp.repeat(state_conv[l][:, r, :], t, axis=0) for r in range(CONV_W - 1)]
        xs, g_s = _ffn(x1_s, mod_s[4], mod_s[3], mod_s[5], w_up_b[l], w_down_b[l], conv_w[l],
                       row2(conv_b[l]), row2(ln2_g[l]), row2(ln2_b[l]), alpha, tm=rows_s,
                       hist=hist, seq_len=t)

        wp = min(DILATIONS[-1][0], s)
        k_cols = slice(4 * W_HEADS, 5 * W_HEADS)
        v_cols = slice(5 * W_HEADS, 6 * W_HEADS)
        outs[0].append(main_p[s - wp:, k_cols].reshape(bp, wp, H_A, HEAD_DIM))
        outs[1].append(main_p[s - wp:, v_cols].reshape(bp, wp, H_A, HEAD_DIM))
        outs[2].append(c_p)
        outs[3].append(n_p)
        outs[4].append(m_p.reshape(bp, H_M))
        outs[5].append(tail_p[-8:][8 - (CONV_W - 1):].reshape(bp, CONV_W - 1, d_ff))
        outs[6].append(main_s[:, k_cols].reshape(bs, t, H_A, HEAD_DIM))
        outs[7].append(main_s[:, v_cols].reshape(bs, t, H_A, HEAD_DIM))
        outs[8].append(c_s)
        outs[9].append(n_s)
        outs[10].append(m_s.reshape(bs, H_M))
        outs[11].append(g_s.reshape(bs, t, d_ff)[:, t - (CONV_W - 1):])

    return (xp.reshape(bp, s, d), xs.reshape(bs, t, d)) + tuple(jnp.stack(o) for o in outs)
```

```python
import functools

import jax
import jax.numpy as jnp
from jax import lax
from jax.experimental import pallas as pl
from jax.experimental.pallas import tpu as pltpu

F32 = jnp.float32
BF16 = jnp.bfloat16
HIGHEST = lax.Precision.HIGHEST

HEAD_DIM = 128
H_M = 8
H_A = 8
DK_M = 64
DV_M = 128
DILATIONS = ((128, 1), (512, 4), (2048, 16))
ATT_BLOCK = 128
PAST_LEN = 8192
ROPE_THETA = 10000.0
CONV_W = 3
LN_EPS = 1e-5
HEAD_NORM_EPS = 1e-6
N_GATE_PAD = 128
W_HEADS = H_A * HEAD_DIM
ATT_SUPER = ATT_BLOCK * DILATIONS[-1][1]

VMEM_LIMIT_BYTES = 56 * 1024 * 1024


def _cparams(*sem):
    return pltpu.CompilerParams(dimension_semantics=sem, vmem_limit_bytes=VMEM_LIMIT_BYTES)


def _dot(a, b):
    return jnp.dot(a, b, preferred_element_type=F32)


def _dot_nt(a, b, precision=None):
    return lax.dot_general(a, b, (((1,), (1,)), ((), ())), precision=precision,
                           preferred_element_type=F32)


def _dot_tn(a, b):
    return lax.dot_general(a, b, (((0,), (0,)), ((), ())), preferred_element_type=F32)


def _iota(shape, dim):
    return lax.broadcasted_iota(jnp.int32, shape, dim)


def _mod_kernel(c_ref, w_ref, b_ref, o_ref):
    c = c_ref[...]
    a = (c * jax.nn.sigmoid(c)).astype(BF16)
    o_ref[...] = _dot(a, w_ref[...].astype(BF16)) + b_ref[...]


def _modulation(c_all, w_ada, b_ada, tn=1024):
    depth, d, n = w_ada.shape
    rows = c_all.shape[0]
    return pl.pallas_call(
        _mod_kernel,
        grid=(depth, n // tn),
        in_specs=[
            pl.BlockSpec((rows, d), lambda l, j: (0, 0)),
            pl.BlockSpec((None, d, tn), lambda l, j: (l, 0, j)),
            pl.BlockSpec((None, 1, tn), lambda l, j: (l, 0, j)),
        ],
        out_specs=pl.BlockSpec((None, rows, tn), lambda l, j: (l, 0, j)),
        out_shape=jax.ShapeDtypeStruct((depth, rows, n), F32),
        compiler_params=_cparams("arbitrary", "arbitrary"),
        name="adaln_mod",
    )(c_all, w_ada, b_ada.reshape(depth, 1, n))


def _inproj_kernel(x_ref, sc_ref, sh_ref, w_ref, wg_ref, gb_ref, cc_ref, ss_ref,
                   main_ref, gate_ref, u_scr, *, tn, rope_lo, rope_hi):
    j = pl.program_id(1)

    @pl.when(j == 0)
    def _():
        ub = (x_ref[...] * (1.0 + sc_ref[...]) + sh_ref[...]).astype(BF16)
        u_scr[...] = ub
        z = _dot(ub, wg_ref[...]) + gb_ref[...]
        lane = _iota(z.shape, 1)
        log_sig = jnp.minimum(z, 0.0) - jnp.log1p(jnp.exp(-jnp.abs(z)))
        gate_ref[...] = jnp.where((lane >= H_M) & (lane < 2 * H_M), log_sig, z)

    acc = _dot(u_scr[...], w_ref[...])
    is_rope = (j >= rope_lo) & (j < rope_hi)

    @pl.when(is_rope)
    def _():
        cc = cc_ref[...]
        ss = ss_ref[...]
        for g in range(tn // HEAD_DIM):
            sl = slice(g * HEAD_DIM, (g + 1) * HEAD_DIM)
            a = acc[:, sl]
            main_ref[:, sl] = a * cc + pltpu.roll(a, HEAD_DIM // 2, 1) * ss

    @pl.when(jnp.logical_not(is_rope))
    def _():
        main_ref[...] = acc


def _inproj(x, sc, sh, w_main, w_gate, gate_bias, rope_cc, rope_ss, layer, tm, tn=512):
    rows, d = x.shape
    n = w_main.shape[2]
    rm = sc.shape[0]
    mod_block = (1, d) if rm == 1 else (tm, d)
    mod_map = (lambda i, j: (0, 0)) if rm == 1 else (lambda i, j: (i, 0))
    rope_lo = (3 * W_HEADS) // tn
    rope_hi = (5 * W_HEADS) // tn
    kern = functools.partial(_inproj_kernel, tn=tn, rope_lo=rope_lo, rope_hi=rope_hi)
    return pl.pallas_call(
        kern,
        grid=(rows // tm, n // tn),
        in_specs=[
            pl.BlockSpec((tm, d), lambda i, j: (i, 0)),
            pl.BlockSpec(mod_block, mod_map),
            pl.BlockSpec(mod_block, mod_map),
            pl.BlockSpec((None, d, tn), lambda i, j: (layer, 0, j)),
            pl.BlockSpec((None, d, N_GATE_PAD), lambda i, j: (layer, 0, 0)),
            pl.BlockSpec((None, 1, N_GATE_PAD), lambda i, j: (layer, 0, 0)),
            pl.BlockSpec((tm, HEAD_DIM), lambda i, j: (i, 0)),
            pl.BlockSpec((tm, HEAD_DIM), lambda i, j: (i, 0)),
        ],
        out_specs=[
            pl.BlockSpec((tm, tn), lambda i, j: (i, j)),
            pl.BlockSpec((tm, N_GATE_PAD), lambda i, j: (i, 0)),
        ],
        out_shape=[
            jax.ShapeDtypeStruct((rows, n), F32),
            jax.ShapeDtypeStruct((rows, N_GATE_PAD), F32),
        ],
        scratch_shapes=[pltpu.VMEM((tm, d), BF16)],
        compiler_params=_cparams("arbitrary", "arbitrary"),
        name="inproj",
    )(x, sc, sh, w_main, w_gate, gate_bias, rope_cc, rope_ss)


def _mlstm_kernel(q_ref, k_ref, v_ref, g_ref, c0_ref, n0_ref, m0_ref,
                  h_ref, c_ref, n_ref, m_ref, *, L):
    @pl.when(pl.program_id(1) == 0)
    def _():
        c_ref[...] = c0_ref[...]
        n_ref[...] = n0_ref[...]
        m_ref[...] = m0_ref[...]

    gates = g_ref[...]
    row = _iota((L, L), 0)
    col = _iota((L, L), 1)
    causal = row >= col
    tril = causal.astype(F32)
    triu = (row <= col).astype(F32)
    eye = (_iota((2 * H_M, N_GATE_PAD), 0) == _iota((2 * H_M, N_GATE_PAD), 1)).astype(F32)
    gates_t = _dot_nt(eye, gates, precision=HIGHEST)
    bcol_all = jnp.dot(tril, gates, precision=HIGHEST, preferred_element_type=F32)
    brow_all = jnp.dot(gates_t, triu, precision=HIGHEST, preferred_element_type=F32)

    for h in range(H_M):
        b_col = bcol_all[:, H_M + h:H_M + h + 1]
        b_row = brow_all[H_M + h:H_M + h + 1, :]
        ig_row = gates_t[h:h + 1, :]
        ig_col = gates[:, h:h + 1]
        m0 = m_ref[:, h:h + 1]
        c0 = c_ref[h]
        n0 = n_ref[h:h + 1, :]

        dmat = jnp.where(causal, b_col - b_row + ig_row, -jnp.inf)
        m_inter = b_col + m0
        m = jnp.maximum(m_inter, jnp.max(dmat, axis=1, keepdims=True))
        w = jnp.exp(dmat - m)
        g = jnp.exp(m_inter - m)

        q = q_ref[:, h * DK_M:(h + 1) * DK_M]
        k = k_ref[:, h * DK_M:(h + 1) * DK_M] * (DK_M ** -0.5)
        v = v_ref[:, h * DV_M:(h + 1) * DV_M]
        qb = q.astype(BF16)
        kb = k.astype(BF16)
        vb = v.astype(BF16)

        s = _dot_nt(qb, kb) * w
        num = g * _dot(qb, c0.astype(BF16)) + _dot(s.astype(BF16), vb)
        den = g * jnp.sum(q * n0, axis=1, keepdims=True) + jnp.sum(s, axis=1, keepdims=True)
        h_ref[:, h * DV_M:(h + 1) * DV_M] = num / jnp.maximum(jnp.abs(den), jnp.exp(-m))

        b_last = b_col[L - 1:L, :]
        a_col = b_last - b_col + ig_col
        m_end = jnp.maximum(b_last + m0, jnp.max(a_col, axis=0, keepdims=True))
        ws = jnp.exp(a_col - m_end)
        g_end = jnp.exp(b_last + m0 - m_end)
        kw = ws * k
        c_ref[h] = g_end * c0 + _dot_tn(kw.astype(BF16), vb)
        n_ref[h:h + 1, :] = g_end * n0 + jnp.sum(kw, axis=0, keepdims=True)
        m_ref[:, h:h + 1] = m_end


def _mlstm(main, gates, c0, n0, m0, batch, L):
    rows = main.shape[0]
    nc = rows // (batch * L)
    kern = functools.partial(_mlstm_kernel, L=L)
    state_map = lambda b, c: (b, 0, 0, 0)
    return pl.pallas_call(
        kern,
        grid=(batch, nc),
        in_specs=[
            pl.BlockSpec((L, H_M * DK_M), lambda b, c: (b * nc + c, 0)),
            pl.BlockSpec((L, H_M * DK_M), lambda b, c: (b * nc + c, 1)),
            pl.BlockSpec((L, H_M * DV_M), lambda b, c: (b * nc + c, 1)),
            pl.BlockSpec((L, N_GATE_PAD), lambda b, c: (b * nc + c, 0)),
            pl.BlockSpec((None, H_M, DK_M, DV_M), state_map),
            pl.BlockSpec((None, H_M, DK_M), lambda b, c: (b, 0, 0)),
            pl.BlockSpec((None, 1, H_M), lambda b, c: (b, 0, 0)),
        ],
        out_specs=[
            pl.BlockSpec((L, H_M * DV_M), lambda b, c: (b * nc + c, 0)),
            pl.BlockSpec((None, H_M, DK_M, DV_M), state_map),
            pl.BlockSpec((None, H_M, DK_M), lambda b, c: (b, 0, 0)),
            pl.BlockSpec((None, 1, H_M), lambda b, c: (b, 0, 0)),
        ],
        out_shape=[
            jax.ShapeDtypeStruct((rows, H_M * DV_M), F32),
            jax.ShapeDtypeStruct((batch, H_M, DK_M, DV_M), F32),
            jax.ShapeDtypeStruct((batch, H_M, DK_M), F32),
            jax.ShapeDtypeStruct((batch, 1, H_M), F32),
        ],
        compiler_params=_cparams("arbitrary", "arbitrary"),
        name="mlstm",
    )(main, main, main, gates, c0, n0, m0.reshape(batch, 1, H_M))


def _attn_prompt_kernel(q_ref, kp_ref, kc_ref, vp_ref, vc_ref, o_ref,
                        kk_scr, vv_scr, num_scr, m_scr, den_scr):
    sb = pl.program_id(0)
    SB = ATT_SUPER
    kk_scr[0:SB, :] = kp_ref[...]
    kk_scr[SB:, :] = kc_ref[...]
    vv_scr[0:SB, :] = vp_ref[...]
    vv_scr[SB:, :] = vc_ref[...]

    shape = (ATT_BLOCK, 2 * ATT_BLOCK)
    qi = _iota(shape, 0)
    ki = _iota(shape, 1)
    window = (ki >= qi) & (ki <= qi + ATT_BLOCK)
    window_first = window & ((ki >= ATT_BLOCK) | (sb > 0))

    for bi, (_, dil) in enumerate(DILATIONS):
        for n in range(SB // (ATT_BLOCK * dil)):
            valid = window_first if n == 0 else window

            def unit(r, carry, bi=bi, dil=dil, n=n, valid=valid):
                q_rows = pl.ds(r + n * ATT_BLOCK * dil, ATT_BLOCK, stride=dil)
                k_rows = pl.ds(SB + r + (n - 1) * ATT_BLOCK * dil, 2 * ATT_BLOCK, stride=dil)
                q = (q_ref[q_rows, :] * (HEAD_DIM ** -0.5)).astype(BF16)
                k = kk_scr[k_rows, :].astype(BF16)
                v = vv_scr[k_rows, :].astype(BF16)
                s = jnp.where(valid, _dot_nt(q, k), -jnp.inf)
                m = jnp.max(s, axis=1, keepdims=True)
                p = jnp.exp(s - m)
                den = jnp.sum(p, axis=1, keepdims=True)
                num_scr[bi, q_rows, :] = _dot(p.astype(BF16), v)
                m_scr[bi, q_rows, :] = jnp.broadcast_to(m, (ATT_BLOCK, HEAD_DIM))
                den_scr[bi, q_rows, :] = jnp.broadcast_to(den, (ATT_BLOCK, HEAD_DIM))
                return carry

            lax.fori_loop(0, dil, unit, 0)

    rows_per_step = 2 * ATT_BLOCK

    def mix(i, carry):
        rows = pl.ds(pl.multiple_of(i * rows_per_step, rows_per_step), rows_per_step)
        ms = [m_scr[b, rows, :] for b in range(len(DILATIONS))]
        big = ms[0]
        for mm in ms[1:]:
            big = jnp.maximum(big, mm)
        w0 = jnp.exp(ms[0] - big)
        acc_num = w0 * num_scr[0, rows, :]
        acc_den = w0 * den_scr[0, rows, :]
        for b in range(1, len(DILATIONS)):
            w = jnp.exp(ms[b] - big)
            acc_num = acc_num + w * num_scr[b, rows, :]
            acc_den = acc_den + w * den_scr[b, rows, :]
        o_ref[rows, :] = acc_num / acc_den
        return carry

    lax.fori_loop(0, SB // rows_per_step, mix, 0)


def _attn_prompt(main):
    s, n_main = main.shape
    SB = ATT_SUPER
    cols = n_main // HEAD_DIM // 6
    blk = (SB, HEAD_DIM)
    prev = lambda i: jnp.maximum(i - 1, 0)
    nbr = len(DILATIONS)
    return pl.pallas_call(
        _attn_prompt_kernel,
        grid=(s // SB, H_A),
        in_specs=[
            pl.BlockSpec(blk, lambda i, h: (i, 3 * cols + h)),
            pl.BlockSpec(blk, lambda i, h: (prev(i), 4 * cols + h)),
            pl.BlockSpec(blk, lambda i, h: (i, 4 * cols + h)),
            pl.BlockSpec(blk, lambda i, h: (prev(i), 5 * cols + h)),
            pl.BlockSpec(blk, lambda i, h: (i, 5 * cols + h)),
        ],
        out_specs=pl.BlockSpec(blk, lambda i, h: (i, h)),
        out_shape=jax.ShapeDtypeStruct((s, W_HEADS), F32),
        scratch_shapes=[
            pltpu.VMEM((2 * SB, HEAD_DIM), F32),
            pltpu.VMEM((2 * SB, HEAD_DIM), F32),
            pltpu.VMEM((nbr, SB, HEAD_DIM), F32),
            pltpu.VMEM((nbr, SB, HEAD_DIM), F32),
            pltpu.VMEM((nbr, SB, HEAD_DIM), F32),
        ],
        compiler_params=_cparams("arbitrary", "arbitrary"),
        name="attn_prompt",
    )(main, main, main, main, main)


def _branch_count(delta):
    cnt = jnp.zeros(delta.shape, F32)
    for window, dil in DILATIONS:
        hit = (delta >= 0) & (delta <= window) & ((delta & (dil - 1)) == 0)
        cnt = cnt + hit.astype(F32)
    return cnt


def _attn_sample_kernel(q_ref, kn_ref, vn_ref, ck_ref, cv_ref, o_ref, *, T, NB):
    cnt_c = _branch_count(NB + _iota((T, NB), 0) - _iota((T, NB), 1))
    cnt_n = _branch_count(_iota((T, T), 0) - _iota((T, T), 1))
    for h in range(H_A):
        sl = slice(h * HEAD_DIM, (h + 1) * HEAD_DIM)
        q = (q_ref[:, sl] * (HEAD_DIM ** -0.5)).astype(BF16)
        s_c = jnp.where(cnt_c > 0, _dot_nt(q, ck_ref[:, h, :].astype(BF16)), -jnp.inf)
        s_n = jnp.where(cnt_n > 0, _dot_nt(q, kn_ref[:, sl].astype(BF16)), -jnp.inf)
        big = jnp.maximum(jnp.max(s_c, axis=1, keepdims=True), jnp.max(s_n, axis=1, keepdims=True))
        p_c = cnt_c * jnp.exp(s_c - big)
        p_n = cnt_n * jnp.exp(s_n - big)
        den = jnp.sum(p_c, axis=1, keepdims=True) + jnp.sum(p_n, axis=1, keepdims=True)
        num = (_dot(p_c.astype(BF16), cv_ref[:, h, :].astype(BF16))
               + _dot(p_n.astype(BF16), vn_ref[:, sl].astype(BF16)))
        o_ref[:, sl] = num / den


def _attn_sample(main, cache_k, cache_v, layer, batch, T):
    nb = cache_k.shape[2]
    kern = functools.partial(_attn_sample_kernel, T=T, NB=nb)
    cache_spec = pl.BlockSpec((None, None, nb, H_A, HEAD_DIM), lambda b: (layer, b, 0, 0, 0))
    return pl.pallas_call(
        kern,
        grid=(batch,),
        in_specs=[
            pl.BlockSpec((T, W_HEADS), lambda b: (b, 3)),
            pl.BlockSpec((T, W_HEADS), lambda b: (b, 4)),
            pl.BlockSpec((T, W_HEADS), lambda b: (b, 5)),
            cache_spec,
            cache_spec,
        ],
        out_specs=pl.BlockSpec((T, W_HEADS), lambda b: (b, 0)),
        out_shape=jax.ShapeDtypeStruct((batch * T, W_HEADS), F32),
        compiler_params=_cparams("arbitrary"),
        name="attn_sample",
    )(main, main, main, cache_k, cache_v)


def _layer_norm(y, g, b):
    mu = jnp.mean(y, axis=1, keepdims=True)
    yc = y - mu
    var = jnp.mean(yc * yc, axis=1, keepdims=True)
    return yc * lax.rsqrt(var + LN_EPS) * g + b


def _merge_kernel(h_ref, om_ref, att_ref, x_ref, gt_ref, gh_ref, w_ref, lg_ref, lb_ref,
                  o_ref, cat_scr, *, alpha):
    for h in range(H_M):
        sl = slice(h * DV_M, (h + 1) * DV_M)
        hh = h_ref[:, sl]
        hn = hh * lax.rsqrt(jnp.mean(hh * hh, axis=1, keepdims=True) + HEAD_NORM_EPS) * gh_ref[:, sl]
        cat_scr[:, sl] = (hn * jax.nn.sigmoid(om_ref[:, sl])).astype(BF16)
    cat_scr[:, H_M * DV_M:] = att_ref[...].astype(BF16)
    mix = _dot(cat_scr[...], w_ref[...])
    y = alpha * x_ref[...] + (1.0 + gt_ref[...]) * mix
    o_ref[...] = _layer_norm(y, lg_ref[...], lb_ref[...])


def _merge(h, main, att, x, gt, g_head, w_out, ln_g, ln_b, layer, alpha, tm):
    rows, d = x.shape
    rm = gt.shape[0]
    mod_block = (1, d) if rm == 1 else (tm, d)
    mod_map = (lambda i: (0, 0)) if rm == 1 else (lambda i: (i, 0))
    wide = pl.BlockSpec((tm, W_HEADS), lambda i: (i, 0))
    const = lambda shape: pl.BlockSpec(shape, lambda i: (0, 0))
    return pl.pallas_call(
        functools.partial(_merge_kernel, alpha=alpha),
        grid=(rows // tm,),
        in_specs=[
            wide,
            pl.BlockSpec((tm, W_HEADS), lambda i: (i, 2)),
            wide,
            pl.BlockSpec((tm, d), lambda i: (i, 0)),
            pl.BlockSpec(mod_block, mod_map),
            const((1, W_HEADS)),
            pl.BlockSpec((None, d, d), lambda i: (layer, 0, 0)),
            const((1, d)),
            const((1, d)),
        ],
        out_specs=pl.BlockSpec((tm, d), lambda i: (i, 0)),
        out_shape=jax.ShapeDtypeStruct((rows, d), F32),
        scratch_shapes=[pltpu.VMEM((tm, d), BF16)],
        compiler_params=_cparams("arbitrary"),
        name="mixer_merge",
    )(h, main, att, x, gt, g_head, w_out, ln_g, ln_b)


def _ffn_kernel(*refs, alpha, seq_len, n_chunks, carry):
    if carry:
        (x_ref, sc_ref, sh_ref, gt_ref, wg_ref, wv_ref, wd_ref, cw_ref, cb_ref, lg_ref, lb_ref,
         o_ref, tail_ref, u_scr, acc_scr, carry_scr) = refs
    else:
        (x_ref, sc_ref, sh_ref, gt_ref, wg_ref, wv_ref, wd_ref, cw_ref, cb_ref, lg_ref, lb_ref,
         h0_ref, h1_ref, o_ref, tail_ref, u_scr, acc_scr) = refs
    i = pl.program_id(0)
    j = pl.program_id(1)

    @pl.when(j == 0)
    def _():
        u_scr[...] = (x_ref[...] * (1.0 + sc_ref[...]) + sh_ref[...]).astype(BF16)
        acc_scr[...] = jnp.zeros_like(acc_scr)

    ub = u_scr[...]
    g = _dot(ub, wg_ref[...])
    v = _dot(ub, wv_ref[...])
    tm = g.shape[0]
    row = _iota(g.shape, 0)
    if carry:
        @pl.when(i == 0)
        def _():
            carry_scr[j] = jnp.zeros(carry_scr.shape[1:], F32)

        tail = carry_scr[j]
        hist0, hist1 = tail[6:7, :], tail[7:8, :]
        pos = row
        carry_scr[j] = g[tm - 8:, :]
        tail_ref[...] = g[tm - 8:, :]
    else:
        hist0, hist1 = h0_ref[...], h1_ref[...]
        pos = row & (seq_len - 1)
        tail_ref[...] = g
    prev1 = jnp.where(pos == 0, hist1, pltpu.roll(g, 1, 0))
    prev2 = jnp.where(pos == 0, hist0, jnp.where(pos == 1, hist1, pltpu.roll(g, 2, 0)))
    cw = cw_ref[...]
    a = prev2 * cw[0:1, :] + prev1 * cw[1:2, :] + g * cw[2:3, :] + cb_ref[...]
    a = a * jax.nn.sigmoid(a) * v
    acc_scr[...] += _dot(a.astype(BF16), wd_ref[...])

    @pl.when(j == n_chunks - 1)
    def _():
        y = alpha * x_ref[...] + (1.0 + gt_ref[...]) * acc_scr[...]
        o_ref[...] = _layer_norm(y, lg_ref[...], lb_ref[...])


def _ffn(x, sc, sh, gt, w_up, w_down, conv_w, conv_b, ln_g, ln_b, layer, alpha, tm,
         hist=None, seq_len=None, tf=512):
    rows, d = x.shape
    d_ff = w_down.shape[1]
    nj = d_ff // tf
    rm = sc.shape[0]
    mod_block = (1, d) if rm == 1 else (tm, d)
    mod_map = (lambda i, j: (0, 0)) if rm == 1 else (lambda i, j: (i, 0))
    carry = hist is None
    in_specs = [
        pl.BlockSpec((tm, d), lambda i, j: (i, 0)),
        pl.BlockSpec(mod_block, mod_map),
        pl.BlockSpec(mod_block, mod_map),
        pl.BlockSpec(mod_block, mod_map),
        pl.BlockSpec((None, d, tf), lambda i, j: (layer, 0, j)),
        pl.BlockSpec((None, d, tf), lambda i, j: (layer, 0, nj + j)),
        pl.BlockSpec((None, tf, d), lambda i, j: (layer, j, 0)),
        pl.BlockSpec((CONV_W, tf), lambda i, j: (0, j)),
        pl.BlockSpec((1, tf), lambda i, j: (0, j)),
        pl.BlockSpec((1, d), lambda i, j: (0, 0)),
        pl.BlockSpec((1, d), lambda i, j: (0, 0)),
    ]
    args = [x, sc, sh, gt, w_up, w_up, w_down, conv_w, conv_b, ln_g, ln_b]
    scratch = [pltpu.VMEM((tm, d), BF16), pltpu.VMEM((tm, d), F32)]
    if carry:
        tail_rows = 8
        scratch.append(pltpu.VMEM((nj, 8, tf), F32))
    else:
        tail_rows = tm
        in_specs += [pl.BlockSpec((tm, tf), lambda i, j: (i, j))] * 2
        args += list(hist)
    kern = functools.partial(_ffn_kernel, alpha=alpha, seq_len=seq_len, n_chunks=nj, carry=carry)
    return pl.pallas_call(
        kern,
        grid=(rows // tm, nj),
        in_specs=in_specs,
        out_specs=[
            pl.BlockSpec((tm, d), lambda i, j: (i, 0)),
            pl.BlockSpec((tail_rows, tf), lambda i, j: (i, j)),
        ],
        out_shape=[
            jax.ShapeDtypeStruct((rows, d), F32),
            jax.ShapeDtypeStruct((rows // tm * tail_rows, d_ff), F32),
        ],
        scratch_shapes=scratch,
        compiler_params=_cparams("arbitrary", "arbitrary"),
        name="conv_ffn",
    )(*args)


def _rope_tables(pos):
    half = HEAD_DIM // 2
    inv = ROPE_THETA ** (-jnp.arange(half, dtype=F32) / half)
    ang = pos.astype(F32)[:, None] * inv[None, :]
    cos, sin = jnp.cos(ang), jnp.sin(ang)
    return jnp.concatenate([cos, cos], -1), jnp.concatenate([-sin, sin], -1)


def kernel(x_prompt, x_sample, cache_k_win, cache_v_win, state_C, state_n, state_m, state_conv,
           c_prompt, c_sample, w_ada, b_ada, w_in, b_gate, g_head, w_out, ln1_g, ln1_b,
           w_up, conv_w, conv_b, w_down, ln2_g, ln2_b):
    bp, s, d = x_prompt.shape
    bs, t, _ = x_sample.shape
    depth = w_in.shape[0]
    d_ff = w_down.shape[1]
    alpha = (2 * depth) ** 0.25
    assert bp == 1 and d == (H_M + H_A) * HEAD_DIM and s % ATT_SUPER == 0

    n_c = bp + bs
    pad = (-n_c) % 8
    c_all = jnp.concatenate([c_prompt, c_sample, jnp.zeros((pad, d), F32)], 0)
    mod = _modulation(c_all, w_ada, b_ada)

    n_m = 2 * H_M * DK_M + 2 * H_M * DV_M
    w_main = jnp.concatenate([w_in[:, :, :n_m], w_in[:, :, n_m + 2 * H_M:]], -1).astype(BF16)
    w_gate = jnp.pad(w_in[:, :, n_m:n_m + 2 * H_M], ((0, 0), (0, 0), (0, N_GATE_PAD - 2 * H_M))).astype(BF16)
    gate_bias = jnp.pad(b_gate.reshape(depth, 1, 2 * H_M), ((0, 0), (0, 0), (0, N_GATE_PAD - 2 * H_M)))
    w_out_b = w_out.astype(BF16)
    w_up_b = w_up.astype(BF16)
    w_down_b = w_down.astype(BF16)

    cc_p, ss_p = _rope_tables(jnp.arange(s, dtype=jnp.int32))
    pos_s = PAST_LEN + jnp.arange(t, dtype=jnp.int32)
    cc_s, ss_s = (jnp.tile(a, (bs, 1)) for a in _rope_tables(pos_s))

    xp = x_prompt.reshape(s, d)
    xs = x_sample.reshape(bs * t, d)
    rows_s = bs * t
    tm_p = 512
    outs = [[] for _ in range(12)]
    zeros_c = jnp.zeros((bp, H_M, DK_M, DV_M), F32)
    zeros_n = jnp.zeros((bp, H_M, DK_M), F32)
    zeros_m = jnp.zeros((bp, H_M), F32)
    row2 = lambda v: v.reshape(1, -1)

    for l in range(depth):
        mod_p = [mod[l, 0:1, k * d:(k + 1) * d] for k in range(6)]
        mod_s = [jnp.repeat(mod[l, bp:bp + bs, k * d:(k + 1) * d], t, axis=0) for k in range(6)]
        ln1 = (row2(ln1_g[l]), row2(ln1_b[l]))
        ffn_small = (conv_w[l], row2(conv_b[l]), row2(ln2_g[l]), row2(ln2_b[l]))

        main_p, gates_p = _inproj(xp, mod_p[1], mod_p[0], w_main, w_gate, gate_bias, cc_p, ss_p, l, tm=tm_p)
        h_p, c_p, n_p, m_p = _mlstm(main_p, gates_p, zeros_c, zeros_n, zeros_m, batch=bp, L=128)
        att_p = _attn_prompt(main_p)
        x1_p = _merge(h_p, main_p, att_p, xp, mod_p[2], row2(g_head[l]), w_out_b, *ln1, l, alpha, tm=256)
        xp, tail_p = _ffn(x1_p, mod_p[4], mod_p[3], mod_p[5], w_up_b, w_down_b, *ffn_small, l, alpha, tm=tm_p)

        main_s, gates_s = _inproj(xs, mod_s[1], mod_s[0], w_main, w_gate, gate_bias, cc_s, ss_s, l, tm=rows_s)
        h_s, c_s, n_s, m_s = _mlstm(main_s, gates_s, state_C[l], state_n[l], state_m[l], batch=bs, L=t)
        att_s = _attn_sample(main_s, cache_k_win, cache_v_win, l, bs, t)
        x1_s = _merge(h_s, main_s, att_s, xs, mod_s[2], row2(g_head[l]), w_out_b, *ln1, l, alpha, tm=rows_s)
        hist = [jnp.repeat(state_conv[l][:, r, :], t, axis=0) for r in range(CONV_W - 1)]
        xs, g_s = _ffn(x1_s, mod_s[4], mod_s[3], mod_s[5], w_up_b, w_down_b, *ffn_small, l, alpha, tm=rows_s,
                       hist=hist, seq_len=t)

        wp = min(DILATIONS[-1][0], s)
        k_cols = slice(4 * W_HEADS, 5 * W_HEADS)
        v_cols = slice(5 * W_HEADS, 6 * W_HEADS)
        outs[0].append(main_p[s - wp:, k_cols].reshape(bp, wp, H_A, HEAD_DIM))
        outs[1].append(main_p[s - wp:, v_cols].reshape(bp, wp, H_A, HEAD_DIM))
        outs[2].append(c_p)
        outs[3].append(n_p)
        outs[4].append(m_p.reshape(bp, H_M))
        outs[5].append(tail_p[-8:][8 - (CONV_W - 1):].reshape(bp, CONV_W - 1, d_ff))
        outs[6].append(main_s[:, k_cols].reshape(bs, t, H_A, HEAD_DIM))
        outs[7].append(main_s[:, v_cols].reshape(bs, t, H_A, HEAD_DIM))
        outs[8].append(c_s)
        outs[9].append(n_s)
        outs[10].append(m_s.reshape(bs, H_M))
        outs[11].append(g_s.reshape(bs, t, d_ff)[:, t - (CONV_W - 1):])

    return (xp.reshape(bp, s, d), xs.reshape(bs, t, d)) + tuple(jnp.stack(o) for o in outs)
```

```python
import functools

import jax
import jax.numpy as jnp
from jax import lax
from jax.experimental import pallas as pl
from jax.experimental.pallas import tpu as pltpu

F32 = jnp.float32
BF16 = jnp.bfloat16
HIGHEST = lax.Precision.HIGHEST

HEAD_DIM = 128
H_M = 8
H_A = 8
DK_M = 64
DV_M = 128
DILATIONS = ((128, 1), (512, 4), (2048, 16))
ATT_BLOCK = 128
PAST_LEN = 8192
ROPE_THETA = 10000.0
CONV_W = 3
LN_EPS = 1e-5
HEAD_NORM_EPS = 1e-6
N_GATE_PAD = 128
W_HEADS = H_A * HEAD_DIM
ATT_SUPER = ATT_BLOCK * DILATIONS[-1][1]
ATT_UNROLL = 2

VMEM_LIMIT_BYTES = 56 * 1024 * 1024


def _cparams(*sem):
    return pltpu.CompilerParams(dimension_semantics=sem, vmem_limit_bytes=VMEM_LIMIT_BYTES)


def _dot(a, b):
    return jnp.dot(a, b, preferred_element_type=F32)


def _dot_nt(a, b, precision=None):
    return lax.dot_general(a, b, (((1,), (1,)), ((), ())), precision=precision,
                           preferred_element_type=F32)


def _dot_tn(a, b):
    return lax.dot_general(a, b, (((0,), (0,)), ((), ())), preferred_element_type=F32)


def _iota(shape, dim):
    return lax.broadcasted_iota(jnp.int32, shape, dim)


def _mod_kernel(c_ref, w_ref, b_ref, o_ref):
    c = c_ref[...]
    a = (c * jax.nn.sigmoid(c)).astype(BF16)
    o_ref[...] = _dot(a, w_ref[...].astype(BF16)) + b_ref[...]


def _modulation(c_all, w_ada, b_ada, tn=1024):
    depth, d, n = w_ada.shape
    rows = c_all.shape[0]
    return pl.pallas_call(
        _mod_kernel,
        grid=(depth, n // tn),
        in_specs=[
            pl.BlockSpec((rows, d), lambda l, j: (0, 0)),
            pl.BlockSpec((None, d, tn), lambda l, j: (l, 0, j)),
            pl.BlockSpec((None, 1, tn), lambda l, j: (l, 0, j)),
        ],
        out_specs=pl.BlockSpec((None, rows, tn), lambda l, j: (l, 0, j)),
        out_shape=jax.ShapeDtypeStruct((depth, rows, n), F32),
        compiler_params=_cparams("arbitrary", "arbitrary"),
        name="adaln_mod",
    )(c_all, w_ada, b_ada.reshape(depth, 1, n))


def _inproj_kernel(x_ref, sc_ref, sh_ref, w_ref, wg_ref, gb_ref, cc_ref, ss_ref,
                   main_ref, kv_ref, gate_ref, u_scr, *, tn, q_lo, k_lo, v_lo):
    j = pl.program_id(1)

    @pl.when(j == 0)
    def _():
        ub = (x_ref[...] * (1.0 + sc_ref[...]) + sh_ref[...]).astype(BF16)
        u_scr[...] = ub
        z = _dot(ub, wg_ref[...]) + gb_ref[...]
        lane = _iota(z.shape, 1)
        log_sig = jnp.minimum(z, 0.0) - jnp.log1p(jnp.exp(-jnp.abs(z)))
        gate_ref[...] = jnp.where((lane >= H_M) & (lane < 2 * H_M), log_sig, z)

    acc = _dot(u_scr[...], w_ref[...])

    def rope(a):
        return a * cc_ref[...] + pltpu.roll(a, HEAD_DIM // 2, 1) * ss_ref[...]

    head_slices = [slice(g * HEAD_DIM, (g + 1) * HEAD_DIM) for g in range(tn // HEAD_DIM)]

    @pl.when(j < q_lo)
    def _():
        main_ref[...] = acc.astype(main_ref.dtype)

    @pl.when((j >= q_lo) & (j < k_lo))
    def _():
        for sl in head_slices:
            main_ref[:, sl] = (rope(acc[:, sl]) * (HEAD_DIM ** -0.5)).astype(main_ref.dtype)

    @pl.when((j >= k_lo) & (j < v_lo))
    def _():
        for sl in head_slices:
            y = rope(acc[:, sl])
            main_ref[:, sl] = y.astype(main_ref.dtype)
            kv_ref[:, sl] = y

    @pl.when(j >= v_lo)
    def _():
        main_ref[...] = acc.astype(main_ref.dtype)
        kv_ref[...] = acc


def _inproj(x, sc, sh, w_main, w_gate, gate_bias, rope_cc, rope_ss, layer, tm, main_dtype, tn=512):
    rows, d = x.shape
    n = w_main.shape[2]
    rm = sc.shape[0]
    mod_block = (1, d) if rm == 1 else (tm, d)
    mod_map = (lambda i, j: (0, 0)) if rm == 1 else (lambda i, j: (i, 0))
    q_lo = (3 * W_HEADS) // tn
    k_lo = (4 * W_HEADS) // tn
    v_lo = (5 * W_HEADS) // tn
    kern = functools.partial(_inproj_kernel, tn=tn, q_lo=q_lo, k_lo=k_lo, v_lo=v_lo)
    return pl.pallas_call(
        kern,
        grid=(rows // tm, n // tn),
        in_specs=[
            pl.BlockSpec((tm, d), lambda i, j: (i, 0)),
            pl.BlockSpec(mod_block, mod_map),
            pl.BlockSpec(mod_block, mod_map),
            pl.BlockSpec((None, d, tn), lambda i, j: (layer, 0, j)),
            pl.BlockSpec((None, d, N_GATE_PAD), lambda i, j: (layer, 0, 0)),
            pl.BlockSpec((None, 1, N_GATE_PAD), lambda i, j: (layer, 0, 0)),
            pl.BlockSpec((tm, HEAD_DIM), lambda i, j: (i, 0)),
            pl.BlockSpec((tm, HEAD_DIM), lambda i, j: (i, 0)),
        ],
        out_specs=[
            pl.BlockSpec((tm, tn), lambda i, j: (i, j)),
            pl.BlockSpec((tm, tn), lambda i, j: (i, jnp.maximum(j - k_lo, 0))),
            pl.BlockSpec((tm, N_GATE_PAD), lambda i, j: (i, 0)),
        ],
        out_shape=[
            jax.ShapeDtypeStruct((rows, n), main_dtype),
            jax.ShapeDtypeStruct((rows, 2 * W_HEADS), F32),
            jax.ShapeDtypeStruct((rows, N_GATE_PAD), F32),
        ],
        scratch_shapes=[pltpu.VMEM((tm, d), BF16)],
        compiler_params=_cparams("arbitrary", "arbitrary"),
        name="inproj",
    )(x, sc, sh, w_main, w_gate, gate_bias, rope_cc, rope_ss)


def _mlstm_kernel(q_ref, k_ref, v_ref, g_ref, c0_ref, n0_ref, m0_ref,
                  h_ref, c_ref, n_ref, m_ref, *, L):
    @pl.when(pl.program_id(1) == 0)
    def _():
        c_ref[...] = c0_ref[...]
        n_ref[...] = n0_ref[...]
        m_ref[...] = m0_ref[...]

    gates = g_ref[...]
    row = _iota((L, L), 0)
    col = _iota((L, L), 1)
    causal = row >= col
    tril = causal.astype(F32)
    triu = (row <= col).astype(F32)
    eye = (_iota((2 * H_M, N_GATE_PAD), 0) == _iota((2 * H_M, N_GATE_PAD), 1)).astype(F32)
    gates_t = _dot_nt(eye, gates, precision=HIGHEST)
    bcol_all = jnp.dot(tril, gates, precision=HIGHEST, preferred_element_type=F32)
    brow_all = jnp.dot(gates_t, triu, precision=HIGHEST, preferred_element_type=F32)

    for h in range(H_M):
        b_col = bcol_all[:, H_M + h:H_M + h + 1]
        b_row = brow_all[H_M + h:H_M + h + 1, :]
        ig_row = gates_t[h:h + 1, :]
        ig_col = gates[:, h:h + 1]
        m0 = m_ref[:, h:h + 1]
        c0 = c_ref[h]
        n0 = n_ref[h:h + 1, :]

        dmat = jnp.where(causal, b_col - b_row + ig_row, -jnp.inf)
        m_inter = b_col + m0
        m = jnp.maximum(m_inter, jnp.max(dmat, axis=1, keepdims=True))
        w = jnp.exp(dmat - m)
        g = jnp.exp(m_inter - m)

        q = q_ref[:, h * DK_M:(h + 1) * DK_M].astype(F32)
        k = k_ref[:, h * DK_M:(h + 1) * DK_M].astype(F32) * (DK_M ** -0.5)
        qb = q.astype(BF16)
        kb = k.astype(BF16)
        vb = v_ref[:, h * DV_M:(h + 1) * DV_M].astype(BF16)

        s = _dot_nt(qb, kb) * w
        num = g * _dot(qb, c0.astype(BF16)) + _dot(s.astype(BF16), vb)
        den = g * jnp.sum(q * n0, axis=1, keepdims=True) + jnp.sum(s, axis=1, keepdims=True)
        h_ref[:, h * DV_M:(h + 1) * DV_M] = num / jnp.maximum(jnp.abs(den), jnp.exp(-m))

        b_last = b_col[L - 1:L, :]
        a_col = b_last - b_col + ig_col
        m_end = jnp.maximum(b_last + m0, jnp.max(a_col, axis=0, keepdims=True))
        ws = jnp.exp(a_col - m_end)
        g_end = jnp.exp(b_last + m0 - m_end)
        kw = ws * k
        c_ref[h] = g_end * c0 + _dot_tn(kw.astype(BF16), vb)
        n_ref[h:h + 1, :] = g_end * n0 + jnp.sum(kw, axis=0, keepdims=True)
        m_ref[:, h:h + 1] = m_end


def _mlstm(main, gates, c0, n0, m0, batch, L):
    rows = main.shape[0]
    nc = rows // (batch * L)
    kern = functools.partial(_mlstm_kernel, L=L)
    state_map = lambda b, c: (b, 0, 0, 0)
    return pl.pallas_call(
        kern,
        grid=(batch, nc),
        in_specs=[
            pl.BlockSpec((L, H_M * DK_M), lambda b, c: (b * nc + c, 0)),
            pl.BlockSpec((L, H_M * DK_M), lambda b, c: (b * nc + c, 1)),
            pl.BlockSpec((L, H_M * DV_M), lambda b, c: (b * nc + c, 1)),
            pl.BlockSpec((L, N_GATE_PAD), lambda b, c: (b * nc + c, 0)),
            pl.BlockSpec((None, H_M, DK_M, DV_M), state_map),
            pl.BlockSpec((None, H_M, DK_M), lambda b, c: (b, 0, 0)),
            pl.BlockSpec((None, 1, H_M), lambda b, c: (b, 0, 0)),
        ],
        out_specs=[
            pl.BlockSpec((L, H_M * DV_M), lambda b, c: (b * nc + c, 0)),
            pl.BlockSpec((None, H_M, DK_M, DV_M), state_map),
            pl.BlockSpec((None, H_M, DK_M), lambda b, c: (b, 0, 0)),
            pl.BlockSpec((None, 1, H_M), lambda b, c: (b, 0, 0)),
        ],
        out_shape=[
            jax.ShapeDtypeStruct((rows, H_M * DV_M), F32),
            jax.ShapeDtypeStruct((batch, H_M, DK_M, DV_M), F32),
            jax.ShapeDtypeStruct((batch, H_M, DK_M), F32),
            jax.ShapeDtypeStruct((batch, 1, H_M), F32),
        ],
        compiler_params=_cparams("arbitrary", "arbitrary"),
        name="mlstm",
    )(main, main, main, gates, c0, n0, m0.reshape(batch, 1, H_M))


def _attn_prompt_kernel(q_ref, kp_ref, kc_ref, vp_ref, vc_ref, o_ref,
                        qq_scr, kk_scr, vv_scr, num_scr, m_scr, den_scr):
    sb = pl.program_id(0)
    SB = ATT_SUPER
    qq_scr[...] = q_ref[...].astype(F32)
    kk_scr[0:SB, :] = kp_ref[...].astype(F32)
    kk_scr[SB:, :] = kc_ref[...].astype(F32)
    vv_scr[0:SB, :] = vp_ref[...].astype(F32)
    vv_scr[SB:, :] = vc_ref[...].astype(F32)

    shape = (ATT_BLOCK, 2 * ATT_BLOCK)
    qi = _iota(shape, 0)
    ki = _iota(shape, 1)
    window = (ki >= qi) & (ki <= qi + ATT_BLOCK)
    window_first = window & ((ki >= ATT_BLOCK) | (sb > 0))

    for bi, (_, dil) in enumerate(DILATIONS):
        for n in range(SB // (ATT_BLOCK * dil)):
            valid = window_first if n == 0 else window

            def unit(r, carry, bi=bi, dil=dil, n=n, valid=valid):
                q_rows = pl.ds(r + n * ATT_BLOCK * dil, ATT_BLOCK, stride=dil)
                k_rows = pl.ds(SB + r + (n - 1) * ATT_BLOCK * dil, 2 * ATT_BLOCK, stride=dil)
                q = qq_scr[q_rows, :].astype(BF16)
                k = kk_scr[k_rows, :].astype(BF16)
                v = vv_scr[k_rows, :].astype(BF16)
                s = jnp.where(valid, _dot_nt(q, k), -jnp.inf)
                m = jnp.max(s, axis=1, keepdims=True)
                p = jnp.exp(s - m)
                den = jnp.sum(p, axis=1, keepdims=True)
                num_scr[bi, q_rows, :] = _dot(p.astype(BF16), v)
                m_scr[bi, q_rows, :] = jnp.broadcast_to(m, (ATT_BLOCK, HEAD_DIM))
                den_scr[bi, q_rows, :] = jnp.broadcast_to(den, (ATT_BLOCK, HEAD_DIM))
                return carry

            lax.fori_loop(0, dil, unit, 0, unroll=min(dil, ATT_UNROLL))

    rows_per_step = 2 * ATT_BLOCK

    def mix(i, carry):
        rows = pl.ds(pl.multiple_of(i * rows_per_step, rows_per_step), rows_per_step)
        ms = [m_scr[b, rows, :] for b in range(len(DILATIONS))]
        big = ms[0]
        for mm in ms[1:]:
            big = jnp.maximum(big, mm)
        w0 = jnp.exp(ms[0] - big)
        acc_num = w0 * num_scr[0, rows, :]
        acc_den = w0 * den_scr[0, rows, :]
        for b in range(1, len(DILATIONS)):
            w = jnp.exp(ms[b] - big)
            acc_num = acc_num + w * num_scr[b, rows, :]
            acc_den = acc_den + w * den_scr[b, rows, :]
        o_ref[rows, :] = acc_num / acc_den
        return carry

    lax.fori_loop(0, SB // rows_per_step, mix, 0)


def _attn_prompt(main):
    s, n_main = main.shape
    SB = ATT_SUPER
    cols = n_main // HEAD_DIM // 6
    blk = (SB, HEAD_DIM)
    prev = lambda i: jnp.maximum(i - 1, 0)
    nbr = len(DILATIONS)
    return pl.pallas_call(
        _attn_prompt_kernel,
        grid=(s // SB, H_A),
        in_specs=[
            pl.BlockSpec(blk, lambda i, h: (i, 3 * cols + h)),
            pl.BlockSpec(blk, lambda i, h: (prev(i), 4 * cols + h)),
            pl.BlockSpec(blk, lambda i, h: (i, 4 * cols + h)),
            pl.BlockSpec(blk, lambda i, h: (prev(i), 5 * cols + h)),
            pl.BlockSpec(blk, lambda i, h: (i, 5 * cols + h)),
        ],
        out_specs=pl.BlockSpec(blk, lambda i, h: (i, h)),
        out_shape=jax.ShapeDtypeStruct((s, W_HEADS), F32),
        scratch_shapes=[
            pltpu.VMEM((SB, HEAD_DIM), F32),
            pltpu.VMEM((2 * SB, HEAD_DIM), F32),
            pltpu.VMEM((2 * SB, HEAD_DIM), F32),
            pltpu.VMEM((nbr, SB, HEAD_DIM), F32),
            pltpu.VMEM((nbr, SB, HEAD_DIM), F32),
            pltpu.VMEM((nbr, SB, HEAD_DIM), F32),
        ],
        compiler_params=_cparams("arbitrary", "arbitrary"),
        name="attn_prompt",
    )(main, main, main, main, main)


def _branch_count(delta):
    cnt = jnp.zeros(delta.shape, F32)
    for window, dil in DILATIONS:
        hit = (delta >= 0) & (delta <= window) & ((delta & (dil - 1)) == 0)
        cnt = cnt + hit.astype(F32)
    return cnt


def _attn_sample_kernel(q_ref, kn_ref, vn_ref, ck_ref, cv_ref, o_ref, cnt_scr, s_scr, *, T, NB, chunk):
    R = H_A * T
    NF = NB * H_A
    t_bits = T.bit_length() - 1
    h_bits = H_A.bit_length() - 1
    chunks = [slice(c * chunk, (c + 1) * chunk) for c in range(NF // chunk)]

    @pl.when(pl.program_id(0) == 0)
    def _():
        for c, sl in enumerate(chunks):
            row = _iota((R, chunk), 0)
            col = c * chunk + _iota((R, chunk), 1)
            delta = NB + (row & (T - 1)) - (col >> h_bits)
            same_head = (row >> t_bits) == (col & (H_A - 1))
            cnt_scr[:, sl] = jnp.where(same_head, _branch_count(delta), 0.0)

    heads = [slice(h * HEAD_DIM, (h + 1) * HEAD_DIM) for h in range(H_A)]
    q = jnp.concatenate([q_ref[:, sl] for sl in heads], axis=0).astype(BF16)
    kn = jnp.concatenate([kn_ref[:, sl] for sl in heads], axis=0).astype(BF16)
    vn = jnp.concatenate([vn_ref[:, sl] for sl in heads], axis=0).astype(BF16)
    row = _iota((R, R), 0)
    col = _iota((R, R), 1)
    cnt_n = jnp.where((row >> t_bits) == (col >> t_bits),
                      _branch_count((row & (T - 1)) - (col & (T - 1))), 0.0)

    s_n = jnp.where(cnt_n > 0, _dot_nt(q, kn), -jnp.inf)
    big = jnp.max(s_n, axis=1, keepdims=True)
    for sl in chunks:
        s = jnp.where(cnt_scr[:, sl] > 0, _dot_nt(q, ck_ref[sl, :].astype(BF16)), -jnp.inf)
        s_scr[:, sl] = s
        big = jnp.maximum(big, jnp.max(s, axis=1, keepdims=True))

    p_n = cnt_n * jnp.exp(s_n - big)
    den = jnp.sum(p_n, axis=1, keepdims=True)
    num = _dot(p_n.astype(BF16), vn)
    for sl in chunks:
        p = cnt_scr[:, sl] * jnp.exp(s_scr[:, sl] - big)
        den = den + jnp.sum(p, axis=1, keepdims=True)
        num = num + _dot(p.astype(BF16), cv_ref[sl, :].astype(BF16))
    out = num / den
    for h, sl in enumerate(heads):
        o_ref[:, sl] = out[h * T:(h + 1) * T, :]


def _attn_sample(main, cache_k, cache_v, layer, batch, T, chunk=2048):
    depth, _, nb, n_heads, hd = cache_k.shape
    assert n_heads == H_A and hd == HEAD_DIM and T & (T - 1) == 0 and H_A & (H_A - 1) == 0
    nf = nb * H_A
    ck = cache_k.reshape(depth, batch, nf, HEAD_DIM)
    cv = cache_v.reshape(depth, batch, nf, HEAD_DIM)
    kern = functools.partial(_attn_sample_kernel, T=T, NB=nb, chunk=chunk)
    cache_spec = pl.BlockSpec((None, None, nf, HEAD_DIM), lambda b: (layer, b, 0, 0))
    return pl.pallas_call(
        kern,
        grid=(batch,),
        in_specs=[
            pl.BlockSpec((T, W_HEADS), lambda b: (b, 3)),
            pl.BlockSpec((T, W_HEADS), lambda b: (b, 4)),
            pl.BlockSpec((T, W_HEADS), lambda b: (b, 5)),
            cache_spec,
            cache_spec,
        ],
        out_specs=pl.BlockSpec((T, W_HEADS), lambda b: (b, 0)),
        out_shape=jax.ShapeDtypeStruct((batch * T, W_HEADS), F32),
        scratch_shapes=[pltpu.VMEM((H_A * T, nf), F32), pltpu.VMEM((H_A * T, nf), F32)],
        compiler_params=_cparams("arbitrary"),
        name="attn_sample",
    )(main, main, main, ck, cv)


def _layer_norm(y, g, b):
    mu = jnp.mean(y, axis=1, keepdims=True)
    yc = y - mu
    var = jnp.mean(yc * yc, axis=1, keepdims=True)
    return yc * lax.rsqrt(var + LN_EPS) * g + b


def _merge_kernel(h_ref, om_ref, att_ref, x_ref, gt_ref, gh_ref, w_ref, lg_ref, lb_ref,
                  o_ref, cat_scr, *, alpha):
    for h in range(H_M):
        sl = slice(h * DV_M, (h + 1) * DV_M)
        hh = h_ref[:, sl]
        hn = hh * lax.rsqrt(jnp.mean(hh * hh, axis=1, keepdims=True) + HEAD_NORM_EPS) * gh_ref[:, sl]
        cat_scr[:, sl] = (hn * jax.nn.sigmoid(om_ref[:, sl].astype(F32))).astype(BF16)
    cat_scr[:, H_M * DV_M:] = att_ref[...].astype(BF16)
    mix = _dot(cat_scr[...], w_ref[...])
    y = alpha * x_ref[...] + (1.0 + gt_ref[...]) * mix
    o_ref[...] = _layer_norm(y, lg_ref[...], lb_ref[...])


def _merge(h, main, att, x, gt, g_head, w_out, ln_g, ln_b, layer, alpha, tm):
    rows, d = x.shape
    rm = gt.shape[0]
    mod_block = (1, d) if rm == 1 else (tm, d)
    mod_map = (lambda i: (0, 0)) if rm == 1 else (lambda i: (i, 0))
    wide = pl.BlockSpec((tm, W_HEADS), lambda i: (i, 0))
    const = lambda shape: pl.BlockSpec(shape, lambda i: (0, 0))
    return pl.pallas_call(
        functools.partial(_merge_kernel, alpha=alpha),
        grid=(rows // tm,),
        in_specs=[
            wide,
            pl.BlockSpec((tm, W_HEADS), lambda i: (i, 2)),
            wide,
            pl.BlockSpec((tm, d), lambda i: (i, 0)),
            pl.BlockSpec(mod_block, mod_map),
            const((1, W_HEADS)),
            pl.BlockSpec((None, d, d), lambda i: (layer, 0, 0)),
            const((1, d)),
            const((1, d)),
        ],
        out_specs=pl.BlockSpec((tm, d), lambda i: (i, 0)),
        out_shape=jax.ShapeDtypeStruct((rows, d), F32),
        scratch_shapes=[pltpu.VMEM((tm, d), BF16)],
        compiler_params=_cparams("arbitrary"),
        name="mixer_merge",
    )(h, main, att, x, gt, g_head, w_out, ln_g, ln_b)


def _ffn_kernel(*refs, alpha, seq_len, n_chunks, carry):
    if carry:
        (x_ref, sc_ref, sh_ref, gt_ref, wg_ref, wv_ref, wd_ref, cw_ref, cb_ref, lg_ref, lb_ref,
         o_ref, tail_ref, u_scr, carry_scr) = refs
    else:
        (x_ref, sc_ref, sh_ref, gt_ref, wg_ref, wv_ref, wd_ref, cw_ref, cb_ref, lg_ref, lb_ref,
         h0_ref, h1_ref, o_ref, tail_ref, u_scr) = refs
    i = pl.program_id(0)
    j = pl.program_id(1)

    @pl.when(j == 0)
    def _():
        u_scr[...] = (x_ref[...] * (1.0 + sc_ref[...]) + sh_ref[...]).astype(BF16)
        o_ref[...] = jnp.zeros_like(o_ref)

    ub = u_scr[...]
    g = _dot(ub, wg_ref[...])
    v = _dot(ub, wv_ref[...])
    tm = g.shape[0]
    row = _iota(g.shape, 0)
    if carry:
        @pl.when(i == 0)
        def _():
            carry_scr[j] = jnp.zeros(carry_scr.shape[1:], F32)

        tail = carry_scr[j]
        hist0, hist1 = tail[6:7, :], tail[7:8, :]
        pos = row
        carry_scr[j] = g[tm - 8:, :]
        tail_ref[...] = g[tm - 8:, :]
    else:
        hist0, hist1 = h0_ref[...], h1_ref[...]
        pos = row & (seq_len - 1)
        tail_ref[...] = g
    prev1 = jnp.where(pos == 0, hist1, pltpu.roll(g, 1, 0))
    prev2 = jnp.where(pos == 0, hist0, jnp.where(pos == 1, hist1, pltpu.roll(g, 2, 0)))
    cw = cw_ref[...]
    a = prev2 * cw[0:1, :] + prev1 * cw[1:2, :] + g * cw[2:3, :] + cb_ref[...]
    a = a * jax.nn.sigmoid(a) * v
    o_ref[...] += _dot(a.astype(BF16), wd_ref[...])

    @pl.when(j == n_chunks - 1)
    def _():
        y = alpha * x_ref[...] + (1.0 + gt_ref[...]) * o_ref[...]
        o_ref[...] = _layer_norm(y, lg_ref[...], lb_ref[...])


def _ffn(x, sc, sh, gt, w_up, w_down, conv_w, conv_b, ln_g, ln_b, layer, alpha, tm,
         hist=None, seq_len=None, tf=512):
    rows, d = x.shape
    d_ff = w_down.shape[1]
    nj = d_ff // tf
    rm = sc.shape[0]
    mod_block = (1, d) if rm == 1 else (tm, d)
    mod_map = (lambda i, j: (0, 0)) if rm == 1 else (lambda i, j: (i, 0))
    carry = hist is None
    in_specs = [
        pl.BlockSpec((tm, d), lambda i, j: (i, 0), pipeline_mode=pl.Buffered(1)),
        pl.BlockSpec(mod_block, mod_map),
        pl.BlockSpec(mod_block, mod_map),
        pl.BlockSpec(mod_block, mod_map),
        pl.BlockSpec((None, d, tf), lambda i, j: (layer, 0, j)),
        pl.BlockSpec((None, d, tf), lambda i, j: (layer, 0, nj + j)),
        pl.BlockSpec((None, tf, d), lambda i, j: (layer, j, 0)),
        pl.BlockSpec((CONV_W, tf), lambda i, j: (0, j)),
        pl.BlockSpec((1, tf), lambda i, j: (0, j)),
        pl.BlockSpec((1, d), lambda i, j: (0, 0)),
        pl.BlockSpec((1, d), lambda i, j: (0, 0)),
    ]
    args = [x, sc, sh, gt, w_up, w_up, w_down, conv_w, conv_b, ln_g, ln_b]
    scratch = [pltpu.VMEM((tm, d), BF16)]
    if carry:
        tail_rows = 8
        scratch.append(pltpu.VMEM((nj, 8, tf), F32))
    else:
        tail_rows = tm
        in_specs += [pl.BlockSpec((tm, tf), lambda i, j: (i, j))] * 2
        args += list(hist)
    kern = functools.partial(_ffn_kernel, alpha=alpha, seq_len=seq_len, n_chunks=nj, carry=carry)
    return pl.pallas_call(
        kern,
        grid=(rows // tm, nj),
        in_specs=in_specs,
        out_specs=[
            pl.BlockSpec((tm, d), lambda i, j: (i, 0)),
            pl.BlockSpec((tail_rows, tf), lambda i, j: (i, j)),
        ],
        out_shape=[
            jax.ShapeDtypeStruct((rows, d), F32),
            jax.ShapeDtypeStruct((rows // tm * tail_rows, d_ff), F32),
        ],
        scratch_shapes=scratch,
        compiler_params=_cparams("arbitrary", "arbitrary"),
        name="conv_ffn",
    )(*args)


def _rope_tables(pos):
    half = HEAD_DIM // 2
    inv = ROPE_THETA ** (-jnp.arange(half, dtype=F32) / half)
    ang = pos.astype(F32)[:, None] * inv[None, :]
    cos, sin = jnp.cos(ang), jnp.sin(ang)
    return jnp.concatenate([cos, cos], -1), jnp.concatenate([-sin, sin], -1)


def kernel(x_prompt, x_sample, cache_k_win, cache_v_win, state_C, state_n, state_m, state_conv,
           c_prompt, c_sample, w_ada, b_ada, w_in, b_gate, g_head, w_out, ln1_g, ln1_b,
           w_up, conv_w, conv_b, w_down, ln2_g, ln2_b):
    bp, s, d = x_prompt.shape
    bs, t, _ = x_sample.shape
    depth = w_in.shape[0]
    d_ff = w_down.shape[1]
    alpha = (2 * depth) ** 0.25
    assert bp == 1 and d == (H_M + H_A) * HEAD_DIM and s % ATT_SUPER == 0

    n_c = bp + bs
    pad = (-n_c) % 8
    c_all = jnp.concatenate([c_prompt, c_sample, jnp.zeros((pad, d), F32)], 0)
    mod = _modulation(c_all, w_ada, b_ada)

    n_m = 2 * H_M * DK_M + 2 * H_M * DV_M
    w_main = jnp.concatenate([w_in[:, :, :n_m], w_in[:, :, n_m + 2 * H_M:]], -1).astype(BF16)
    w_gate = jnp.pad(w_in[:, :, n_m:n_m + 2 * H_M], ((0, 0), (0, 0), (0, N_GATE_PAD - 2 * H_M))).astype(BF16)
    gate_bias = jnp.pad(b_gate.reshape(depth, 1, 2 * H_M), ((0, 0), (0, 0), (0, N_GATE_PAD - 2 * H_M)))
    w_out_b = w_out.astype(BF16)
    w_up_b = w_up.astype(BF16)
    w_down_b = w_down.astype(BF16)

    cc_p, ss_p = _rope_tables(jnp.arange(s, dtype=jnp.int32))
    pos_s = PAST_LEN + jnp.arange(t, dtype=jnp.int32)
    cc_s, ss_s = (jnp.tile(a, (bs, 1)) for a in _rope_tables(pos_s))

    xp = x_prompt.reshape(s, d)
    xs = x_sample.reshape(bs * t, d)
    rows_s = bs * t
    tm_p = 1024
    outs =[[] for _ in range(12)]
    zeros_c = jnp.zeros((bp, H_M, DK_M, DV_M), F32)
    zeros_n = jnp.zeros((bp, H_M, DK_M), F32)
    zeros_m = jnp.zeros((bp, H_M), F32)
    row2 = lambda v: v.reshape(1, -1)

    for l in range(depth):
        mod_p = [mod[l, 0:1, k * d:(k + 1) * d] for k in range(6)]
        mod_s = [jnp.repeat(mod[l, bp:bp + bs, k * d:(k + 1) * d], t, axis=0) for k in range(6)]
        ln1 = (row2(ln1_g[l]), row2(ln1_b[l]))
        ffn_small = (conv_w[l], row2(conv_b[l]), row2(ln2_g[l]), row2(ln2_b[l]))

        main_p, kv_p, gates_p = _inproj(xp, mod_p[1], mod_p[0], w_main, w_gate, gate_bias, cc_p, ss_p, l,
                                        tm=tm_p, main_dtype=BF16)
        h_p, c_p, n_p, m_p = _mlstm(main_p, gates_p, zeros_c, zeros_n, zeros_m, batch=bp, L=128)
        att_p = _attn_prompt(main_p)
        x1_p = _merge(h_p, main_p, att_p, xp, mod_p[2], row2(g_head[l]), w_out_b, *ln1, l, alpha, tm=512)
        xp, tail_p = _ffn(x1_p, mod_p[4], mod_p[3], mod_p[5], w_up_b, w_down_b, *ffn_small, l, alpha, tm=tm_p)

        main_s, kv_s, gates_s = _inproj(xs, mod_s[1], mod_s[0], w_main, w_gate, gate_bias, cc_s, ss_s, l,
                                        tm=rows_s, main_dtype=F32)
        h_s, c_s, n_s, m_s = _mlstm(main_s, gates_s, state_C[l], state_n[l], state_m[l], batch=bs, L=t)
        att_s = _attn_sample(main_s, cache_k_win, cache_v_win, l, bs, t)
        x1_s = _merge(h_s, main_s, att_s, xs, mod_s[2], row2(g_head[l]), w_out_b, *ln1, l, alpha, tm=rows_s)
        hist = [jnp.repeat(state_conv[l][:, r, :], t, axis=0) for r in range(CONV_W - 1)]
        xs, g_s = _ffn(x1_s, mod_s[4], mod_s[3], mod_s[5], w_up_b, w_down_b, *ffn_small, l, alpha, tm=rows_s,
                       hist=hist, seq_len=t)

        wp = min(DILATIONS[-1][0], s)
        k_cols = slice(0, W_HEADS)
        v_cols = slice(W_HEADS, 2 * W_HEADS)
        outs[0].append(kv_p[s - wp:, k_cols].reshape(bp, wp, H_A, HEAD_DIM))
        outs[1].append(kv_p[s - wp:, v_cols].reshape(bp, wp, H_A, HEAD_DIM))
        outs[2].append(c_p)
        outs[3].append(n_p)
        outs[4].append(m_p.reshape(bp, H_M))
        outs[5].append(tail_p[-8:][8 - (CONV_W - 1):].reshape(bp, CONV_W - 1, d_ff))
        outs[6].append(kv_s[:, k_cols].reshape(bs, t, H_A, HEAD_DIM))
        outs[7].append(kv_s[:, v_cols].reshape(bs, t, H_A, HEAD_DIM))
        outs[8].append(c_s)
        outs[9].append(n_s)
        outs[10].append(m_s.reshape(bs, H_M))
        outs[11].append(g_s.reshape(bs, t, d_ff)[:, t - (CONV_W - 1):])

    return (xp.reshape(bp, s, d), xs.reshape(bs, t, d)) + tuple(jnp.stack(o) for o in outs)
```

```python
import functools

import jax
import jax.numpy as jnp
from jax import lax
from jax.experimental import pallas as pl
from jax.experimental.pallas import tpu as pltpu

F32 = jnp.float32
BF16 = jnp.bfloat16
HIGHEST = lax.Precision.HIGHEST

HEAD_DIM = 128
H_M = 8
H_A = 8
DK_M = 64
DV_M = 128
DILATIONS = ((128, 1), (512, 4), (2048, 16))
ATT_BLOCK = 128
PAST_LEN = 8192
ROPE_THETA = 10000.0
CONV_W = 3
LN_EPS = 1e-5
HEAD_NORM_EPS = 1e-6
N_GATE_PAD = 128
W_HEADS = H_A * HEAD_DIM
ATT_SUPER = ATT_BLOCK * DILATIONS[-1][1]
VMEM_LIMIT_BYTES = 56 * 1024 * 1024


def _cparams(*sem):
    return pltpu.CompilerParams(dimension_semantics=sem, vmem_limit_bytes=VMEM_LIMIT_BYTES)


def _dot(a, b):
    return jnp.dot(a, b, preferred_element_type=F32)


def _dot_nt(a, b, precision=None):
    return lax.dot_general(a, b, (((1,), (1,)), ((), ())), precision=precision,
                           preferred_element_type=F32)


def _dot_tn(a, b):
    return lax.dot_general(a, b, (((0,), (0,)), ((), ())), preferred_element_type=F32)


def _iota(shape, dim):
    return lax.broadcasted_iota(jnp.int32, shape, dim)


def _mod_kernel(c_ref, w_ref, b_ref, o_ref):
    c = c_ref[...]
    a = (c * jax.nn.sigmoid(c)).astype(BF16)
    o_ref[...] = _dot(a, w_ref[...].astype(BF16)) + b_ref[...]


def _modulation(c_all, w_ada, b_ada, tn=1024):
    depth, d, n = w_ada.shape
    rows = c_all.shape[0]
    return pl.pallas_call(
        _mod_kernel,
        grid=(depth, n // tn),
        in_specs=[
            pl.BlockSpec((rows, d), lambda l, j: (0, 0)),
            pl.BlockSpec((None, d, tn), lambda l, j: (l, 0, j)),
            pl.BlockSpec((None, 1, tn), lambda l, j: (l, 0, j)),
        ],
        out_specs=pl.BlockSpec((None, rows, tn), lambda l, j: (l, 0, j)),
        out_shape=jax.ShapeDtypeStruct((depth, rows, n), F32),
        compiler_params=_cparams("arbitrary", "arbitrary"),
        name="adaln_mod",
    )(c_all, w_ada, b_ada.reshape(depth, 1, n))


def _inproj_kernel(x_ref, sc_ref, sh_ref, w_ref, wg_ref, gb_ref, cc_ref, ss_ref,
                   main_ref, kv_ref, gate_ref, u_scr, *, tn, q_lo, k_lo, v_lo, sub):
    j = pl.program_id(1)
    tm = u_scr.shape[0]
    head_slices = [slice(g * HEAD_DIM, (g + 1) * HEAD_DIM) for g in range(tn // HEAD_DIM)]

    def rows_of(ref, rs):
        return ref[...] if ref.shape[0] == 1 else ref[rs, :]

    def step(kind, first):
        for r0 in range(0, tm, sub):
            rs = slice(r0, r0 + sub)
            if first:
                ub = (x_ref[rs, :] * (1.0 + rows_of(sc_ref, rs)) + rows_of(sh_ref, rs)).astype(BF16)
                u_scr[rs, :] = ub
                z = _dot(ub, wg_ref[...]) + gb_ref[...]
                lane = _iota(z.shape, 1)
                log_sig = jnp.minimum(z, 0.0) - jnp.log1p(jnp.exp(-jnp.abs(z)))
                gate_ref[rs, :] = jnp.where((lane >= H_M) & (lane < 2 * H_M), log_sig, z)
            else:
                ub = u_scr[rs, :]
            acc = _dot(ub, w_ref[...])

            def rope(a):
                return a * cc_ref[rs, :] + pltpu.roll(a, HEAD_DIM // 2, 1) * ss_ref[rs, :]

            if kind == "plain":
                main_ref[rs, :] = acc.astype(main_ref.dtype)
            elif kind == "q":
                for sl in head_slices:
                    main_ref[rs, sl] = (rope(acc[:, sl]) * (HEAD_DIM ** -0.5)).astype(main_ref.dtype)
            elif kind == "k":
                for sl in head_slices:
                    y = rope(acc[:, sl])
                    main_ref[rs, sl] = y.astype(main_ref.dtype)
                    kv_ref[rs, sl] = y
            else:
                main_ref[rs, :] = acc.astype(main_ref.dtype)
                kv_ref[rs, :] = acc

    pl.when(j == 0)(functools.partial(step, "plain", True))
    pl.when((j > 0) & (j < q_lo))(functools.partial(step, "plain", False))
    pl.when((j >= q_lo) & (j < k_lo))(functools.partial(step, "q", False))
    pl.when((j >= k_lo) & (j < v_lo))(functools.partial(step, "k", False))
    pl.when(j >= v_lo)(functools.partial(step, "v", False))


def _inproj(x, sc, sh, w_main, w_gate, gate_bias, rope_cc, rope_ss, layer, tm, main_dtype, tn=512, sub=256):
    rows, d = x.shape
    n = w_main.shape[2]
    rm = sc.shape[0]
    mod_block = (1, d) if rm == 1 else (tm, d)
    mod_map = (lambda i, j: (0, 0)) if rm == 1 else (lambda i, j: (i, 0))
    q_lo = (3 * W_HEADS) // tn
    k_lo = (4 * W_HEADS) // tn
    v_lo = (5 * W_HEADS) // tn
    kern = functools.partial(_inproj_kernel, tn=tn, q_lo=q_lo, k_lo=k_lo, v_lo=v_lo, sub=min(sub, tm))
    return pl.pallas_call(
        kern,
        grid=(rows // tm, n // tn),
        in_specs=[
            pl.BlockSpec((tm, d), lambda i, j: (i, 0)),
            pl.BlockSpec(mod_block, mod_map),
            pl.BlockSpec(mod_block, mod_map),
            pl.BlockSpec((None, d, tn), lambda i, j: (layer, 0, j)),
            pl.BlockSpec((None, d, N_GATE_PAD), lambda i, j: (layer, 0, 0)),
            pl.BlockSpec((None, 1, N_GATE_PAD), lambda i, j: (layer, 0, 0)),
            pl.BlockSpec((tm, HEAD_DIM), lambda i, j: (i, 0)),
            pl.BlockSpec((tm, HEAD_DIM), lambda i, j: (i, 0)),
        ],
        out_specs=[
            pl.BlockSpec((tm, tn), lambda i, j: (i, j)),
            pl.BlockSpec((tm, tn), lambda i, j: (i, jnp.maximum(j - k_lo, 0))),
            pl.BlockSpec((tm, N_GATE_PAD), lambda i, j: (i, 0)),
        ],
        out_shape=[
            jax.ShapeDtypeStruct((rows, n), main_dtype),
            jax.ShapeDtypeStruct((rows, 2 * W_HEADS), F32),
            jax.ShapeDtypeStruct((rows, N_GATE_PAD), F32),
        ],
        scratch_shapes=[pltpu.VMEM((tm, d), BF16)],
        compiler_params=_cparams("arbitrary", "arbitrary"),
        name="inproj",
    )(x, sc, sh, w_main, w_gate, gate_bias, rope_cc, rope_ss)


def _mlstm_kernel(q_ref, k_ref, v_ref, g_ref, c0_ref, n0_ref, m0_ref,
                  h_ref, c_ref, n_ref, m_ref, *, L):
    @pl.when(pl.program_id(1) == 0)
    def _():
        c_ref[...] = c0_ref[...]
        n_ref[...] = n0_ref[...]
        m_ref[...] = m0_ref[...]

    gates = g_ref[...]
    row = _iota((L, L), 0)
    col = _iota((L, L), 1)
    causal = row >= col
    tril = causal.astype(F32)
    triu = (row <= col).astype(F32)
    eye = (_iota((2 * H_M, N_GATE_PAD), 0) == _iota((2 * H_M, N_GATE_PAD), 1)).astype(F32)
    gates_t = _dot_nt(eye, gates, precision=HIGHEST)
    bcol_all = jnp.dot(tril, gates, precision=HIGHEST, preferred_element_type=F32)
    brow_all = jnp.dot(gates_t, triu, precision=HIGHEST, preferred_element_type=F32)

    for h in range(H_M):
        b_col = bcol_all[:, H_M + h:H_M + h + 1]
        b_row = brow_all[H_M + h:H_M + h + 1, :]
        ig_row = gates_t[h:h + 1, :]
        ig_col = gates[:, h:h + 1]
        m0 = m_ref[:, h:h + 1]
        c0 = c_ref[h]
        n0 = n_ref[h:h + 1, :]

        dmat = jnp.where(causal, b_col - b_row + ig_row, -jnp.inf)
        m_inter = b_col + m0
        m = jnp.maximum(m_inter, jnp.max(dmat, axis=1, keepdims=True))
        w = jnp.exp(dmat - m)
        g = jnp.exp(m_inter - m)

        q = q_ref[:, h * DK_M:(h + 1) * DK_M].astype(F32)
        k = k_ref[:, h * DK_M:(h + 1) * DK_M].astype(F32) * (DK_M ** -0.5)
        qb = q.astype(BF16)
        kb = k.astype(BF16)
        vb = v_ref[:, h * DV_M:(h + 1) * DV_M].astype(BF16)

        s = _dot_nt(qb, kb) * w
        num = g * _dot(qb, c0.astype(BF16)) + _dot(s.astype(BF16), vb)
        den = g * jnp.sum(q * n0, axis=1, keepdims=True) + jnp.sum(s, axis=1, keepdims=True)
        h_ref[:, h * DV_M:(h + 1) * DV_M] = num / jnp.maximum(jnp.abs(den), jnp.exp(-m))

        b_last = b_col[L - 1:L, :]
        a_col = b_last - b_col + ig_col
        m_end = jnp.maximum(b_last + m0, jnp.max(a_col, axis=0, keepdims=True))
        ws = jnp.exp(a_col - m_end)
        g_end = jnp.exp(b_last + m0 - m_end)
        kw = ws * k
        c_ref[h] = g_end * c0 + _dot_tn(kw.astype(BF16), vb)
        n_ref[h:h + 1, :] = g_end * n0 + jnp.sum(kw, axis=0, keepdims=True)
        m_ref[:, h:h + 1] = m_end


def _mlstm(main, gates, c0, n0, m0, batch, L):
    rows = main.shape[0]
    nc = rows // (batch * L)
    kern = functools.partial(_mlstm_kernel, L=L)
    state_map = lambda b, c: (b, 0, 0, 0)
    return pl.pallas_call(
        kern,
        grid=(batch, nc),
        in_specs=[
            pl.BlockSpec((L, H_M * DK_M), lambda b, c: (b * nc + c, 0)),
            pl.BlockSpec((L, H_M * DK_M), lambda b, c: (b * nc + c, 1)),
            pl.BlockSpec((L, H_M * DV_M), lambda b, c: (b * nc + c, 1)),
            pl.BlockSpec((L, N_GATE_PAD), lambda b, c: (b * nc + c, 0)),
            pl.BlockSpec((None, H_M, DK_M, DV_M), state_map),
            pl.BlockSpec((None, H_M, DK_M), lambda b, c: (b, 0, 0)),
            pl.BlockSpec((None, 1, H_M), lambda b, c: (b, 0, 0)),
        ],
        out_specs=[
            pl.BlockSpec((L, H_M * DV_M), lambda b, c: (b * nc + c, 0)),
            pl.BlockSpec((None, H_M, DK_M, DV_M), state_map),
            pl.BlockSpec((None, H_M, DK_M), lambda b, c: (b, 0, 0)),
            pl.BlockSpec((None, 1, H_M), lambda b, c: (b, 0, 0)),
        ],
        out_shape=[
            jax.ShapeDtypeStruct((rows, H_M * DV_M), F32),
            jax.ShapeDtypeStruct((batch, H_M, DK_M, DV_M), F32),
            jax.ShapeDtypeStruct((batch, H_M, DK_M), F32),
            jax.ShapeDtypeStruct((batch, 1, H_M), F32),
        ],
        compiler_params=_cparams("arbitrary", "arbitrary"),
        name="mlstm",
    )(main, main, main, gates, c0, n0, m0.reshape(batch, 1, H_M))


def _attn_prompt_kernel(q_ref, kp_ref, kc_ref, vp_ref, vc_ref, o_ref,
                        qq_scr, kk_scr, vv_scr, num_scr, m_scr, den_scr):
    sb = pl.program_id(0)
    SB = ATT_SUPER
    qq_scr[...] = q_ref[...].astype(F32)
    kk_scr[0:SB, :] = kp_ref[...].astype(F32)
    kk_scr[SB:, :] = kc_ref[...].astype(F32)
    vv_scr[0:SB, :] = vp_ref[...].astype(F32)
    vv_scr[SB:, :] = vc_ref[...].astype(F32)

    shape = (ATT_BLOCK, 2 * ATT_BLOCK)
    qi = _iota(shape, 0)
    ki = _iota(shape, 1)
    window = (ki >= qi) & (ki <= qi + ATT_BLOCK)
    window_first = window & ((ki >= ATT_BLOCK) | (sb > 0))

    for bi, (_, dil) in enumerate(DILATIONS):
        for n in range(SB // (ATT_BLOCK * dil)):
            valid = window_first if n == 0 else window

            for r in range(dil):
                q_rows = pl.ds(r + n * ATT_BLOCK * dil, ATT_BLOCK, stride=dil)
                k_rows = pl.ds(SB + r + (n - 1) * ATT_BLOCK * dil, 2 * ATT_BLOCK, stride=dil)
                q = qq_scr[q_rows, :].astype(BF16)
                k = kk_scr[k_rows, :].astype(BF16)
                v = vv_scr[k_rows, :].astype(BF16)
                s = jnp.where(valid, _dot_nt(q, k), -jnp.inf)
                m = jnp.max(s, axis=1, keepdims=True)
                p = jnp.exp(s - m)
                den = jnp.sum(p, axis=1, keepdims=True)
                num_scr[bi, q_rows, :] = _dot(p.astype(BF16), v)
                m_scr[bi, q_rows, :] = jnp.broadcast_to(m, (ATT_BLOCK, HEAD_DIM))
                den_scr[bi, q_rows, :] = jnp.broadcast_to(den, (ATT_BLOCK, HEAD_DIM))

    rows_per_step = 2 * ATT_BLOCK

    def mix(i, carry):
        rows = pl.ds(pl.multiple_of(i * rows_per_step, rows_per_step), rows_per_step)
        ms = [m_scr[b, rows, :] for b in range(len(DILATIONS))]
        big = ms[0]
        for mm in ms[1:]:
            big = jnp.maximum(big, mm)
        w0 = jnp.exp(ms[0] - big)
        acc_num = w0 * num_scr[0, rows, :]
        acc_den = w0 * den_scr[0, rows, :]
        for b in range(1, len(DILATIONS)):
            w = jnp.exp(ms[b] - big)
            acc_num = acc_num + w * num_scr[b, rows, :]
            acc_den = acc_den + w * den_scr[b, rows, :]
        o_ref[rows, :] = acc_num / acc_den
        return carry

    lax.fori_loop(0, SB // rows_per_step, mix, 0)


def _attn_prompt(main):
    s, n_main = main.shape
    SB = ATT_SUPER
    cols = n_main // HEAD_DIM // 6
    blk = (SB, HEAD_DIM)
    prev = lambda i: jnp.maximum(i - 1, 0)
    nbr = len(DILATIONS)
    return pl.pallas_call(
        _attn_prompt_kernel,
        grid=(s // SB, H_A),
        in_specs=[
            pl.BlockSpec(blk, lambda i, h: (i, 3 * cols + h)),
            pl.BlockSpec(blk, lambda i, h: (prev(i), 4 * cols + h)),
            pl.BlockSpec(blk, lambda i, h: (i, 4 * cols + h)),
            pl.BlockSpec(blk, lambda i, h: (prev(i), 5 * cols + h)),
            pl.BlockSpec(blk, lambda i, h: (i, 5 * cols + h)),
        ],
        out_specs=pl.BlockSpec(blk, lambda i, h: (i, h)),
        out_shape=jax.ShapeDtypeStruct((s, W_HEADS), F32),
        scratch_shapes=[
            pltpu.VMEM((SB, HEAD_DIM), F32),
            pltpu.VMEM((2 * SB, HEAD_DIM), F32),
            pltpu.VMEM((2 * SB, HEAD_DIM), F32),
            pltpu.VMEM((nbr, SB, HEAD_DIM), F32),
            pltpu.VMEM((nbr, SB, HEAD_DIM), F32),
            pltpu.VMEM((nbr, SB, HEAD_DIM), F32),
        ],
        compiler_params=_cparams("arbitrary", "arbitrary"),
        name="attn_prompt",
    )(main, main, main, main, main)


def _branch_count(delta):
    cnt = jnp.zeros(delta.shape, F32)
    for window, dil in DILATIONS:
        hit = (delta >= 0) & (delta <= window) & ((delta & (dil - 1)) == 0)
        cnt = cnt + hit.astype(F32)
    return cnt


def _attn_sample_kernel(q_ref, kn_ref, vn_ref, ck_ref, cv_ref, o_ref, cnt_scr, s_scr, *, T, NB, chunk):
    R = H_A * T
    NF = NB * H_A
    t_bits = T.bit_length() - 1
    h_bits = H_A.bit_length() - 1
    chunks = [slice(c * chunk, (c + 1) * chunk) for c in range(NF // chunk)]

    @pl.when(pl.program_id(0) == 0)
    def _():
        for c, sl in enumerate(chunks):
            row = _iota((R, chunk), 0)
            col = c * chunk + _iota((R, chunk), 1)
            delta = NB + (row & (T - 1)) - (col >> h_bits)
            same_head = (row >> t_bits) == (col & (H_A - 1))
            cnt_scr[:, sl] = jnp.where(same_head, _branch_count(delta), 0.0)

    heads = [slice(h * HEAD_DIM, (h + 1) * HEAD_DIM) for h in range(H_A)]
    q = jnp.concatenate([q_ref[:, sl] for sl in heads], axis=0).astype(BF16)
    kn = jnp.concatenate([kn_ref[:, sl] for sl in heads], axis=0).astype(BF16)
    vn = jnp.concatenate([vn_ref[:, sl] for sl in heads], axis=0).astype(BF16)
    row = _iota((R, R), 0)
    col = _iota((R, R), 1)
    cnt_n = jnp.where((row >> t_bits) == (col >> t_bits),
                      _branch_count((row & (T - 1)) - (col & (T - 1))), 0.0)

    s_n = jnp.where(cnt_n > 0, _dot_nt(q, kn), -jnp.inf)
    big = jnp.max(s_n, axis=1, keepdims=True)
    for sl in chunks:
        s = jnp.where(cnt_scr[:, sl] > 0, _dot_nt(q, ck_ref[sl, :].astype(BF16)), -jnp.inf)
        s_scr[:, sl] = s
        big = jnp.maximum(big, jnp.max(s, axis=1, keepdims=True))

    p_n = cnt_n * jnp.exp(s_n - big)
    den = jnp.sum(p_n, axis=1, keepdims=True)
    num = _dot(p_n.astype(BF16), vn)
    for sl in chunks:
        p = cnt_scr[:, sl] * jnp.exp(s_scr[:, sl] - big)
        den = den + jnp.sum(p, axis=1, keepdims=True)
        num = num + _dot(p.astype(BF16), cv_ref[sl, :].astype(BF16))
    out = num / den
    for h, sl in enumerate(heads):
        o_ref[:, sl] = out[h * T:(h + 1) * T, :]


def _attn_sample(main, cache_k, cache_v, layer, batch, T, chunk=2048):
    depth, _, nb, n_heads, hd = cache_k.shape
    assert n_heads == H_A and hd == HEAD_DIM and T & (T - 1) == 0 and H_A & (H_A - 1) == 0
    nf = nb * H_A
    ck = cache_k.reshape(depth, batch, nf, HEAD_DIM)
    cv = cache_v.reshape(depth, batch, nf, HEAD_DIM)
    kern = functools.partial(_attn_sample_kernel, T=T, NB=nb, chunk=chunk)
    cache_spec = pl.BlockSpec((None, None, nf, HEAD_DIM), lambda b: (layer, b, 0, 0))
    return pl.pallas_call(
        kern,
        grid=(batch,),
        in_specs=[
            pl.BlockSpec((T, W_HEADS), lambda b: (b, 3)),
            pl.BlockSpec((T, W_HEADS), lambda b: (b, 4)),
            pl.BlockSpec((T, W_HEADS), lambda b: (b, 5)),
            cache_spec,
            cache_spec,
        ],
        out_specs=pl.BlockSpec((T, W_HEADS), lambda b: (b, 0)),
        out_shape=jax.ShapeDtypeStruct((batch * T, W_HEADS), F32),
        scratch_shapes=[pltpu.VMEM((H_A * T, nf), F32), pltpu.VMEM((H_A * T, nf), F32)],
        compiler_params=_cparams("arbitrary"),
        name="attn_sample",
    )(main, main, main, ck, cv)


def _layer_norm(y, g, b):
    mu = jnp.mean(y, axis=1, keepdims=True)
    yc = y - mu
    var = jnp.mean(yc * yc, axis=1, keepdims=True)
    return yc * lax.rsqrt(var + LN_EPS) * g + b


def _merge_kernel(h_ref, om_ref, att_ref, x_ref, gt_ref, gh_ref, w_ref, lg_ref, lb_ref,
                  o_ref, cat_scr, *, alpha, sub):
    for r0 in range(0, cat_scr.shape[0], sub):
        rs = slice(r0, r0 + sub)
        for h in range(H_M):
            sl = slice(h * DV_M, (h + 1) * DV_M)
            hh = h_ref[rs, sl]
            hn = hh * lax.rsqrt(jnp.mean(hh * hh, axis=1, keepdims=True) + HEAD_NORM_EPS) * gh_ref[:, sl]
            cat_scr[rs, sl] = (hn * jax.nn.sigmoid(om_ref[rs, sl].astype(F32))).astype(BF16)
        cat_scr[rs, H_M * DV_M:] = att_ref[rs, :].astype(BF16)
        mix = _dot(cat_scr[rs, :], w_ref[...])
        gt = gt_ref[...] if gt_ref.shape[0] == 1 else gt_ref[rs, :]
        y = alpha * x_ref[rs, :] + (1.0 + gt) * mix
        o_ref[rs, :] = _layer_norm(y, lg_ref[...], lb_ref[...])


def _merge(h, main, att, x, gt, g_head, w_out, ln_g, ln_b, layer, alpha, tm, sub=256):
    rows, d = x.shape
    rm = gt.shape[0]
    mod_block = (1, d) if rm == 1 else (tm, d)
    mod_map = (lambda i: (0, 0)) if rm == 1 else (lambda i: (i, 0))
    wide = pl.BlockSpec((tm, W_HEADS), lambda i: (i, 0))
    const = lambda shape: pl.BlockSpec(shape, lambda i: (0, 0))
    return pl.pallas_call(
        functools.partial(_merge_kernel, alpha=alpha, sub=min(sub, tm)),
        grid=(rows // tm,),
        in_specs=[
            wide,
            pl.BlockSpec((tm, W_HEADS), lambda i: (i, 2)),
            wide,
            pl.BlockSpec((tm, d), lambda i: (i, 0)),
            pl.BlockSpec(mod_block, mod_map),
            const((1, W_HEADS)),
            pl.BlockSpec((None, d, d), lambda i: (layer, 0, 0)),
            const((1, d)),
            const((1, d)),
        ],
        out_specs=pl.BlockSpec((tm, d), lambda i: (i, 0)),
        out_shape=jax.ShapeDtypeStruct((rows, d), F32),
        scratch_shapes=[pltpu.VMEM((tm, d), BF16)],
        compiler_params=_cparams("arbitrary"),
        name="mixer_merge",
    )(h, main, att, x, gt, g_head, w_out, ln_g, ln_b)


def _ffn_kernel(*refs, alpha, seq_len, n_chunks, carry, sub):
    if carry:
        (x_ref, sc_ref, sh_ref, gt_ref, wg_ref, wv_ref, wd_ref, cw_ref, cb_ref, lg_ref, lb_ref,
         o_ref, tail_ref, u_scr, carry_scr) = refs
    else:
        (x_ref, sc_ref, sh_ref, gt_ref, wg_ref, wv_ref, wd_ref, cw_ref, cb_ref, lg_ref, lb_ref,
         h0_ref, h1_ref, o_ref, tail_ref, u_scr) = refs
    i = pl.program_id(0)
    j = pl.program_id(1)
    tm = u_scr.shape[0]
    assert n_chunks >= 2

    def rows_of(ref, rs):
        return ref[...] if ref.shape[0] == 1 else ref[rs, :]

    if carry:
        @pl.when(i == 0)
        def _():
            carry_scr[j] = jnp.zeros(carry_scr.shape[1:], F32)

    def step(first, last):
        cw = cw_ref[...]
        if carry:
            tail = carry_scr[j]
        for r0 in range(0, tm, sub):
            rs = slice(r0, r0 + sub)
            if first:
                ub = (x_ref[rs, :] * (1.0 + rows_of(sc_ref, rs)) + rows_of(sh_ref, rs)).astype(BF16)
                u_scr[rs, :] = ub
            else:
                ub = u_scr[rs, :]
            g = _dot(ub, wg_ref[...])
            v = _dot(ub, wv_ref[...])
            row = _iota(g.shape, 0)
            if carry:
                hist0, hist1 = tail[6:7, :], tail[7:8, :]
                pos = row
                tail = g[sub - 8:, :]
            else:
                hist0, hist1 = h0_ref[rs, :], h1_ref[rs, :]
                pos = row & (seq_len - 1)
                tail_ref[rs, :] = g
            prev1 = jnp.where(pos == 0, hist1, pltpu.roll(g, 1, 0))
            prev2 = jnp.where(pos == 0, hist0, jnp.where(pos == 1, hist1, pltpu.roll(g, 2, 0)))
            a = prev2 * cw[0:1, :] + prev1 * cw[1:2, :] + g * cw[2:3, :] + cb_ref[...]
            a = a * jax.nn.sigmoid(a) * v
            acc = _dot(a.astype(BF16), wd_ref[...])
            if not first:
                acc = o_ref[rs, :] + acc
            if last:
                y = alpha * x_ref[rs, :] + (1.0 + rows_of(gt_ref, rs)) * acc
                acc = _layer_norm(y, lg_ref[...], lb_ref[...])
            o_ref[rs, :] = acc
        if carry:
            carry_scr[j] = tail
            tail_ref[...] = tail

    pl.when(j == 0)(functools.partial(step, True, False))
    pl.when((j > 0) & (j < n_chunks - 1))(functools.partial(step, False, False))
    pl.when(j == n_chunks - 1)(functools.partial(step, False, True))


def _ffn(x, sc, sh, gt, w_up, w_down, conv_w, conv_b, ln_g, ln_b, layer, alpha, tm,
         hist=None, seq_len=None, tf=512, sub=256):
    rows, d = x.shape
    d_ff = w_down.shape[1]
    nj = d_ff // tf
    rm = sc.shape[0]
    mod_block = (1, d) if rm == 1 else (tm, d)
    mod_map = (lambda i, j: (0, 0)) if rm == 1 else (lambda i, j: (i, 0))
    carry = hist is None
    in_specs = [
        pl.BlockSpec((tm, d), lambda i, j: (i, 0), pipeline_mode=pl.Buffered(1)),
        pl.BlockSpec(mod_block, mod_map),
        pl.BlockSpec(mod_block, mod_map),
        pl.BlockSpec(mod_block, mod_map),
        pl.BlockSpec((None, d, tf), lambda i, j: (layer, 0, j)),
        pl.BlockSpec((None, d, tf), lambda i, j: (layer, 0, nj + j)),
        pl.BlockSpec((None, tf, d), lambda i, j: (layer, j, 0)),
        pl.BlockSpec((CONV_W, tf), lambda i, j: (0, j)),
        pl.BlockSpec((1, tf), lambda i, j: (0, j)),
        pl.BlockSpec((1, d), lambda i, j: (0, 0)),
        pl.BlockSpec((1, d), lambda i, j: (0, 0)),
    ]
    args = [x, sc, sh, gt, w_up, w_up, w_down, conv_w, conv_b, ln_g, ln_b]
    scratch = [pltpu.VMEM((tm, d), BF16)]
    if carry:
        tail_rows = 8
        scratch.append(pltpu.VMEM((nj, 8, tf), F32))
    else:
        tail_rows = tm
        in_specs += [pl.BlockSpec((tm, tf), lambda i, j: (i, j))] * 2
        args += list(hist)
    kern = functools.partial(_ffn_kernel, alpha=alpha, seq_len=seq_len, n_chunks=nj, carry=carry,
                             sub=min(sub, tm))
    return pl.pallas_call(
        kern,
        grid=(rows // tm, nj),
        in_specs=in_specs,
        out_specs=[
            pl.BlockSpec((tm, d), lambda i, j: (i, 0)),
            pl.BlockSpec((tail_rows, tf), lambda i, j: (i, j)),
        ],
        out_shape=[
            jax.ShapeDtypeStruct((rows, d), F32),
            jax.ShapeDtypeStruct((rows // tm * tail_rows, d_ff), F32),
        ],
        scratch_shapes=scratch,
        compiler_params=_cparams("arbitrary", "arbitrary"),
        name="conv_ffn",
    )(*args)


def _rope_tables(pos):
    half = HEAD_DIM // 2
    inv = ROPE_THETA ** (-jnp.arange(half, dtype=F32) / half)
    ang = pos.astype(F32)[:, None] * inv[None, :]
    cos, sin = jnp.cos(ang), jnp.sin(ang)
    return jnp.concatenate([cos, cos], -1), jnp.concatenate([-sin, sin], -1)


def kernel(x_prompt, x_sample, cache_k_win, cache_v_win, state_C, state_n, state_m, state_conv,
           c_prompt, c_sample, w_ada, b_ada, w_in, b_gate, g_head, w_out, ln1_g, ln1_b,
           w_up, conv_w, conv_b, w_down, ln2_g, ln2_b):
    bp, s, d = x_prompt.shape
    bs, t, _ = x_sample.shape
    depth = w_in.shape[0]
    d_ff = w_down.shape[1]
    alpha = (2 * depth) ** 0.25
    assert bp == 1 and d == (H_M + H_A) * HEAD_DIM and s % ATT_SUPER == 0

    n_c = bp + bs
    pad = (-n_c) % 8
    c_all = jnp.concatenate([c_prompt, c_sample, jnp.zeros((pad, d), F32)], 0)
    mod = _modulation(c_all, w_ada, b_ada)

    n_m = 2 * H_M * DK_M + 2 * H_M * DV_M
    w_main = jnp.concatenate([w_in[:, :, :n_m], w_in[:, :, n_m + 2 * H_M:]], -1).astype(BF16)
    w_gate = jnp.pad(w_in[:, :, n_m:n_m + 2 * H_M], ((0, 0), (0, 0), (0, N_GATE_PAD - 2 * H_M))).astype(BF16)
    gate_bias = jnp.pad(b_gate.reshape(depth, 1, 2 * H_M), ((0, 0), (0, 0), (0, N_GATE_PAD - 2 * H_M)))
    w_out_b = w_out.astype(BF16)
    w_up_b = w_up.astype(BF16)
    w_down_b = w_down.astype(BF16)

    cc_p, ss_p = _rope_tables(jnp.arange(s, dtype=jnp.int32))
    pos_s = PAST_LEN + jnp.arange(t, dtype=jnp.int32)
    cc_s, ss_s = (jnp.tile(a, (bs, 1)) for a in _rope_tables(pos_s))

    xp = x_prompt.reshape(s, d)
    xs = x_sample.reshape(bs * t, d)
    rows_s = bs * t
    tm_p = 1024
    outs =[[] for _ in range(12)]
    zeros_c = jnp.zeros((bp, H_M, DK_M, DV_M), F32)
    zeros_n = jnp.zeros((bp, H_M, DK_M), F32)
    zeros_m = jnp.zeros((bp, H_M), F32)
    row2 = lambda v: v.reshape(1, -1)

    for l in range(depth):
        mod_p = [mod[l, 0:1, k * d:(k + 1) * d] for k in range(6)]
        mod_s = [jnp.repeat(mod[l, bp:bp + bs, k * d:(k + 1) * d], t, axis=0) for k in range(6)]
        ln1 = (row2(ln1_g[l]), row2(ln1_b[l]))
        ffn_small = (conv_w[l], row2(conv_b[l]), row2(ln2_g[l]), row2(ln2_b[l]))

        main_p, kv_p, gates_p = _inproj(xp, mod_p[1], mod_p[0], w_main, w_gate, gate_bias, cc_p, ss_p, l,
                                        tm=tm_p, main_dtype=BF16)
        h_p, c_p, n_p, m_p = _mlstm(main_p, gates_p, zeros_c, zeros_n, zeros_m, batch=bp, L=128)
        att_p = _attn_prompt(main_p)
        x1_p = _merge(h_p, main_p, att_p, xp, mod_p[2], row2(g_head[l]), w_out_b, *ln1, l, alpha, tm=512)
        xp, tail_p = _ffn(x1_p, mod_p[4], mod_p[3], mod_p[5], w_up_b, w_down_b, *ffn_small, l, alpha, tm=tm_p)

        main_s, kv_s, gates_s = _inproj(xs, mod_s[1], mod_s[0], w_main, w_gate, gate_bias, cc_s, ss_s, l,
                                        tm=rows_s, main_dtype=F32)
        h_s, c_s, n_s, m_s = _mlstm(main_s, gates_s, state_C[l], state_n[l], state_m[l], batch=bs, L=t)
        att_s = _attn_sample(main_s, cache_k_win, cache_v_win, l, bs, t)
        x1_s = _merge(h_s, main_s, att_s, xs, mod_s[2], row2(g_head[l]), w_out_b, *ln1, l, alpha, tm=rows_s)
        hist = [jnp.repeat(state_conv[l][:, r, :], t, axis=0) for r in range(CONV_W - 1)]
        xs, g_s = _ffn(x1_s, mod_s[4], mod_s[3], mod_s[5], w_up_b, w_down_b, *ffn_small, l, alpha, tm=rows_s,
                       hist=hist, seq_len=t)

        wp = min(DILATIONS[-1][0], s)
        k_cols = slice(0, W_HEADS)
        v_cols = slice(W_HEADS, 2 * W_HEADS)
        outs[0].append(kv_p[s - wp:, k_cols].reshape(bp, wp, H_A, HEAD_DIM))
        outs[1].append(kv_p[s - wp:, v_cols].reshape(bp, wp, H_A, HEAD_DIM))
        outs[2].append(c_p)
        outs[3].append(n_p)
        outs[4].append(m_p.reshape(bp, H_M))
        outs[5].append(tail_p[-8:][8 - (CONV_W - 1):].reshape(bp, CONV_W - 1, d_ff))
        outs[6].append(kv_s[:, k_cols].reshape(bs, t, H_A, HEAD_DIM))
        outs[7].append(kv_s[:, v_cols].reshape(bs, t, H_A, HEAD_DIM))
        outs[8].append(c_s)
        outs[9].append(n_s)
        outs[10].append(m_s.reshape(bs, H_M))
        outs[11].append(g_s.reshape(bs, t, d_ff)[:, t - (CONV_W - 1):])

    return (xp.reshape(bp, s, d), xs.reshape(bs, t, d)) + tuple(jnp.stack(o) for o in outs)
```

```python
import functools

import jax
import jax.numpy as jnp
from jax import lax
from jax.experimental import pallas as pl
from jax.experimental.pallas import tpu as pltpu

F32 = jnp.float32
BF16 = jnp.bfloat16
HIGHEST = lax.Precision.HIGHEST

HEAD_DIM = 128
H_M = 8
H_A = 8
DK_M = 64
DV_M = 128
DILATIONS = ((128, 1), (512, 4), (2048, 16))
ATT_BLOCK = 128
PAST_LEN = 8192
ROPE_THETA = 10000.0
CONV_W = 3
LN_EPS = 1e-5
HEAD_NORM_EPS = 1e-6
N_GATE_PAD = 128
W_HEADS = H_A * HEAD_DIM
ATT_SUPER = ATT_BLOCK * DILATIONS[-1][1]
VMEM_LIMIT_BYTES = 56 * 1024 * 1024


def _cparams(*sem):
    return pltpu.CompilerParams(dimension_semantics=sem, vmem_limit_bytes=VMEM_LIMIT_BYTES)


def _dot(a, b):
    return jnp.dot(a, b, preferred_element_type=F32)


def _dot_nt(a, b, precision=None):
    return lax.dot_general(a, b, (((1,), (1,)), ((), ())), precision=precision,
                           preferred_element_type=F32)


def _dot_tn(a, b):
    return lax.dot_general(a, b, (((0,), (0,)), ((), ())), preferred_element_type=F32)


def _iota(shape, dim):
    return lax.broadcasted_iota(jnp.int32, shape, dim)


def _mod_kernel(c_ref, w_ref, b_ref, o_ref):
    c = c_ref[...]
    a = (c * jax.nn.sigmoid(c)).astype(BF16)
    o_ref[...] = _dot(a, w_ref[...].astype(BF16)) + b_ref[...]


def _modulation(c_all, w_ada, b_ada, tn=1024):
    depth, d, n = w_ada.shape
    rows = c_all.shape[0]
    return pl.pallas_call(
        _mod_kernel,
        grid=(depth, n // tn),
        in_specs=[
            pl.BlockSpec((rows, d), lambda l, j: (0, 0)),
            pl.BlockSpec((None, d, tn), lambda l, j: (l, 0, j)),
            pl.BlockSpec((None, 1, tn), lambda l, j: (l, 0, j)),
        ],
        out_specs=pl.BlockSpec((None, rows, tn), lambda l, j: (l, 0, j)),
        out_shape=jax.ShapeDtypeStruct((depth, rows, n), F32),
        compiler_params=_cparams("arbitrary", "arbitrary"),
        name="adaln_mod",
    )(c_all, w_ada, b_ada.reshape(depth, 1, n))


def _inproj_kernel(x_ref, sc_ref, sh_ref, w_ref, wg_ref, gb_ref, cc_ref, ss_ref,
                   main_ref, kv_ref, gate_ref, u_scr, *, tn, q_lo, k_lo, v_lo, sub):
    j = pl.program_id(1)
    tm = u_scr.shape[0]
    head_slices = [slice(g * HEAD_DIM, (g + 1) * HEAD_DIM) for g in range(tn // HEAD_DIM)]

    def rows_of(ref, rs):
        return ref[...] if ref.shape[0] == 1 else ref[rs, :]

    def step(kind, first):
        for r0 in range(0, tm, sub):
            rs = slice(r0, r0 + sub)
            if first:
                ub = (x_ref[rs, :] * (1.0 + rows_of(sc_ref, rs)) + rows_of(sh_ref, rs)).astype(BF16)
                u_scr[rs, :] = ub
                z = _dot(ub, wg_ref[...]) + gb_ref[...]
                lane = _iota(z.shape, 1)
                log_sig = jnp.minimum(z, 0.0) - jnp.log1p(jnp.exp(-jnp.abs(z)))
                gate_ref[rs, :] = jnp.where((lane >= H_M) & (lane < 2 * H_M), log_sig, z)
            else:
                ub = u_scr[rs, :]
            acc = _dot(ub, w_ref[...])

            def rope(a):
                return a * cc_ref[rs, :] + pltpu.roll(a, HEAD_DIM // 2, 1) * ss_ref[rs, :]

            if kind == "plain":
                main_ref[rs, :] = acc.astype(main_ref.dtype)
            elif kind == "q":
                for sl in head_slices:
                    main_ref[rs, sl] = (rope(acc[:, sl]) * (HEAD_DIM ** -0.5)).astype(main_ref.dtype)
            elif kind == "k":
                for sl in head_slices:
                    y = rope(acc[:, sl])
                    main_ref[rs, sl] = y.astype(main_ref.dtype)
                    kv_ref[rs, sl] = y
            else:
                main_ref[rs, :] = acc.astype(main_ref.dtype)
                kv_ref[rs, :] = acc

    pl.when(j == 0)(functools.partial(step, "plain", True))
    pl.when((j > 0) & (j < q_lo))(functools.partial(step, "plain", False))
    pl.when((j >= q_lo) & (j < k_lo))(functools.partial(step, "q", False))
    pl.when((j >= k_lo) & (j < v_lo))(functools.partial(step, "k", False))
    pl.when(j >= v_lo)(functools.partial(step, "v", False))


def _inproj(x, sc, sh, w_main, w_gate, gate_bias, rope_cc, rope_ss, layer, tm, main_dtype, tn=1024, sub=256):
    rows, d = x.shape
    n = w_main.shape[2]
    rm = sc.shape[0]
    mod_block = (1, d) if rm == 1 else (tm, d)
    mod_map = (lambda i, j: (0, 0)) if rm == 1 else (lambda i, j: (i, 0))
    q_lo = (3 * W_HEADS) // tn
    k_lo = (4 * W_HEADS) // tn
    v_lo = (5 * W_HEADS) // tn
    kern = functools.partial(_inproj_kernel, tn=tn, q_lo=q_lo, k_lo=k_lo, v_lo=v_lo, sub=min(sub, tm))
    return pl.pallas_call(
        kern,
        grid=(rows // tm, n // tn),
        in_specs=[
            pl.BlockSpec((tm, d), lambda i, j: (i, 0)),
            pl.BlockSpec(mod_block, mod_map),
            pl.BlockSpec(mod_block, mod_map),
            pl.BlockSpec((None, d, tn), lambda i, j: (layer, 0, j)),
            pl.BlockSpec((None, d, N_GATE_PAD), lambda i, j: (layer, 0, 0)),
            pl.BlockSpec((None, 1, N_GATE_PAD), lambda i, j: (layer, 0, 0)),
            pl.BlockSpec((tm, HEAD_DIM), lambda i, j: (i, 0)),
            pl.BlockSpec((tm, HEAD_DIM), lambda i, j: (i, 0)),
        ],
        out_specs=[
            pl.BlockSpec((tm, tn), lambda i, j: (i, j)),
            pl.BlockSpec((tm, tn), lambda i, j: (i, jnp.maximum(j - k_lo, 0))),
            pl.BlockSpec((tm, N_GATE_PAD), lambda i, j: (i, 0)),
        ],
        out_shape=[
            jax.ShapeDtypeStruct((rows, n), main_dtype),
            jax.ShapeDtypeStruct((rows, 2 * W_HEADS), F32),
            jax.ShapeDtypeStruct((rows, N_GATE_PAD), F32),
        ],
        scratch_shapes=[pltpu.VMEM((tm, d), BF16)],
        compiler_params=_cparams("arbitrary", "arbitrary"),
        name="inproj",
    )(x, sc, sh, w_main, w_gate, gate_bias, rope_cc, rope_ss)


def _mlstm_kernel(q_ref, k_ref, v_ref, g_ref, c0_ref, n0_ref, m0_ref,
                  h_ref, c_ref, n_ref, m_ref, *, L):
    @pl.when(pl.program_id(1) == 0)
    def _():
        c_ref[...] = c0_ref[...]
        n_ref[...] = n0_ref[...]
        m_ref[...] = m0_ref[...]

    gates = g_ref[...]
    row = _iota((L, L), 0)
    col = _iota((L, L), 1)
    causal = row >= col
    tril = causal.astype(F32)
    triu = (row <= col).astype(F32)
    eye = (_iota((2 * H_M, N_GATE_PAD), 0) == _iota((2 * H_M, N_GATE_PAD), 1)).astype(F32)
    gates_t = _dot_nt(eye, gates, precision=HIGHEST)
    bcol_all = jnp.dot(tril, gates, precision=HIGHEST, preferred_element_type=F32)
    brow_all = jnp.dot(gates_t, triu, precision=HIGHEST, preferred_element_type=F32)

    for h in range(H_M):
        b_col = bcol_all[:, H_M + h:H_M + h + 1]
        b_row = brow_all[H_M + h:H_M + h + 1, :]
        ig_row = gates_t[h:h + 1, :]
        ig_col = gates[:, h:h + 1]
        m0 = m_ref[:, h:h + 1]
        c0 = c_ref[h]
        n0 = n_ref[h:h + 1, :]

        dmat = jnp.where(causal, b_col - b_row + ig_row, -jnp.inf)
        m_inter = b_col + m0
        m = jnp.maximum(m_inter, jnp.max(dmat, axis=1, keepdims=True))
        w = jnp.exp(dmat - m)
        g = jnp.exp(m_inter - m)

        q = q_ref[:, h * DK_M:(h + 1) * DK_M].astype(F32)
        k = k_ref[:, h * DK_M:(h + 1) * DK_M].astype(F32) * (DK_M ** -0.5)
        qb = q.astype(BF16)
        kb = k.astype(BF16)
        vb = v_ref[:, h * DV_M:(h + 1) * DV_M].astype(BF16)

        s = _dot_nt(qb, kb) * w
        num = g * _dot(qb, c0.astype(BF16)) + _dot(s.astype(BF16), vb)
        den = g * jnp.sum(q * n0, axis=1, keepdims=True) + jnp.sum(s, axis=1, keepdims=True)
        h_ref[:, h * DV_M:(h + 1) * DV_M] = num / jnp.maximum(jnp.abs(den), jnp.exp(-m))

        b_last = b_col[L - 1:L, :]
        a_col = b_last - b_col + ig_col
        m_end = jnp.maximum(b_last + m0, jnp.max(a_col, axis=0, keepdims=True))
        ws = jnp.exp(a_col - m_end)
        g_end = jnp.exp(b_last + m0 - m_end)
        kw = ws * k
        c_ref[h] = g_end * c0 + _dot_tn(kw.astype(BF16), vb)
        n_ref[h:h + 1, :] = g_end * n0 + jnp.sum(kw, axis=0, keepdims=True)
        m_ref[:, h:h + 1] = m_end


def _mlstm_wide_kernel(q_ref, k_ref, v_ref, g_ref, c0_ref, n0_ref, m0_ref,
                       h_ref, c_ref, n_ref, m_ref, nmat_scr):
    L = HEAD_DIM
    first = pl.program_id(1) == 0
    sub8 = _iota((H_M, L), 0)

    @pl.when(first)
    def _():
        c_ref[...] = c0_ref[...]
        n_ref[...] = n0_ref[...]
        m_ref[...] = m0_ref[...]
        for h in range(H_M):
            nmat_scr[h] = lax.dot_general(n0_ref[...], (sub8 == h).astype(F32), (((0,), (0,)), ((), ())),
                                          precision=HIGHEST, preferred_element_type=F32)

    gates = g_ref[...]
    row = _iota((L, L), 0)
    col = _iota((L, L), 1)
    causal = row >= col
    triu = (row <= col).astype(F32)
    eye = (_iota((2 * H_M, N_GATE_PAD), 0) == _iota((2 * H_M, N_GATE_PAD), 1)).astype(F32)
    gates_t = _dot_nt(eye, gates, precision=HIGHEST)
    ig = gates_t[:H_M]
    b = jnp.dot(gates_t[H_M:], triu, precision=HIGHEST, preferred_element_type=F32)
    a = ig - b
    cm = a
    lane = _iota((H_M, L), 1)
    shift = 1
    while shift < L:
        cm = jnp.maximum(cm, jnp.where(lane >= shift, pltpu.roll(cm, shift, 1), -jnp.inf))
        shift *= 2

    eye8 = _iota((H_M, H_M), 0) == _iota((H_M, H_M), 1)
    m0_col = jnp.sum(jnp.where(eye8, jnp.broadcast_to(m_ref[...], (H_M, H_M)), 0.0), axis=1, keepdims=True)
    mm = jnp.maximum(m0_col, cm)
    mm_last = mm[:, L - 1:L]
    ws_all = jnp.exp(a - mm_last)
    g_end_all = jnp.exp(m0_col - mm_last)
    m_end = b[:, L - 1:L] + mm_last
    m_ref[...] = jnp.sum(jnp.where(eye8, jnp.broadcast_to(m_end, (H_M, H_M)), 0.0), axis=0, keepdims=True)
    def split(x, parts):
        out = []
        for _ in range(parts):
            p = x.astype(BF16)
            out.append(p)
            x = x - p.astype(F32)
        return out

    heads = range(H_M)
    ones_b = jnp.ones((L, L), BF16)

    cols = jnp.concatenate([-mm, -mm - b], axis=0).T
    n_sel = 3 * 2 * H_M
    sel_r = _iota((n_sel, H_M * 2 * L), 0) & (2 * H_M - 1)
    sel_c = _iota((n_sel, H_M * 2 * L), 1)
    sel_h = sel_c >> ((2 * L).bit_length() - 1)
    left = (sel_c & (2 * L - 1)) < L
    sel = (left & (sel_r == sel_h)) | (jnp.logical_not(left) & (sel_r == H_M + sel_h))
    bc = _dot(jnp.concatenate(split(cols, 3), axis=1), sel.astype(F32).astype(BF16))

    qb = [q_ref[:, h * DK_M:(h + 1) * DK_M].astype(BF16) for h in heads]
    kt = [(k_ref[:, h * DK_M:(h + 1) * DK_M].astype(F32) * (DK_M ** -0.5)).T for h in heads]
    vo = [jnp.concatenate([v_ref[:, h * DV_M:(h + 1) * DV_M].astype(BF16), ones_b], axis=1) for h in heads]
    s_raw = [_dot(qb[h], kt[h].astype(BF16)) for h in heads]
    inter = [_dot(qb[h], jnp.concatenate([c_ref[h].astype(BF16), nmat_scr[h].astype(BF16)], axis=1))
             for h in heads]

    for h in heads:
        neg_mm = bc[:, 2 * L * h:2 * L * h + L]
        neg_m = bc[:, 2 * L * h + L:2 * L * (h + 1)]
        w = jnp.where(causal, jnp.exp(neg_mm + a[h:h + 1, :]), 0.0)
        gmat = jnp.exp(neg_mm + m0_col[h:h + 1, :])
        s_hi, s_lo = split(s_raw[h] * w, 2)
        r = _dot(s_hi, vo[h])
        num = gmat * inter[h][:, :L] + r[:, :L]
        den = gmat * inter[h][:, L:] + r[:, L:] + _dot(s_lo, ones_b)
        h_ref[:, h * DV_M:(h + 1) * DV_M] = num / jnp.maximum(jnp.abs(den), jnp.exp(neg_m))

    for h in heads:
        g_end = g_end_all[h:h + 1, :]
        kw_hi, kw_lo = split(kt[h] * ws_all[h:h + 1, :], 2)
        u = _dot(kw_hi, vo[h])
        c_ref[h] = g_end * c_ref[h] + u[:, :L]
        nmat_scr[h] = g_end * nmat_scr[h] + u[:, L:] + _dot(kw_lo, ones_b)

    @pl.when(pl.program_id(1) == pl.num_programs(1) - 1)
    def _():
        for h in heads:
            n_ref[h:h + 1, :] = nmat_scr[h].T[0:1, :]


def _mlstm(main, gates, c0, n0, m0, batch, L):
    rows = main.shape[0]
    nc = rows // (batch * L)
    if L == HEAD_DIM:
        kern, scratch = _mlstm_wide_kernel, [pltpu.VMEM((H_M, DK_M, HEAD_DIM), F32)]
    else:
        kern, scratch = functools.partial(_mlstm_kernel, L=L), []
    state_map = lambda b, c: (b, 0, 0, 0)
    return pl.pallas_call(
        kern,
        grid=(batch, nc),
        in_specs=[
            pl.BlockSpec((L, H_M * DK_M), lambda b, c: (b * nc + c, 0)),
            pl.BlockSpec((L, H_M * DK_M), lambda b, c: (b * nc + c, 1)),
            pl.BlockSpec((L, H_M * DV_M), lambda b, c: (b * nc + c, 1)),
            pl.BlockSpec((L, N_GATE_PAD), lambda b, c: (b * nc + c, 0)),
            pl.BlockSpec((None, H_M, DK_M, DV_M), state_map),
            pl.BlockSpec((None, H_M, DK_M), lambda b, c: (b, 0, 0)),
            pl.BlockSpec((None, 1, H_M), lambda b, c: (b, 0, 0)),
        ],
        out_specs=[
            pl.BlockSpec((L, H_M * DV_M), lambda b, c: (b * nc + c, 0)),
            pl.BlockSpec((None, H_M, DK_M, DV_M), state_map),
            pl.BlockSpec((None, H_M, DK_M), lambda b, c: (b, 0, 0)),
            pl.BlockSpec((None, 1, H_M), lambda b, c: (b, 0, 0)),
        ],
        out_shape=[
            jax.ShapeDtypeStruct((rows, H_M * DV_M), F32),
            jax.ShapeDtypeStruct((batch, H_M, DK_M, DV_M), F32),
            jax.ShapeDtypeStruct((batch, H_M, DK_M), F32),
            jax.ShapeDtypeStruct((batch, 1, H_M), F32),
        ],
        scratch_shapes=scratch,
        compiler_params=_cparams("arbitrary", "arbitrary"),
        name="mlstm",
    )(main, main, main, gates, c0, n0, m0.reshape(batch, 1, H_M))


def _attn_prompt_kernel(q_ref, kp_ref, kc_ref, vp_ref, vc_ref, o_ref,
                        qq_scr, kk_scr, vv_scr, num_scr, m_scr, den_scr):
    sb = pl.program_id(0)
    SB = ATT_SUPER
    qq_scr[...] = q_ref[...].astype(F32)
    kk_scr[0:SB, :] = kp_ref[...].astype(F32)
    kk_scr[SB:, :] = kc_ref[...].astype(F32)
    vv_scr[0:SB, :] = vp_ref[...].astype(F32)
    vv_scr[SB:, :] = vc_ref[...].astype(F32)

    shape = (ATT_BLOCK, 2 * ATT_BLOCK)
    qi = _iota(shape, 0)
    ki = _iota(shape, 1)
    window = (ki >= qi) & (ki <= qi + ATT_BLOCK)
    window_first = window & ((ki >= ATT_BLOCK) | (sb > 0))

    for bi, (_, dil) in enumerate(DILATIONS):
        for n in range(SB // (ATT_BLOCK * dil)):
            valid = window_first if n == 0 else window

            for r in range(dil):
                q_rows = pl.ds(r + n * ATT_BLOCK * dil, ATT_BLOCK, stride=dil)
                k_rows = pl.ds(SB + r + (n - 1) * ATT_BLOCK * dil, 2 * ATT_BLOCK, stride=dil)
                q = qq_scr[q_rows, :].astype(BF16)
                k = kk_scr[k_rows, :].astype(BF16)
                v = vv_scr[k_rows, :].astype(BF16)
                s = jnp.where(valid, _dot_nt(q, k), -jnp.inf)
                m = jnp.max(s, axis=1, keepdims=True)
                p = jnp.exp(s - m)
                den = jnp.sum(p, axis=1, keepdims=True)
                num_scr[bi, q_rows, :] = _dot(p.astype(BF16), v)
                m_scr[bi, q_rows, :] = jnp.broadcast_to(m, (ATT_BLOCK, HEAD_DIM))
                den_scr[bi, q_rows, :] = jnp.broadcast_to(den, (ATT_BLOCK, HEAD_DIM))

    rows_per_step = 2 * ATT_BLOCK

    def mix(i, carry):
        rows = pl.ds(pl.multiple_of(i * rows_per_step, rows_per_step), rows_per_step)
        ms = [m_scr[b, rows, :] for b in range(len(DILATIONS))]
        big = ms[0]
        for mm in ms[1:]:
            big = jnp.maximum(big, mm)
        w0 = jnp.exp(ms[0] - big)
        acc_num = w0 * num_scr[0, rows, :]
        acc_den = w0 * den_scr[0, rows, :]
        for b in range(1, len(DILATIONS)):
            w = jnp.exp(ms[b] - big)
            acc_num = acc_num + w * num_scr[b, rows, :]
            acc_den = acc_den + w * den_scr[b, rows, :]
        o_ref[rows, :] = acc_num / acc_den
        return carry

    lax.fori_loop(0, SB // rows_per_step, mix, 0)


def _attn_prompt(main):
    s, n_main = main.shape
    SB = ATT_SUPER
    cols = n_main // HEAD_DIM // 6
    blk = (SB, HEAD_DIM)
    prev = lambda i: jnp.maximum(i - 1, 0)
    nbr = len(DILATIONS)
    return pl.pallas_call(
        _attn_prompt_kernel,
        grid=(s // SB, H_A),
        in_specs=[
            pl.BlockSpec(blk, lambda i, h: (i, 3 * cols + h)),
            pl.BlockSpec(blk, lambda i, h: (prev(i), 4 * cols + h)),
            pl.BlockSpec(blk, lambda i, h: (i, 4 * cols + h)),
            pl.BlockSpec(blk, lambda i, h: (prev(i), 5 * cols + h)),
            pl.BlockSpec(blk, lambda i, h: (i, 5 * cols + h)),
        ],
        out_specs=pl.BlockSpec(blk, lambda i, h: (i, h)),
        out_shape=jax.ShapeDtypeStruct((s, W_HEADS), F32),
        scratch_shapes=[
            pltpu.VMEM((SB, HEAD_DIM), F32),
            pltpu.VMEM((2 * SB, HEAD_DIM), F32),
            pltpu.VMEM((2 * SB, HEAD_DIM), F32),
            pltpu.VMEM((nbr, SB, HEAD_DIM), F32),
            pltpu.VMEM((nbr, SB, HEAD_DIM), F32),
            pltpu.VMEM((nbr, SB, HEAD_DIM), F32),
        ],
        compiler_params=_cparams("arbitrary", "arbitrary"),
        name="attn_prompt",
    )(main, main, main, main, main)


def _branch_count(delta):
    cnt = jnp.zeros(delta.shape, F32)
    for window, dil in DILATIONS:
        hit = (delta >= 0) & (delta <= window) & ((delta & (dil - 1)) == 0)
        cnt = cnt + hit.astype(F32)
    return cnt


def _attn_sample_kernel(q_ref, kn_ref, vn_ref, kfar_ref, knear_ref, vfar_ref, vnear_ref, o_ref,
                        cnt_scr, s_scr, *, T, NB, chunk):
    R = H_A * T
    t_bits = T.bit_length() - 1
    h_bits = H_A.bit_length() - 1
    d_far = DILATIONS[-1][1]
    n_groups, keep = kfar_ref.shape[0], kfar_ref.shape[1]
    keep_bits = keep.bit_length() - 1
    near_pos = knear_ref.shape[0] // H_A
    gpc = chunk // (keep * H_A)
    far_chunks = n_groups // gpc
    n_chunks = far_chunks + near_pos * H_A // chunk
    chunks = [slice(c * chunk, (c + 1) * chunk) for c in range(n_chunks)]

    def rows(far_ref, near_ref, c):
        if c < far_chunks:
            return far_ref[c * gpc:(c + 1) * gpc].reshape(chunk, HEAD_DIM)
        return near_ref[chunks[c - far_chunks], :]

    @pl.when(pl.program_id(0) == 0)
    def _():
        for c, sl in enumerate(chunks):
            row = _iota((R, chunk), 0)
            col = _iota((R, chunk), 1)
            if c < far_chunks:
                pos = (c * gpc + (col >> (h_bits + keep_bits))) * d_far + ((col >> h_bits) & (keep - 1))
            else:
                pos = NB - near_pos + (c - far_chunks) * (chunk // H_A) + (col >> h_bits)
            same_head = (row >> t_bits) == (col & (H_A - 1))
            cnt_scr[:, sl] = jnp.where(same_head, _branch_count(NB + (row & (T - 1)) - pos), 0.0)

    heads = [slice(h * HEAD_DIM, (h + 1) * HEAD_DIM) for h in range(H_A)]
    q = jnp.concatenate([q_ref[:, sl] for sl in heads], axis=0).astype(BF16)
    kn = jnp.concatenate([kn_ref[:, sl] for sl in heads], axis=0).astype(BF16)
    vn = jnp.concatenate([vn_ref[:, sl] for sl in heads], axis=0).astype(BF16)
    row = _iota((R, R), 0)
    col = _iota((R, R), 1)
    cnt_n = jnp.where((row >> t_bits) == (col >> t_bits),
                      _branch_count((row & (T - 1)) - (col & (T - 1))), 0.0)

    s_n = jnp.where(cnt_n > 0, _dot_nt(q, kn), -jnp.inf)
    big = jnp.max(s_n, axis=1, keepdims=True)
    for c, sl in enumerate(chunks):
        s = jnp.where(cnt_scr[:, sl] > 0, _dot_nt(q, rows(kfar_ref, knear_ref, c).astype(BF16)), -jnp.inf)
        s_scr[:, sl] = s
        big = jnp.maximum(big, jnp.max(s, axis=1, keepdims=True))

    p_n = cnt_n * jnp.exp(s_n - big)
    den = jnp.sum(p_n, axis=1, keepdims=True)
    num = _dot(p_n.astype(BF16), vn)
    for c, sl in enumerate(chunks):
        p = cnt_scr[:, sl] * jnp.exp(s_scr[:, sl] - big)
        den = den + jnp.sum(p, axis=1, keepdims=True)
        num = num + _dot(p.astype(BF16), rows(vfar_ref, vnear_ref, c).astype(BF16))
    out = num / den
    for h, sl in enumerate(heads):
        o_ref[:, sl] = out[h * T:(h + 1) * T, :]


def _attn_sample(main, cache_k, cache_v, layer, batch, T, chunk=2048):
    depth, _, nb, n_heads, hd = cache_k.shape
    assert n_heads == H_A and hd == HEAD_DIM and T & (T - 1) == 0 and H_A & (H_A - 1) == 0
    d_far = DILATIONS[-1][1]
    near_pos = DILATIONS[-2][0]
    assert all(w <= near_pos for w, _ in DILATIONS[:-1]) and nb % d_far == 0 and T <= d_far
    assert (nb - near_pos) % d_far == 0 and near_pos <= nb
    n_groups = (nb - near_pos) // d_far
    nf = nb * H_A
    n_keys = (n_groups * T + near_pos) * H_A
    assert (n_groups * T * H_A) % chunk == 0 and (near_pos * H_A) % chunk == 0 and nf % (near_pos * H_A) == 0
    far_view = lambda c: c.reshape(depth, batch, nb // d_far, d_far, H_A, HEAD_DIM)
    near_view = lambda c: c.reshape(depth, batch, nf, HEAD_DIM)
    kern = functools.partial(_attn_sample_kernel, T=T, NB=nb, chunk=chunk)
    far_spec = pl.BlockSpec((None, None, n_groups, T, H_A, HEAD_DIM), lambda b: (layer, b, 0, 0, 0, 0))
    near_spec = pl.BlockSpec((None, None, near_pos * H_A, HEAD_DIM),
                             lambda b: (layer, b, nf // (near_pos * H_A) - 1, 0))
    return pl.pallas_call(
        kern,
        grid=(batch,),
        in_specs=[
            pl.BlockSpec((T, W_HEADS), lambda b: (b, 3)),
            pl.BlockSpec((T, W_HEADS), lambda b: (b, 4)),
            pl.BlockSpec((T, W_HEADS), lambda b: (b, 5)),
            far_spec,
            near_spec,
            far_spec,
            near_spec,
        ],
        out_specs=pl.BlockSpec((T, W_HEADS), lambda b: (b, 0)),
        out_shape=jax.ShapeDtypeStruct((batch * T, W_HEADS), F32),
        scratch_shapes=[pltpu.VMEM((H_A * T, n_keys), F32), pltpu.VMEM((H_A * T, n_keys), F32)],
        compiler_params=_cparams("arbitrary"),
        name="attn_sample",
    )(main, main, main, far_view(cache_k), near_view(cache_k), far_view(cache_v), near_view(cache_v))


def _layer_norm(y, g, b):
    mu = jnp.mean(y, axis=1, keepdims=True)
    yc = y - mu
    var = jnp.mean(yc * yc, axis=1, keepdims=True)
    return yc * lax.rsqrt(var + LN_EPS) * g + b


def _merge_kernel(h_ref, om_ref, att_ref, x_ref, gt_ref, gh_ref, w_ref, lg_ref, lb_ref,
                  o_ref, cat_scr, *, alpha, sub):
    for r0 in range(0, cat_scr.shape[0], sub):
        rs = slice(r0, r0 + sub)
        for h in range(H_M):
            sl = slice(h * DV_M, (h + 1) * DV_M)
            hh = h_ref[rs, sl]
            hn = hh * lax.rsqrt(jnp.mean(hh * hh, axis=1, keepdims=True) + HEAD_NORM_EPS) * gh_ref[:, sl]
            cat_scr[rs, sl] = (hn * jax.nn.sigmoid(om_ref[rs, sl].astype(F32))).astype(BF16)
        cat_scr[rs, H_M * DV_M:] = att_ref[rs, :].astype(BF16)
        mix = _dot(cat_scr[rs, :], w_ref[...])
        gt = gt_ref[...] if gt_ref.shape[0] == 1 else gt_ref[rs, :]
        y = alpha * x_ref[rs, :] + (1.0 + gt) * mix
        o_ref[rs, :] = _layer_norm(y, lg_ref[...], lb_ref[...])


def _merge(h, main, att, x, gt, g_head, w_out, ln_g, ln_b, layer, alpha, tm, sub=256):
    rows, d = x.shape
    rm = gt.shape[0]
    mod_block = (1, d) if rm == 1 else (tm, d)
    mod_map = (lambda i: (0, 0)) if rm == 1 else (lambda i: (i, 0))
    wide = pl.BlockSpec((tm, W_HEADS), lambda i: (i, 0))
    const = lambda shape: pl.BlockSpec(shape, lambda i: (0, 0))
    return pl.pallas_call(
        functools.partial(_merge_kernel, alpha=alpha, sub=min(sub, tm)),
        grid=(rows // tm,),
        in_specs=[
            wide,
            pl.BlockSpec((tm, W_HEADS), lambda i: (i, 2)),
            wide,
            pl.BlockSpec((tm, d), lambda i: (i, 0)),
            pl.BlockSpec(mod_block, mod_map),
            const((1, W_HEADS)),
            pl.BlockSpec((None, d, d), lambda i: (layer, 0, 0)),
            const((1, d)),
            const((1, d)),
        ],
        out_specs=pl.BlockSpec((tm, d), lambda i: (i, 0)),
        out_shape=jax.ShapeDtypeStruct((rows, d), F32),
        scratch_shapes=[pltpu.VMEM((tm, d), BF16)],
        compiler_params=_cparams("arbitrary"),
        name="mixer_merge",
    )(h, main, att, x, gt, g_head, w_out, ln_g, ln_b)


def _ffn_kernel(*refs, alpha, seq_len, n_chunks, carry, sub):
    if carry:
        (x_ref, sc_ref, sh_ref, gt_ref, wg_ref, wv_ref, wd_ref, cw_ref, cb_ref, lg_ref, lb_ref,
         o_ref, tail_ref, u_scr, carry_scr) = refs
    else:
        (x_ref, sc_ref, sh_ref, gt_ref, wg_ref, wv_ref, wd_ref, cw_ref, cb_ref, lg_ref, lb_ref,
         h0_ref, h1_ref, o_ref, tail_ref, u_scr) = refs
    i = pl.program_id(0)
    j = pl.program_id(1)
    tm = u_scr.shape[0]
    assert n_chunks >= 2

    def rows_of(ref, rs):
        return ref[...] if ref.shape[0] == 1 else ref[rs, :]

    if carry:
        @pl.when(i == 0)
        def _():
            carry_scr[j] = jnp.zeros(carry_scr.shape[1:], F32)

    def step(first, last):
        cw = cw_ref[...]
        if carry:
            tail = carry_scr[j]
        for r0 in range(0, tm, sub):
            rs = slice(r0, r0 + sub)
            if first:
                ub = (x_ref[rs, :] * (1.0 + rows_of(sc_ref, rs)) + rows_of(sh_ref, rs)).astype(BF16)
                u_scr[rs, :] = ub
            else:
                ub = u_scr[rs, :]
            g = _dot(ub, wg_ref[...])
            v = _dot(ub, wv_ref[...])
            row = _iota(g.shape, 0)
            if carry:
                hist0, hist1 = tail[6:7, :], tail[7:8, :]
                pos = row
                tail = g[sub - 8:, :]
            else:
                hist0, hist1 = h0_ref[rs, :], h1_ref[rs, :]
                pos = row & (seq_len - 1)
                tail_ref[rs, :] = g
            prev1 = jnp.where(pos == 0, hist1, pltpu.roll(g, 1, 0))
            prev2 = jnp.where(pos == 0, hist0, jnp.where(pos == 1, hist1, pltpu.roll(g, 2, 0)))
            a = prev2 * cw[0:1, :] + prev1 * cw[1:2, :] + g * cw[2:3, :] + cb_ref[...]
            a = a * jax.nn.sigmoid(a) * v
            acc = _dot(a.astype(BF16), wd_ref[...])
            if not first:
                acc = o_ref[rs, :] + acc
            if last:
                y = alpha * x_ref[rs, :] + (1.0 + rows_of(gt_ref, rs)) * acc
                acc = _layer_norm(y, lg_ref[...], lb_ref[...])
            o_ref[rs, :] = acc
        if carry:
            carry_scr[j] = tail
            tail_ref[...] = tail

    pl.when(j == 0)(functools.partial(step, True, False))
    pl.when((j > 0) & (j < n_chunks - 1))(functools.partial(step, False, False))
    pl.when(j == n_chunks - 1)(functools.partial(step, False, True))


def _ffn(x, sc, sh, gt, w_up, w_down, conv_w, conv_b, ln_g, ln_b, layer, alpha, tm,
         hist=None, seq_len=None, tf=512, sub=256):
    rows, d = x.shape
    d_ff = w_down.shape[1]
    nj = d_ff // tf
    rm = sc.shape[0]
    mod_block = (1, d) if rm == 1 else (tm, d)
    mod_map = (lambda i, j: (0, 0)) if rm == 1 else (lambda i, j: (i, 0))
    carry = hist is None
    in_specs = [
        pl.BlockSpec((tm, d), lambda i, j: (i, 0), pipeline_mode=pl.Buffered(1)),
        pl.BlockSpec(mod_block, mod_map),
        pl.BlockSpec(mod_block, mod_map),
        pl.BlockSpec(mod_block, mod_map),
        pl.BlockSpec((None, d, tf), lambda i, j: (layer, 0, j)),
        pl.BlockSpec((None, d, tf), lambda i, j: (layer, 0, nj + j)),
        pl.BlockSpec((None, tf, d), lambda i, j: (layer, j, 0)),
        pl.BlockSpec((CONV_W, tf), lambda i, j: (0, j)),
        pl.BlockSpec((1, tf), lambda i, j: (0, j)),
        pl.BlockSpec((1, d), lambda i, j: (0, 0)),
        pl.BlockSpec((1, d), lambda i, j: (0, 0)),
    ]
    args = [x, sc, sh, gt, w_up, w_up, w_down, conv_w, conv_b, ln_g, ln_b]
    scratch = [pltpu.VMEM((tm, d), BF16)]
    if carry:
        tail_rows = 8
        scratch.append(pltpu.VMEM((nj, 8, tf), F32))
    else:
        tail_rows = tm
        in_specs += [pl.BlockSpec((tm, tf), lambda i, j: (i, j))] * 2
        args += list(hist)
    kern = functools.partial(_ffn_kernel, alpha=alpha, seq_len=seq_len, n_chunks=nj, carry=carry,
                             sub=min(sub, tm))
    return pl.pallas_call(
        kern,
        grid=(rows // tm, nj),
        in_specs=in_specs,
        out_specs=[
            pl.BlockSpec((tm, d), lambda i, j: (i, 0)),
            pl.BlockSpec((tail_rows, tf), lambda i, j: (i, j)),
        ],
        out_shape=[
            jax.ShapeDtypeStruct((rows, d), F32),
            jax.ShapeDtypeStruct((rows // tm * tail_rows, d_ff), F32),
        ],
        scratch_shapes=scratch,
        compiler_params=_cparams("arbitrary", "arbitrary"),
        name="conv_ffn",
    )(*args)


def _rope_tables(pos):
    half = HEAD_DIM // 2
    inv = ROPE_THETA ** (-jnp.arange(half, dtype=F32) / half)
    ang = pos.astype(F32)[:, None] * inv[None, :]
    cos, sin = jnp.cos(ang), jnp.sin(ang)
    return jnp.concatenate([cos, cos], -1), jnp.concatenate([-sin, sin], -1)


def kernel(x_prompt, x_sample, cache_k_win, cache_v_win, state_C, state_n, state_m, state_conv,
           c_prompt, c_sample, w_ada, b_ada, w_in, b_gate, g_head, w_out, ln1_g, ln1_b,
           w_up, conv_w, conv_b, w_down, ln2_g, ln2_b):
    bp, s, d = x_prompt.shape
    bs, t, _ = x_sample.shape
    depth = w_in.shape[0]
    d_ff = w_down.shape[1]
    alpha = (2 * depth) ** 0.25
    assert bp == 1 and d == (H_M + H_A) * HEAD_DIM and s % ATT_SUPER == 0

    n_c = bp + bs
    pad = (-n_c) % 8
    c_all = jnp.concatenate([c_prompt, c_sample, jnp.zeros((pad, d), F32)], 0)
    mod = _modulation(c_all, w_ada, b_ada)

    n_m = 2 * H_M * DK_M + 2 * H_M * DV_M
    w_main = jnp.concatenate([w_in[:, :, :n_m], w_in[:, :, n_m + 2 * H_M:]], -1).astype(BF16)
    w_gate = jnp.pad(w_in[:, :, n_m:n_m + 2 * H_M], ((0, 0), (0, 0), (0, N_GATE_PAD - 2 * H_M))).astype(BF16)
    gate_bias = jnp.pad(b_gate.reshape(depth, 1, 2 * H_M), ((0, 0), (0, 0), (0, N_GATE_PAD - 2 * H_M)))
    w_out_b = w_out.astype(BF16)
    w_up_b = w_up.astype(BF16)
    w_down_b = w_down.astype(BF16)

    cc_p, ss_p = _rope_tables(jnp.arange(s, dtype=jnp.int32))
    pos_s = PAST_LEN + jnp.arange(t, dtype=jnp.int32)
    cc_s, ss_s = (jnp.tile(a, (bs, 1)) for a in _rope_tables(pos_s))

    xp = x_prompt.reshape(s, d)
    xs = x_sample.reshape(bs * t, d)
    rows_s = bs * t
    tm_p = 1024
    outs =[[] for _ in range(12)]
    zeros_c = jnp.zeros((bp, H_M, DK_M, DV_M), F32)
    zeros_n = jnp.zeros((bp, H_M, DK_M), F32)
    zeros_m = jnp.zeros((bp, H_M), F32)
    row2 = lambda v: v.reshape(1, -1)

    for l in range(depth):
        mod_p = [mod[l, 0:1, k * d:(k + 1) * d] for k in range(6)]
        mod_s = [jnp.repeat(mod[l, bp:bp + bs, k * d:(k + 1) * d], t, axis=0) for k in range(6)]
        ln1 = (row2(ln1_g[l]), row2(ln1_b[l]))
        ffn_small = (conv_w[l], row2(conv_b[l]), row2(ln2_g[l]), row2(ln2_b[l]))

        main_p, kv_p, gates_p = _inproj(xp, mod_p[1], mod_p[0], w_main, w_gate, gate_bias, cc_p, ss_p, l,
                                        tm=tm_p, main_dtype=BF16)
        h_p, c_p, n_p, m_p = _mlstm(main_p, gates_p, zeros_c, zeros_n, zeros_m, batch=bp, L=128)
        att_p = _attn_prompt(main_p)
        x1_p = _merge(h_p, main_p, att_p, xp, mod_p[2], row2(g_head[l]), w_out_b, *ln1, l, alpha, tm=512)
        xp, tail_p = _ffn(x1_p, mod_p[4], mod_p[3], mod_p[5], w_up_b, w_down_b, *ffn_small, l, alpha, tm=tm_p)

        main_s, kv_s, gates_s = _inproj(xs, mod_s[1], mod_s[0], w_main, w_gate, gate_bias, cc_s, ss_s, l,
                                        tm=rows_s, main_dtype=F32)
        h_s, c_s, n_s, m_s = _mlstm(main_s, gates_s, state_C[l], state_n[l], state_m[l], batch=bs, L=t)
        att_s = _attn_sample(main_s, cache_k_win, cache_v_win, l, bs, t)
        x1_s = _merge(h_s, main_s, att_s, xs, mod_s[2], row2(g_head[l]), w_out_b, *ln1, l, alpha, tm=rows_s)
        hist = [jnp.repeat(state_conv[l][:, r, :], t, axis=0) for r in range(CONV_W - 1)]
        xs, g_s = _ffn(x1_s, mod_s[4], mod_s[3], mod_s[5], w_up_b, w_down_b, *ffn_small, l, alpha, tm=rows_s,
                       hist=hist, seq_len=t)

        wp = min(DILATIONS[-1][0], s)
        k_cols = slice(0, W_HEADS)
        v_cols = slice(W_HEADS, 2 * W_HEADS)
        outs[0].append(kv_p[s - wp:, k_cols].reshape(bp, wp, H_A, HEAD_DIM))
        outs[1].append(kv_p[s - wp:, v_cols].reshape(bp, wp, H_A, HEAD_DIM))
        outs[2].append(c_p)
        outs[3].append(n_p)
        outs[4].append(m_p.reshape(bp, H_M))
        outs[5].append(tail_p[-8:][8 - (CONV_W - 1):].reshape(bp, CONV_W - 1, d_ff))
        outs[6].append(kv_s[:, k_cols].reshape(bs, t, H_A, HEAD_DIM))
        outs[7].append(kv_s[:, v_cols].reshape(bs, t, H_A, HEAD_DIM))
        outs[8].append(c_s)
        outs[9].append(n_s)
        outs[10].append(m_s.reshape(bs, H_M))
        outs[11].append(g_s.reshape(bs, t, d_ff)[:, t - (CONV_W - 1):])

    return (xp.reshape(bp, s, d), xs.reshape(bs, t, d)) + tuple(jnp.stack(o) for o in outs)
```

```python
import functools

import jax
import jax.numpy as jnp
from jax import lax
from jax.experimental import pallas as pl
from jax.experimental.pallas import tpu as pltpu

F32 = jnp.float32
BF16 = jnp.bfloat16
HIGHEST = lax.Precision.HIGHEST

HEAD_DIM = 128
H_M = 8
H_A = 8
DK_M = 64
DV_M = 128
DILATIONS = ((128, 1), (512, 4), (2048, 16))
ATT_BLOCK = 128
PAST_LEN = 8192
ROPE_THETA = 10000.0
CONV_W = 3
LN_EPS = 1e-5
HEAD_NORM_EPS = 1e-6
N_GATE_PAD = 128
W_HEADS = H_A * HEAD_DIM
ATT_SUPER = ATT_BLOCK * DILATIONS[-1][1]
MLSTM_SEQS_PER_STEP = 4
VMEM_LIMIT_BYTES = 56 * 1024 * 1024


def _cparams(*sem):
    return pltpu.CompilerParams(dimension_semantics=sem, vmem_limit_bytes=VMEM_LIMIT_BYTES)


def _dot(a, b):
    return jnp.dot(a, b, preferred_element_type=F32)


def _dot_nt(a, b, precision=None):
    return lax.dot_general(a, b, (((1,), (1,)), ((), ())), precision=precision,
                           preferred_element_type=F32)


def _dot_tn(a, b):
    return lax.dot_general(a, b, (((0,), (0,)), ((), ())), preferred_element_type=F32)


def _iota(shape, dim):
    return lax.broadcasted_iota(jnp.int32, shape, dim)


def _mod_kernel(c_ref, w_ref, b_ref, o_ref):
    c = c_ref[...]
    a = (c * jax.nn.sigmoid(c)).astype(BF16)
    o_ref[...] = _dot(a, w_ref[...].astype(BF16)) + b_ref[...]


def _modulation(c_all, w_ada, b_ada, tn=1024):
    depth, d, n = w_ada.shape
    rows = c_all.shape[0]
    return pl.pallas_call(
        _mod_kernel,
        grid=(depth, n // tn),
        in_specs=[
            pl.BlockSpec((rows, d), lambda l, j: (0, 0)),
            pl.BlockSpec((None, d, tn), lambda l, j: (l, 0, j)),
            pl.BlockSpec((None, 1, tn), lambda l, j: (l, 0, j)),
        ],
        out_specs=pl.BlockSpec((None, rows, tn), lambda l, j: (l, 0, j)),
        out_shape=jax.ShapeDtypeStruct((depth, rows, n), F32),
        compiler_params=_cparams("arbitrary", "arbitrary"),
        name="adaln_mod",
    )(c_all, w_ada, b_ada.reshape(depth, 1, n))


def _inproj_kernel(x_ref, sc_ref, sh_ref, wm_ref, wa_ref, wg_ref, gb_ref, cc_ref, ss_ref,
                   main_ref, kv_ref, gate_ref, u_scr, *, tn, q_lo, k_lo, v_lo, sub):
    j = pl.program_id(1)
    tm = u_scr.shape[0]
    head_slices = [slice(g * HEAD_DIM, (g + 1) * HEAD_DIM) for g in range(tn // HEAD_DIM)]

    def rows_of(ref, rs):
        return ref[...] if ref.shape[0] == 1 else ref[rs, :]

    def step(kind, first):
        for r0 in range(0, tm, sub):
            rs = slice(r0, r0 + sub)
            if first:
                ub = (x_ref[rs, :] * (1.0 + rows_of(sc_ref, rs)) + rows_of(sh_ref, rs)).astype(BF16)
                u_scr[rs, :] = ub
                z = _dot(ub, wg_ref[...]) + gb_ref[...]
                lane = _iota(z.shape, 1)
                log_sig = jnp.minimum(z, 0.0) - jnp.log1p(jnp.exp(-jnp.abs(z)))
                gate_ref[rs, :] = jnp.where((lane >= H_M) & (lane < 2 * H_M), log_sig, z)
            else:
                ub = u_scr[rs, :]
            acc = _dot(ub, (wm_ref if kind == "plain" else wa_ref)[...])

            def rope(a):
                return a * cc_ref[rs, :] + pltpu.roll(a, HEAD_DIM // 2, 1) * ss_ref[rs, :]

            if kind == "plain":
                main_ref[rs, :] = acc.astype(main_ref.dtype)
            elif kind == "q":
                for sl in head_slices:
                    main_ref[rs, sl] = (rope(acc[:, sl]) * (HEAD_DIM ** -0.5)).astype(main_ref.dtype)
            elif kind == "k":
                for sl in head_slices:
                    y = rope(acc[:, sl])
                    main_ref[rs, sl] = y.astype(main_ref.dtype)
                    kv_ref[rs, sl] = y
            else:
                main_ref[rs, :] = acc.astype(main_ref.dtype)
                kv_ref[rs, :] = acc

    pl.when(j == 0)(functools.partial(step, "plain", True))
    pl.when((j > 0) & (j < q_lo))(functools.partial(step, "plain", False))
    pl.when((j >= q_lo) & (j < k_lo))(functools.partial(step, "q", False))
    pl.when((j >= k_lo) & (j < v_lo))(functools.partial(step, "k", False))
    pl.when(j >= v_lo)(functools.partial(step, "v", False))


def _inproj(x, sc, sh, w_m, w_a, w_gate, gate_bias, rope_cc, rope_ss, layer, tm, main_dtype, tn=1024, sub=256):
    rows, d = x.shape
    n = w_m.shape[2] + w_a.shape[2]
    rm = sc.shape[0]
    mod_block = (1, d) if rm == 1 else (tm, d)
    mod_map = (lambda i, j: (0, 0)) if rm == 1 else (lambda i, j: (i, 0))
    q_lo = (3 * W_HEADS) // tn
    k_lo = (4 * W_HEADS) // tn
    v_lo = (5 * W_HEADS) // tn
    kern = functools.partial(_inproj_kernel, tn=tn, q_lo=q_lo, k_lo=k_lo, v_lo=v_lo, sub=min(sub, tm))
    return pl.pallas_call(
        kern,
        grid=(rows // tm, n // tn),
        in_specs=[
            pl.BlockSpec((tm, d), lambda i, j: (i, 0)),
            pl.BlockSpec(mod_block, mod_map),
            pl.BlockSpec(mod_block, mod_map),
            pl.BlockSpec((None, d, tn), lambda i, j: (layer, 0, jnp.minimum(j, q_lo - 1))),
            pl.BlockSpec((None, d, tn), lambda i, j: (layer, 0, jnp.maximum(j - q_lo, 0))),
            pl.BlockSpec((None, d, N_GATE_PAD), lambda i, j: (layer, 0, 0)),
            pl.BlockSpec((None, 1, N_GATE_PAD), lambda i, j: (layer, 0, 0)),
            pl.BlockSpec((tm, HEAD_DIM), lambda i, j: (i, 0)),
            pl.BlockSpec((tm, HEAD_DIM), lambda i, j: (i, 0)),
        ],
        out_specs=[
            pl.BlockSpec((tm, tn), lambda i, j: (i, j)),
            pl.BlockSpec((tm, tn), lambda i, j: (i, jnp.maximum(j - k_lo, 0))),
            pl.BlockSpec((tm, N_GATE_PAD), lambda i, j: (i, 0)),
        ],
        out_shape=[
            jax.ShapeDtypeStruct((rows, n), main_dtype),
            jax.ShapeDtypeStruct((rows, 2 * W_HEADS), F32),
            jax.ShapeDtypeStruct((rows, N_GATE_PAD), F32),
        ],
        scratch_shapes=[pltpu.VMEM((tm, d), BF16)],
        compiler_params=_cparams("arbitrary", "arbitrary"),
        name="inproj",
    )(x, sc, sh, w_m, w_a, w_gate, gate_bias, rope_cc, rope_ss)


def _mlstm_kernel(q_ref, k_ref, v_ref, g_ref, c0_ref, n0_ref, m0_ref,
                  h_ref, c_ref, n_ref, m_ref, *, L, nseq):
    @pl.when(pl.program_id(1) == 0)
    def _():
        c_ref[...] = c0_ref[...]
        n_ref[...] = n0_ref[...]
        m_ref[...] = m0_ref[...]

    row = _iota((L, L), 0)
    col = _iota((L, L), 1)
    causal = row >= col
    tril = causal.astype(F32)
    triu = (row <= col).astype(F32)
    eye = (_iota((2 * H_M, N_GATE_PAD), 0) == _iota((2 * H_M, N_GATE_PAD), 1)).astype(F32)
    hi_dot = functools.partial(jnp.dot, precision=HIGHEST, preferred_element_type=F32)

    seqs = []
    for bi in range(nseq):
        gates = g_ref[bi * L:(bi + 1) * L, :]
        gates_t = _dot_nt(eye, gates, precision=HIGHEST)
        seqs.append((gates, gates_t, hi_dot(tril, gates), hi_dot(gates_t, triu)))

    units = []
    for bi, (gates, gates_t, bcol_all, brow_all) in enumerate(seqs):
        rows = slice(bi * L, (bi + 1) * L)
        for h in range(H_M):
            b_col = bcol_all[:, H_M + h:H_M + h + 1]
            b_row = brow_all[H_M + h:H_M + h + 1, :]
            m0 = m_ref[bi, :, h:h + 1]
            dmat = jnp.where(causal, b_col - b_row + gates_t[h:h + 1, :], -jnp.inf)
            m_inter = b_col + m0
            m = jnp.maximum(m_inter, jnp.max(dmat, axis=1, keepdims=True))
            q = q_ref[rows, h * DK_M:(h + 1) * DK_M].astype(F32)
            k = k_ref[rows, h * DK_M:(h + 1) * DK_M].astype(F32) * (DK_M ** -0.5)
            qb = q.astype(BF16)
            units.append(dict(
                bi=bi, h=h, rows=rows, m0=m0, m=m, w=jnp.exp(dmat - m), g=jnp.exp(m_inter - m),
                q=q, k=k, qb=qb, vb=v_ref[rows, h * DV_M:(h + 1) * DV_M].astype(BF16),
                s_raw=_dot_nt(qb, k.astype(BF16)), b_col=b_col, ig_col=gates[:, h:h + 1]))

    for u in units:
        bi, h = u["bi"], u["h"]
        s = u["s_raw"] * u["w"]
        num = u["g"] * _dot(u["qb"], c_ref[bi, h].astype(BF16)) + _dot(s.astype(BF16), u["vb"])
        den = (u["g"] * jnp.sum(u["q"] * n_ref[bi, h:h + 1, :], axis=1, keepdims=True)
               + jnp.sum(s, axis=1, keepdims=True))
        h_ref[u["rows"], h * DV_M:(h + 1) * DV_M] = num / jnp.maximum(jnp.abs(den), jnp.exp(-u["m"]))

    for u in units:
        bi, h, b_col, m0 = u["bi"], u["h"], u["b_col"], u["m0"]
        b_last = b_col[L - 1:L, :]
        a_col = b_last - b_col + u["ig_col"]
        m_end = jnp.maximum(b_last + m0, jnp.max(a_col, axis=0, keepdims=True))
        g_end = jnp.exp(b_last + m0 - m_end)
        kw = jnp.exp(a_col - m_end) * u["k"]
        c_ref[bi, h] = g_end * c_ref[bi, h] + _dot_tn(kw.astype(BF16), u["vb"])
        n_ref[bi, h:h + 1, :] = g_end * n_ref[bi, h:h + 1, :] + jnp.sum(kw, axis=0, keepdims=True)
        m_ref[bi, :, h:h + 1] = m_end


def _mlstm_wide_kernel(q_ref, k_ref, v_ref, g_ref, c0_ref, n0_ref, m0_ref,
                       h_ref, c_ref, n_ref, m_ref, nmat_scr):
    L = HEAD_DIM
    first = pl.program_id(1) == 0
    sub8 = _iota((H_M, L), 0)

    @pl.when(first)
    def _():
        c_ref[...] = c0_ref[...]
        n_ref[...] = n0_ref[...]
        m_ref[...] = m0_ref[...]
        for h in range(H_M):
            nmat_scr[h] = lax.dot_general(n0_ref[...], (sub8 == h).astype(F32), (((0,), (0,)), ((), ())),
                                          precision=HIGHEST, preferred_element_type=F32)

    gates = g_ref[...]
    row = _iota((L, L), 0)
    col = _iota((L, L), 1)
    causal = row >= col
    triu = (row <= col).astype(F32)
    eye = (_iota((2 * H_M, N_GATE_PAD), 0) == _iota((2 * H_M, N_GATE_PAD), 1)).astype(F32)
    gates_t = _dot_nt(eye, gates, precision=HIGHEST)
    ig = gates_t[:H_M]
    b = jnp.dot(gates_t[H_M:], triu, precision=HIGHEST, preferred_element_type=F32)
    a = ig - b
    cm = a
    lane = _iota((H_M, L), 1)
    shift = 1
    while shift < L:
        cm = jnp.maximum(cm, jnp.where(lane >= shift, pltpu.roll(cm, shift, 1), -jnp.inf))
        shift *= 2

    eye8 = _iota((H_M, H_M), 0) == _iota((H_M, H_M), 1)
    m0_col = jnp.sum(jnp.where(eye8, jnp.broadcast_to(m_ref[...], (H_M, H_M)), 0.0), axis=1, keepdims=True)
    mm = jnp.maximum(m0_col, cm)
    mm_last = mm[:, L - 1:L]
    ws_all = jnp.exp(a - mm_last)
    g_end_all = jnp.exp(m0_col - mm_last)
    m_end = b[:, L - 1:L] + mm_last
    m_ref[...] = jnp.sum(jnp.where(eye8, jnp.broadcast_to(m_end, (H_M, H_M)), 0.0), axis=0, keepdims=True)
    def split(x, parts):
        out = []
        for _ in range(parts):
            p = x.astype(BF16)
            out.append(p)
            x = x - p.astype(F32)
        return out

    heads = range(H_M)
    ones_b = jnp.ones((L, L), BF16)

    cols = jnp.concatenate([-mm, -mm - b], axis=0).T
    n_sel = 3 * 2 * H_M
    sel_r = _iota((n_sel, H_M * 2 * L), 0) & (2 * H_M - 1)
    sel_c = _iota((n_sel, H_M * 2 * L), 1)
    sel_h = sel_c >> ((2 * L).bit_length() - 1)
    left = (sel_c & (2 * L - 1)) < L
    sel = (left & (sel_r == sel_h)) | (jnp.logical_not(left) & (sel_r == H_M + sel_h))
    bc = _dot(jnp.concatenate(split(cols, 3), axis=1), sel.astype(F32).astype(BF16))

    qb = [q_ref[:, h * DK_M:(h + 1) * DK_M].astype(BF16) for h in heads]
    kt = [(k_ref[:, h * DK_M:(h + 1) * DK_M].astype(F32) * (DK_M ** -0.5)).T for h in heads]
    vo = [jnp.concatenate([v_ref[:, h * DV_M:(h + 1) * DV_M].astype(BF16), ones_b], axis=1) for h in heads]
    s_raw = [_dot(qb[h], kt[h].astype(BF16)) for h in heads]
    inter = [_dot(qb[h], jnp.concatenate([c_ref[h].astype(BF16), nmat_scr[h].astype(BF16)], axis=1))
             for h in heads]

    for h in heads:
        neg_mm = bc[:, 2 * L * h:2 * L * h + L]
        neg_m = bc[:, 2 * L * h + L:2 * L * (h + 1)]
        w = jnp.where(causal, jnp.exp(neg_mm + a[h:h + 1, :]), 0.0)
        gmat = jnp.exp(neg_mm + m0_col[h:h + 1, :])
        s_hi, s_lo = split(s_raw[h] * w, 2)
        r = _dot(s_hi, vo[h])
        num = gmat * inter[h][:, :L] + r[:, :L]
        den = gmat * inter[h][:, L:] + r[:, L:] + _dot(s_lo, ones_b)
        h_ref[:, h * DV_M:(h + 1) * DV_M] = num / jnp.maximum(jnp.abs(den), jnp.exp(neg_m))

    for h in heads:
        g_end = g_end_all[h:h + 1, :]
        kw_hi, kw_lo = split(kt[h] * ws_all[h:h + 1, :], 2)
        u = _dot(kw_hi, vo[h])
        c_ref[h] = g_end * c_ref[h] + u[:, :L]
        nmat_scr[h] = g_end * nmat_scr[h] + u[:, L:] + _dot(kw_lo, ones_b)

    @pl.when(pl.program_id(1) == pl.num_programs(1) - 1)
    def _():
        for h in heads:
            n_ref[h:h + 1, :] = nmat_scr[h].T[0:1, :]


def _mlstm(main, gates, c0, n0, m0, batch, L, layer=None, nseq=1):
    rows = main.shape[0]
    nc = rows // (batch * L)
    if L == HEAD_DIM:
        assert nseq == 1
        kern, scratch, lead = _mlstm_wide_kernel, [pltpu.VMEM((H_M, DK_M, HEAD_DIM), F32)], None
    else:
        assert nc == 1 and batch % nseq == 0
        kern, scratch, lead = functools.partial(_mlstm_kernel, L=L, nseq=nseq), [], nseq
    m0 = m0.reshape(m0.shape[:-1] + (1, H_M))

    def state_spec(tail, stacked):
        zeros = (0,) * len(tail)
        if stacked:
            return pl.BlockSpec((None, lead) + tail, lambda b, c: (layer, b) + zeros)
        return pl.BlockSpec((lead,) + tail, lambda b, c: (b,) + zeros)

    tails = ((H_M, DK_M, DV_M), (H_M, DK_M), (1, H_M))
    blk = nseq * L
    return pl.pallas_call(
        kern,
        grid=(batch // nseq, nc),
        in_specs=[
            pl.BlockSpec((blk, H_M * DK_M), lambda b, c: (b * nc + c, 0)),
            pl.BlockSpec((blk, H_M * DK_M), lambda b, c: (b * nc + c, 1)),
            pl.BlockSpec((blk, H_M * DV_M), lambda b, c: (b * nc + c, 1)),
            pl.BlockSpec((blk, N_GATE_PAD), lambda b, c: (b * nc + c, 0)),
        ] + [state_spec(t, layer is not None) for t in tails],
        out_specs=[pl.BlockSpec((blk, H_M * DV_M), lambda b, c: (b * nc + c, 0))]
        + [state_spec(t, False) for t in tails],
        out_shape=[
            jax.ShapeDtypeStruct((rows, H_M * DV_M), F32),
            jax.ShapeDtypeStruct((batch, H_M, DK_M, DV_M), F32),
            jax.ShapeDtypeStruct((batch, H_M, DK_M), F32),
            jax.ShapeDtypeStruct((batch, 1, H_M), F32),
        ],
        scratch_shapes=scratch,
        compiler_params=_cparams("arbitrary", "arbitrary"),
        name="mlstm",
    )(main, main, main, gates, c0, n0, m0)


def _attn_prompt_kernel(q_ref, kp_ref, kc_ref, vp_ref, vc_ref, o_ref,
                        qq_scr, kk_scr, vv_scr, num_scr, m_scr, den_scr):
    sb = pl.program_id(0)
    SB = ATT_SUPER
    qq_scr[...] = q_ref[...].astype(F32)
    kk_scr[0:SB, :] = kp_ref[...].astype(F32)
    kk_scr[SB:, :] = kc_ref[...].astype(F32)
    vv_scr[0:SB, :] = vp_ref[...].astype(F32)
    vv_scr[SB:, :] = vc_ref[...].astype(F32)

    shape = (ATT_BLOCK, 2 * ATT_BLOCK)
    qi = _iota(shape, 0)
    ki = _iota(shape, 1)
    window = (ki >= qi) & (ki <= qi + ATT_BLOCK)
    window_first = window & ((ki >= ATT_BLOCK) | (sb > 0))
    ones_b = jnp.ones((2 * ATT_BLOCK, HEAD_DIM), BF16)

    for bi, (_, dil) in enumerate(DILATIONS):
        for n in range(SB // (ATT_BLOCK * dil)):
            valid = window_first if n == 0 else window

            for r in range(dil):
                q_rows = pl.ds(r + n * ATT_BLOCK * dil, ATT_BLOCK, stride=dil)
                k_rows = pl.ds(SB + r + (n - 1) * ATT_BLOCK * dil, 2 * ATT_BLOCK, stride=dil)
                q = qq_scr[q_rows, :].astype(BF16)
                k = kk_scr[k_rows, :].astype(BF16)
                v = vv_scr[k_rows, :].astype(BF16)
                s = jnp.where(valid, _dot_nt(q, k), -jnp.inf)
                m = jnp.max(s, axis=1, keepdims=True)
                p = jnp.exp(s - m).astype(BF16)
                r = _dot(p, jnp.concatenate([v, ones_b], axis=1))
                num_scr[bi, q_rows, :] = r[:, :HEAD_DIM]
                den_scr[bi, q_rows, :] = r[:, HEAD_DIM:]
                m_scr[bi, q_rows, :] = jnp.broadcast_to(m, (ATT_BLOCK, HEAD_DIM))

    rows_per_step = 2 * ATT_BLOCK

    def mix(i, carry):
        rows = pl.ds(pl.multiple_of(i * rows_per_step, rows_per_step), rows_per_step)
        ms = [m_scr[b, rows, :] for b in range(len(DILATIONS))]
        big = ms[0]
        for mm in ms[1:]:
            big = jnp.maximum(big, mm)
        w0 = jnp.exp(ms[0] - big)
        acc_num = w0 * num_scr[0, rows, :]
        acc_den = w0 * den_scr[0, rows, :]
        for b in range(1, len(DILATIONS)):
            w = jnp.exp(ms[b] - big)
            acc_num = acc_num + w * num_scr[b, rows, :]
            acc_den = acc_den + w * den_scr[b, rows, :]
        o_ref[rows, :] = acc_num / acc_den
        return carry

    lax.fori_loop(0, SB // rows_per_step, mix, 0)


def _attn_prompt(main):
    s, n_main = main.shape
    SB = ATT_SUPER
    cols = n_main // HEAD_DIM // 6
    blk = (SB, HEAD_DIM)
    prev = lambda i: jnp.maximum(i - 1, 0)
    nbr = len(DILATIONS)
    return pl.pallas_call(
        _attn_prompt_kernel,
        grid=(s // SB, H_A),
        in_specs=[
            pl.BlockSpec(blk, lambda i, h: (i, 3 * cols + h)),
            pl.BlockSpec(blk, lambda i, h: (prev(i), 4 * cols + h)),
            pl.BlockSpec(blk, lambda i, h: (i, 4 * cols + h)),
            pl.BlockSpec(blk, lambda i, h: (prev(i), 5 * cols + h)),
            pl.BlockSpec(blk, lambda i, h: (i, 5 * cols + h)),
        ],
        out_specs=pl.BlockSpec(blk, lambda i, h: (i, h)),
        out_shape=jax.ShapeDtypeStruct((s, W_HEADS), F32),
        scratch_shapes=[
            pltpu.VMEM((SB, HEAD_DIM), F32),
            pltpu.VMEM((2 * SB, HEAD_DIM), F32),
            pltpu.VMEM((2 * SB, HEAD_DIM), F32),
            pltpu.VMEM((nbr, SB, HEAD_DIM), F32),
            pltpu.VMEM((nbr, SB, HEAD_DIM), F32),
            pltpu.VMEM((nbr, SB, HEAD_DIM), F32),
        ],
        compiler_params=_cparams("arbitrary", "arbitrary"),
        name="attn_prompt",
    )(main, main, main, main, main)


def _branch_count(delta):
    cnt = jnp.zeros(delta.shape, F32)
    for window, dil in DILATIONS:
        hit = (delta >= 0) & (delta <= window) & ((delta & (dil - 1)) == 0)
        cnt = cnt + hit.astype(F32)
    return cnt


def _attn_sample_kernel(q_ref, kn_ref, vn_ref, kfar_ref, knear_ref, vfar_ref, vnear_ref, o_ref,
                        cnt_scr, s_scr, *, T, NB, chunk):
    R = H_A * T
    t_bits = T.bit_length() - 1
    h_bits = H_A.bit_length() - 1
    d_far = DILATIONS[-1][1]
    n_groups, keep = kfar_ref.shape[0], kfar_ref.shape[1]
    keep_bits = keep.bit_length() - 1
    near_pos = knear_ref.shape[0] // H_A
    gpc = chunk // (keep * H_A)
    far_chunks = n_groups // gpc
    n_chunks = far_chunks + near_pos * H_A // chunk
    chunks = [slice(c * chunk, (c + 1) * chunk) for c in range(n_chunks)]

    def rows(far_ref, near_ref, c):
        if c < far_chunks:
            return far_ref[c * gpc:(c + 1) * gpc].reshape(chunk, HEAD_DIM)
        return near_ref[chunks[c - far_chunks], :]

    @pl.when(pl.program_id(0) == 0)
    def _():
        for c, sl in enumerate(chunks):
            row = _iota((R, chunk), 0)
            col = _iota((R, chunk), 1)
            if c < far_chunks:
                pos = (c * gpc + (col >> (h_bits + keep_bits))) * d_far + ((col >> h_bits) & (keep - 1))
            else:
                pos = NB - near_pos + (c - far_chunks) * (chunk // H_A) + (col >> h_bits)
            same_head = (row >> t_bits) == (col & (H_A - 1))
            cnt_scr[:, sl] = jnp.where(same_head, _branch_count(NB + (row & (T - 1)) - pos), 0.0)

    heads = [slice(h * HEAD_DIM, (h + 1) * HEAD_DIM) for h in range(H_A)]
    q = jnp.concatenate([q_ref[:, sl] for sl in heads], axis=0).astype(BF16)
    kn = jnp.concatenate([kn_ref[:, sl] for sl in heads], axis=0).astype(BF16)
    vn = jnp.concatenate([vn_ref[:, sl] for sl in heads], axis=0).astype(BF16)
    row = _iota((R, R), 0)
    col = _iota((R, R), 1)
    cnt_n = jnp.where((row >> t_bits) == (col >> t_bits),
                      _branch_count((row & (T - 1)) - (col & (T - 1))), 0.0)

    s_n = jnp.where(cnt_n > 0, _dot_nt(q, kn), -jnp.inf)
    big = jnp.max(s_n, axis=1, keepdims=True)
    for c, sl in enumerate(chunks):
        s = jnp.where(cnt_scr[:, sl] > 0, _dot_nt(q, rows(kfar_ref, knear_ref, c).astype(BF16)), -jnp.inf)
        s_scr[:, sl] = s
        big = jnp.maximum(big, jnp.max(s, axis=1, keepdims=True))

    p_n = cnt_n * jnp.exp(s_n - big)
    den = jnp.sum(p_n, axis=1, keepdims=True)
    num = _dot(p_n.astype(BF16), vn)
    for c, sl in enumerate(chunks):
        p = cnt_scr[:, sl] * jnp.exp(s_scr[:, sl] - big)
        den = den + jnp.sum(p, axis=1, keepdims=True)
        num = num + _dot(p.astype(BF16), rows(vfar_ref, vnear_ref, c).astype(BF16))
    out = num / den
    for h, sl in enumerate(heads):
        o_ref[:, sl] = out[h * T:(h + 1) * T, :]


def _attn_sample(main, cache_k, cache_v, layer, batch, T, chunk=2048):
    depth, _, nb, n_heads, hd = cache_k.shape
    assert n_heads == H_A and hd == HEAD_DIM and T & (T - 1) == 0 and H_A & (H_A - 1) == 0
    d_far = DILATIONS[-1][1]
    near_pos = DILATIONS[-2][0]
    assert all(w <= near_pos for w, _ in DILATIONS[:-1]) and nb % d_far == 0 and T <= d_far
    assert (nb - near_pos) % d_far == 0 and near_pos <= nb
    n_groups = (nb - near_pos) // d_far
    nf = nb * H_A
    n_keys = (n_groups * T + near_pos) * H_A
    assert (n_groups * T * H_A) % chunk == 0 and (near_pos * H_A) % chunk == 0 and nf % (near_pos * H_A) == 0
    far_view = lambda c: c.reshape(depth, batch, nb // d_far, d_far, H_A, HEAD_DIM)
    near_view = lambda c: c.reshape(depth, batch, nf, HEAD_DIM)
    kern = functools.partial(_attn_sample_kernel, T=T, NB=nb, chunk=chunk)
    far_spec = pl.BlockSpec((None, None, n_groups, T, H_A, HEAD_DIM), lambda b: (layer, b, 0, 0, 0, 0))
    near_spec = pl.BlockSpec((None, None, near_pos * H_A, HEAD_DIM),
                             lambda b: (layer, b, nf // (near_pos * H_A) - 1, 0))
    return pl.pallas_call(
        kern,
        grid=(batch,),
        in_specs=[
            pl.BlockSpec((T, W_HEADS), lambda b: (b, 3)),
            pl.BlockSpec((T, W_HEADS), lambda b: (b, 4)),
            pl.BlockSpec((T, W_HEADS), lambda b: (b, 5)),
            far_spec,
            near_spec,
            far_spec,
            near_spec,
        ],
        out_specs=pl.BlockSpec((T, W_HEADS), lambda b: (b, 0)),
        out_shape=jax.ShapeDtypeStruct((batch * T, W_HEADS), F32),
        scratch_shapes=[pltpu.VMEM((H_A * T, n_keys), F32), pltpu.VMEM((H_A * T, n_keys), F32)],
        compiler_params=_cparams("arbitrary"),
        name="attn_sample",
    )(main, main, main, far_view(cache_k), near_view(cache_k), far_view(cache_v), near_view(cache_v))


def _layer_norm(y, g, b):
    mu = jnp.mean(y, axis=1, keepdims=True)
    yc = y - mu
    var = jnp.mean(yc * yc, axis=1, keepdims=True)
    return yc * lax.rsqrt(var + LN_EPS) * g + b


def _merge_kernel(h_ref, om_ref, att_ref, x_ref, gt_ref, gh_ref, w_ref, lg_ref, lb_ref,
                  o_ref, cat_scr, *, alpha, sub):
    for r0 in range(0, cat_scr.shape[0], sub):
        rs = slice(r0, r0 + sub)
        for h in range(H_M):
            sl = slice(h * DV_M, (h + 1) * DV_M)
            hh = h_ref[rs, sl]
            hn = hh * lax.rsqrt(jnp.mean(hh * hh, axis=1, keepdims=True) + HEAD_NORM_EPS) * gh_ref[:, sl]
            cat_scr[rs, sl] = (hn * jax.nn.sigmoid(om_ref[rs, sl].astype(F32))).astype(BF16)
        cat_scr[rs, H_M * DV_M:] = att_ref[rs, :].astype(BF16)
        mix = _dot(cat_scr[rs, :], w_ref[...])
        gt = gt_ref[...] if gt_ref.shape[0] == 1 else gt_ref[rs, :]
        y = alpha * x_ref[rs, :] + (1.0 + gt) * mix
        o_ref[rs, :] = _layer_norm(y, lg_ref[...], lb_ref[...])


def _merge(h, main, att, x, gt, g_head, w_out, ln_g, ln_b, layer, alpha, tm, sub=256):
    rows, d = x.shape
    rm = gt.shape[0]
    mod_block = (1, d) if rm == 1 else (tm, d)
    mod_map = (lambda i: (0, 0)) if rm == 1 else (lambda i: (i, 0))
    wide = pl.BlockSpec((tm, W_HEADS), lambda i: (i, 0))
    const = lambda shape: pl.BlockSpec(shape, lambda i: (0, 0))
    return pl.pallas_call(
        functools.partial(_merge_kernel, alpha=alpha, sub=min(sub, tm)),
        grid=(rows // tm,),
        in_specs=[
            wide,
            pl.BlockSpec((tm, W_HEADS), lambda i: (i, 2)),
            wide,
            pl.BlockSpec((tm, d), lambda i: (i, 0)),
            pl.BlockSpec(mod_block, mod_map),
            const((1, W_HEADS)),
            pl.BlockSpec((None, d, d), lambda i: (layer, 0, 0)),
            const((1, d)),
            const((1, d)),
        ],
        out_specs=pl.BlockSpec((tm, d), lambda i: (i, 0)),
        out_shape=jax.ShapeDtypeStruct((rows, d), F32),
        scratch_shapes=[pltpu.VMEM((tm, d), BF16)],
        compiler_params=_cparams("arbitrary"),
        name="mixer_merge",
    )(h, main, att, x, gt, g_head, w_out, ln_g, ln_b)


def _ffn_kernel(*refs, alpha, seq_len, n_chunks, carry, sub):
    if carry:
        (x_ref, sc_ref, sh_ref, gt_ref, wg_ref, wv_ref, wd_ref, cw_ref, cb_ref, lg_ref, lb_ref,
         o_ref, tail_ref, u_scr, carry_scr) = refs
    else:
        (x_ref, sc_ref, sh_ref, gt_ref, wg_ref, wv_ref, wd_ref, cw_ref, cb_ref, lg_ref, lb_ref,
         h0_ref, h1_ref, o_ref, tail_ref, u_scr) = refs
    i = pl.program_id(0)
    j = pl.program_id(1)
    tm = u_scr.shape[0]
    assert n_chunks >= 2

    def rows_of(ref, rs):
        return ref[...] if ref.shape[0] == 1 else ref[rs, :]

    if carry:
        @pl.when(i == 0)
        def _():
            carry_scr[j] = jnp.zeros(carry_scr.shape[1:], F32)

    def step(first, last):
        cw = cw_ref[...]
        if carry:
            tail = carry_scr[j]
        for r0 in range(0, tm, sub):
            rs = slice(r0, r0 + sub)
            if first:
                ub = (x_ref[rs, :] * (1.0 + rows_of(sc_ref, rs)) + rows_of(sh_ref, rs)).astype(BF16)
                u_scr[rs, :] = ub
            else:
                ub = u_scr[rs, :]
            g = _dot(ub, wg_ref[...])
            v = _dot(ub, wv_ref[...])
            row = _iota(g.shape, 0)
            if carry:
                hist0, hist1 = tail[6:7, :], tail[7:8, :]
                pos = row
                tail = g[sub - 8:, :]
            else:
                hist0, hist1 = h0_ref[rs, :], h1_ref[rs, :]
                pos = row & (seq_len - 1)
                tail_ref[rs, :] = g
            prev1 = jnp.where(pos == 0, hist1, pltpu.roll(g, 1, 0))
            prev2 = jnp.where(pos == 0, hist0, jnp.where(pos == 1, hist1, pltpu.roll(g, 2, 0)))
            a = prev2 * cw[0:1, :] + prev1 * cw[1:2, :] + g * cw[2:3, :] + cb_ref[...]
            a = a * jax.nn.sigmoid(a) * v
            acc = _dot(a.astype(BF16), wd_ref[...])
            if not first:
                acc = o_ref[rs, :] + acc
            if last:
                y = alpha * x_ref[rs, :] + (1.0 + rows_of(gt_ref, rs)) * acc
                acc = _layer_norm(y, lg_ref[...], lb_ref[...])
            o_ref[rs, :] = acc
        if carry:
            carry_scr[j] = tail
            tail_ref[...] = tail

    pl.when(j == 0)(functools.partial(step, True, False))
    pl.when((j > 0) & (j < n_chunks - 1))(functools.partial(step, False, False))
    pl.when(j == n_chunks - 1)(functools.partial(step, False, True))


def _ffn(x, sc, sh, gt, w_up, w_down, conv_w, conv_b, ln_g, ln_b, layer, alpha, tm,
         hist=None, seq_len=None, tf=512, sub=256):
    rows, d = x.shape
    d_ff = w_down.shape[1]
    nj = d_ff // tf
    rm = sc.shape[0]
    mod_block = (1, d) if rm == 1 else (tm, d)
    mod_map = (lambda i, j: (0, 0)) if rm == 1 else (lambda i, j: (i, 0))
    carry = hist is None
    in_specs = [
        pl.BlockSpec((tm, d), lambda i, j: (i, 0), pipeline_mode=pl.Buffered(1)),
        pl.BlockSpec(mod_block, mod_map),
        pl.BlockSpec(mod_block, mod_map),
        pl.BlockSpec(mod_block, mod_map),
        pl.BlockSpec((None, d, tf), lambda i, j: (layer, 0, j)),
        pl.BlockSpec((None, d, tf), lambda i, j: (layer, 0, nj + j)),
        pl.BlockSpec((None, tf, d), lambda i, j: (layer, j, 0)),
        pl.BlockSpec((CONV_W, tf), lambda i, j: (0, j)),
        pl.BlockSpec((1, tf), lambda i, j: (0, j)),
        pl.BlockSpec((1, d), lambda i, j: (0, 0)),
        pl.BlockSpec((1, d), lambda i, j: (0, 0)),
    ]
    args = [x, sc, sh, gt, w_up, w_up, w_down, conv_w, conv_b, ln_g, ln_b]
    scratch = [pltpu.VMEM((tm, d), BF16)]
    if carry:
        tail_rows = 8
        scratch.append(pltpu.VMEM((nj, 8, tf), F32))
    else:
        tail_rows = tm
        in_specs += [pl.BlockSpec((tm, tf), lambda i, j: (i, j))] * 2
        args += list(hist)
    kern = functools.partial(_ffn_kernel, alpha=alpha, seq_len=seq_len, n_chunks=nj, carry=carry,
                             sub=min(sub, tm))
    return pl.pallas_call(
        kern,
        grid=(rows // tm, nj),
        in_specs=in_specs,
        out_specs=[
            pl.BlockSpec((tm, d), lambda i, j: (i, 0)),
            pl.BlockSpec((tail_rows, tf), lambda i, j: (i, j)),
        ],
        out_shape=[
            jax.ShapeDtypeStruct((rows, d), F32),
            jax.ShapeDtypeStruct((rows // tm * tail_rows, d_ff), F32),
        ],
        scratch_shapes=scratch,
        compiler_params=_cparams("arbitrary", "arbitrary"),
        name="conv_ffn",
    )(*args)


def _rope_tables(pos):
    half = HEAD_DIM // 2
    inv = ROPE_THETA ** (-jnp.arange(half, dtype=F32) / half)
    ang = pos.astype(F32)[:, None] * inv[None, :]
    cos, sin = jnp.cos(ang), jnp.sin(ang)
    return jnp.concatenate([cos, cos], -1), jnp.concatenate([-sin, sin], -1)


def kernel(x_prompt, x_sample, cache_k_win, cache_v_win, state_C, state_n, state_m, state_conv,
           c_prompt, c_sample, w_ada, b_ada, w_in, b_gate, g_head, w_out, ln1_g, ln1_b,
           w_up, conv_w, conv_b, w_down, ln2_g, ln2_b):
    bp, s, d = x_prompt.shape
    bs, t, _ = x_sample.shape
    depth = w_in.shape[0]
    d_ff = w_down.shape[1]
    alpha = (2 * depth) ** 0.25
    assert bp == 1 and d == (H_M + H_A) * HEAD_DIM and s % ATT_SUPER == 0

    n_c = bp + bs
    pad = (-n_c) % 8
    c_all = jnp.concatenate([c_prompt, c_sample, jnp.zeros((pad, d), F32)], 0)
    mod = _modulation(c_all, w_ada, b_ada)

    n_m = 2 * H_M * DK_M + 2 * H_M * DV_M
    w_m = w_in[:, :, :n_m].astype(BF16)
    w_a = w_in[:, :, n_m + 2 * H_M:].astype(BF16)
    w_gate = jnp.pad(w_in[:, :, n_m:n_m + 2 * H_M], ((0, 0), (0, 0), (0, N_GATE_PAD - 2 * H_M))).astype(BF16)
    gate_bias = jnp.pad(b_gate.reshape(depth, 1, 2 * H_M), ((0, 0), (0, 0), (0, N_GATE_PAD - 2 * H_M)))
    w_out_b = w_out.astype(BF16)
    w_up_b = w_up.astype(BF16)
    w_down_b = w_down.astype(BF16)

    cc_p, ss_p = _rope_tables(jnp.arange(s, dtype=jnp.int32))
    pos_s = PAST_LEN + jnp.arange(t, dtype=jnp.int32)
    cc_s, ss_s = (jnp.tile(a, (bs, 1)) for a in _rope_tables(pos_s))

    xp = x_prompt.reshape(s, d)
    xs = x_sample.reshape(bs * t, d)
    rows_s = bs * t
    tm_p = 1024
    outs =[[] for _ in range(12)]
    zeros_c = jnp.zeros((bp, H_M, DK_M, DV_M), F32)
    zeros_n = jnp.zeros((bp, H_M, DK_M), F32)
    zeros_m = jnp.zeros((bp, H_M), F32)
    row2 = lambda v: v.reshape(1, -1)

    for l in range(depth):
        mod_p = [mod[l, 0:1, k * d:(k + 1) * d] for k in range(6)]
        mod_s = [jnp.repeat(mod[l, bp:bp + bs, k * d:(k + 1) * d], t, axis=0) for k in range(6)]
        ln1 = (row2(ln1_g[l]), row2(ln1_b[l]))
        ffn_small = (conv_w[l], row2(conv_b[l]), row2(ln2_g[l]), row2(ln2_b[l]))

        main_p, kv_p, gates_p = _inproj(xp, mod_p[1], mod_p[0], w_m, w_a, w_gate, gate_bias, cc_p, ss_p, l,
                                        tm=tm_p, main_dtype=BF16)
        h_p, c_p, n_p, m_p = _mlstm(main_p, gates_p, zeros_c, zeros_n, zeros_m, batch=bp, L=128)
        att_p = _attn_prompt(main_p)
        x1_p = _merge(h_p, main_p, att_p, xp, mod_p[2], row2(g_head[l]), w_out_b, *ln1, l, alpha, tm=512)
        xp, tail_p = _ffn(x1_p, mod_p[4], mod_p[3], mod_p[5], w_up_b, w_down_b, *ffn_small, l, alpha, tm=tm_p)

        main_s, kv_s, gates_s = _inproj(xs, mod_s[1], mod_s[0], w_m, w_a, w_gate, gate_bias, cc_s, ss_s, l,
                                        tm=rows_s, main_dtype=F32)
        h_s, c_s, n_s, m_s = _mlstm(main_s, gates_s, state_C, state_n, state_m, batch=bs, L=t, layer=l,
                                    nseq=MLSTM_SEQS_PER_STEP)
        att_s = _attn_sample(main_s, cache_k_win, cache_v_win, l, bs, t)
        x1_s = _merge(h_s, main_s, att_s, xs, mod_s[2], row2(g_head[l]), w_out_b, *ln1, l, alpha, tm=rows_s)
        hist = [jnp.repeat(state_conv[l][:, r, :], t, axis=0) for r in range(CONV_W - 1)]
        xs, g_s = _ffn(x1_s, mod_s[4], mod_s[3], mod_s[5], w_up_b, w_down_b, *ffn_small, l, alpha, tm=rows_s,
                       hist=hist, seq_len=t)

        wp = min(DILATIONS[-1][0], s)
        k_cols = slice(0, W_HEADS)
        v_cols = slice(W_HEADS, 2 * W_HEADS)
        outs[0].append(kv_p[s - wp:, k_cols].reshape(bp, wp, H_A, HEAD_DIM))
        outs[1].append(kv_p[s - wp:, v_cols].reshape(bp, wp, H_A, HEAD_DIM))
        outs[2].append(c_p)
        outs[3].append(n_p)
        outs[4].append(m_p.reshape(bp, H_M))
        outs[5].append(tail_p[-8:][8 - (CONV_W - 1):].reshape(bp, CONV_W - 1, d_ff))
        outs[6].append(kv_s[:, k_cols].reshape(bs, t, H_A, HEAD_DIM))
        outs[7].append(kv_s[:, v_cols].reshape(bs, t, H_A, HEAD_DIM))
        outs[8].append(c_s)
        outs[9].append(n_s)
        outs[10].append(m_s.reshape(bs, H_M))
        outs[11].append(g_s.reshape(bs, t, d_ff)[:, t - (CONV_W - 1):])

    return (xp.reshape(bp, s, d), xs.reshape(bs, t, d)) + tuple(jnp.stack(o) for o in outs)
```

```python
import functools

import jax
import jax.numpy as jnp
from jax import lax
from jax.experimental import pallas as pl
from jax.experimental.pallas import tpu as pltpu

F32 = jnp.float32
BF16 = jnp.bfloat16
HIGHEST = lax.Precision.HIGHEST

HEAD_DIM = 128
H_M = 8
H_A = 8
DK_M = 64
DV_M = 128
DILATIONS = ((128, 1), (512, 4), (2048, 16))
ATT_BLOCK = 128
PAST_LEN = 8192
ROPE_THETA = 10000.0
CONV_W = 3
LN_EPS = 1e-5
HEAD_NORM_EPS = 1e-6
N_GATE_PAD = 128
W_HEADS = H_A * HEAD_DIM
ATT_SUPER = ATT_BLOCK * DILATIONS[-1][1]
MLSTM_SEQS_PER_STEP = 4
VMEM_LIMIT_BYTES = 56 * 1024 * 1024


def _cparams(*sem):
    return pltpu.CompilerParams(dimension_semantics=sem, vmem_limit_bytes=VMEM_LIMIT_BYTES)


def _dot(a, b):
    return jnp.dot(a, b, preferred_element_type=F32)


def _dot_nt(a, b, precision=None):
    return lax.dot_general(a, b, (((1,), (1,)), ((), ())), precision=precision,
                           preferred_element_type=F32)


def _dot_tn(a, b):
    return lax.dot_general(a, b, (((0,), (0,)), ((), ())), preferred_element_type=F32)


def _iota(shape, dim):
    return lax.broadcasted_iota(jnp.int32, shape, dim)


def _mod_kernel(c_ref, w_ref, b_ref, o_ref):
    c = c_ref[...]
    a = (c * jax.nn.sigmoid(c)).astype(BF16)
    o_ref[...] = _dot(a, w_ref[...].astype(BF16)) + b_ref[...]


def _modulation(c_all, w_ada, b_ada, tn=1024):
    depth, d, n = w_ada.shape
    rows = c_all.shape[0]
    return pl.pallas_call(
        _mod_kernel,
        grid=(depth, n // tn),
        in_specs=[
            pl.BlockSpec((rows, d), lambda l, j: (0, 0)),
            pl.BlockSpec((None, d, tn), lambda l, j: (l, 0, j)),
            pl.BlockSpec((None, 1, tn), lambda l, j: (l, 0, j)),
        ],
        out_specs=pl.BlockSpec((None, rows, tn), lambda l, j: (l, 0, j)),
        out_shape=jax.ShapeDtypeStruct((depth, rows, n), F32),
        compiler_params=_cparams("arbitrary", "arbitrary"),
        name="adaln_mod",
    )(c_all, w_ada, b_ada.reshape(depth, 1, n))


def _inproj_kernel(x_ref, sc_ref, sh_ref, wm_ref, wa_ref, wg_ref, gb_ref, cc_ref, ss_ref,
                   main_ref, kv_ref, gate_ref, u_scr, *, tn, q_lo, k_lo, v_lo, sub):
    j = pl.program_id(1)
    tm = u_scr.shape[0]
    head_slices = [slice(g * HEAD_DIM, (g + 1) * HEAD_DIM) for g in range(tn // HEAD_DIM)]

    def rows_of(ref, rs):
        return ref[...] if ref.shape[0] == 1 else ref[rs, :]

    def step(kind, first):
        for r0 in range(0, tm, sub):
            rs = slice(r0, r0 + sub)
            if first:
                ub = (x_ref[rs, :] * (1.0 + rows_of(sc_ref, rs)) + rows_of(sh_ref, rs)).astype(BF16)
                u_scr[rs, :] = ub
                z = _dot(ub, wg_ref[...]) + gb_ref[...]
                lane = _iota(z.shape, 1)
                log_sig = jnp.minimum(z, 0.0) - jnp.log1p(jnp.exp(-jnp.abs(z)))
                gate_ref[rs, :] = jnp.where((lane >= H_M) & (lane < 2 * H_M), log_sig, z)
            else:
                ub = u_scr[rs, :]
            acc = _dot(ub, (wm_ref if kind == "plain" else wa_ref)[...])

            def rope(a):
                return a * cc_ref[rs, :] + pltpu.roll(a, HEAD_DIM // 2, 1) * ss_ref[rs, :]

            if kind == "plain":
                main_ref[rs, :] = acc.astype(main_ref.dtype)
            elif kind == "q":
                for sl in head_slices:
                    main_ref[rs, sl] = (rope(acc[:, sl]) * (HEAD_DIM ** -0.5)).astype(main_ref.dtype)
            elif kind == "k":
                for sl in head_slices:
                    y = rope(acc[:, sl])
                    main_ref[rs, sl] = y.astype(main_ref.dtype)
                    kv_ref[rs, sl] = y
            else:
                main_ref[rs, :] = acc.astype(main_ref.dtype)
                kv_ref[rs, :] = acc

    pl.when(j == 0)(functools.partial(step, "plain", True))
    pl.when((j > 0) & (j < q_lo))(functools.partial(step, "plain", False))
    pl.when((j >= q_lo) & (j < k_lo))(functools.partial(step, "q", False))
    pl.when((j >= k_lo) & (j < v_lo))(functools.partial(step, "k", False))
    pl.when(j >= v_lo)(functools.partial(step, "v", False))


def _inproj(x, sc, sh, w_m, w_a, w_gate, gate_bias, rope_cc, rope_ss, layer, tm, main_dtype, tn=1024, sub=512):
    rows, d = x.shape
    n = w_m.shape[2] + w_a.shape[2]
    rm = sc.shape[0]
    mod_block = (1, d) if rm == 1 else (tm, d)
    mod_map = (lambda i, j: (0, 0)) if rm == 1 else (lambda i, j: (i, 0))
    q_lo = (3 * W_HEADS) // tn
    k_lo = (4 * W_HEADS) // tn
    v_lo = (5 * W_HEADS) // tn
    kern = functools.partial(_inproj_kernel, tn=tn, q_lo=q_lo, k_lo=k_lo, v_lo=v_lo, sub=min(sub, tm))
    return pl.pallas_call(
        kern,
        grid=(rows // tm, n // tn),
        in_specs=[
            pl.BlockSpec((tm, d), lambda i, j: (i, 0)),
            pl.BlockSpec(mod_block, mod_map),
            pl.BlockSpec(mod_block, mod_map),
            pl.BlockSpec((None, d, tn), lambda i, j: (layer, 0, jnp.minimum(j, q_lo - 1))),
            pl.BlockSpec((None, d, tn), lambda i, j: (layer, 0, jnp.maximum(j - q_lo, 0))),
            pl.BlockSpec((None, d, N_GATE_PAD), lambda i, j: (layer, 0, 0)),
            pl.BlockSpec((None, 1, N_GATE_PAD), lambda i, j: (layer, 0, 0)),
            pl.BlockSpec((tm, HEAD_DIM), lambda i, j: (i, 0)),
            pl.BlockSpec((tm, HEAD_DIM), lambda i, j: (i, 0)),
        ],
        out_specs=[
            pl.BlockSpec((tm, tn), lambda i, j: (i, j)),
            pl.BlockSpec((tm, tn), lambda i, j: (i, jnp.maximum(j - k_lo, 0))),
            pl.BlockSpec((tm, N_GATE_PAD), lambda i, j: (i, 0)),
        ],
        out_shape=[
            jax.ShapeDtypeStruct((rows, n), main_dtype),
            jax.ShapeDtypeStruct((rows, 2 * W_HEADS), F32),
            jax.ShapeDtypeStruct((rows, N_GATE_PAD), F32),
        ],
        scratch_shapes=[pltpu.VMEM((tm, d), BF16)],
        compiler_params=_cparams("arbitrary", "arbitrary"),
        name="inproj",
    )(x, sc, sh, w_m, w_a, w_gate, gate_bias, rope_cc, rope_ss)


def _mlstm_kernel(q_ref, k_ref, v_ref, g_ref, c0_ref, n0_ref, m0_ref,
                  h_ref, c_ref, n_ref, m_ref, *, L, nseq):
    @pl.when(pl.program_id(1) == 0)
    def _():
        c_ref[...] = c0_ref[...]
        n_ref[...] = n0_ref[...]
        m_ref[...] = m0_ref[...]

    row = _iota((L, L), 0)
    col = _iota((L, L), 1)
    causal = row >= col
    tril = causal.astype(F32)
    triu = (row <= col).astype(F32)
    eye = (_iota((2 * H_M, N_GATE_PAD), 0) == _iota((2 * H_M, N_GATE_PAD), 1)).astype(F32)
    hi_dot = functools.partial(jnp.dot, precision=HIGHEST, preferred_element_type=F32)

    seqs = []
    for bi in range(nseq):
        gates = g_ref[bi * L:(bi + 1) * L, :]
        gates_t = _dot_nt(eye, gates, precision=HIGHEST)
        seqs.append((gates, gates_t, hi_dot(tril, gates), hi_dot(gates_t, triu)))

    units = []
    for bi, (gates, gates_t, bcol_all, brow_all) in enumerate(seqs):
        rows = slice(bi * L, (bi + 1) * L)
        for h in range(H_M):
            b_col = bcol_all[:, H_M + h:H_M + h + 1]
            b_row = brow_all[H_M + h:H_M + h + 1, :]
            m0 = m_ref[bi, :, h:h + 1]
            dmat = jnp.where(causal, b_col - b_row + gates_t[h:h + 1, :], -jnp.inf)
            m_inter = b_col + m0
            m = jnp.maximum(m_inter, jnp.max(dmat, axis=1, keepdims=True))
            q = q_ref[rows, h * DK_M:(h + 1) * DK_M].astype(F32)
            k = k_ref[rows, h * DK_M:(h + 1) * DK_M].astype(F32) * (DK_M ** -0.5)
            qb = q.astype(BF16)
            units.append(dict(
                bi=bi, h=h, rows=rows, m0=m0, m=m, w=jnp.exp(dmat - m), g=jnp.exp(m_inter - m),
                q=q, k=k, qb=qb, vb=v_ref[rows, h * DV_M:(h + 1) * DV_M].astype(BF16),
                s_raw=_dot_nt(qb, k.astype(BF16)), b_col=b_col, ig_col=gates[:, h:h + 1]))

    for u in units:
        bi, h = u["bi"], u["h"]
        s = u["s_raw"] * u["w"]
        num = u["g"] * _dot(u["qb"], c_ref[bi, h].astype(BF16)) + _dot(s.astype(BF16), u["vb"])
        den = (u["g"] * jnp.sum(u["q"] * n_ref[bi, h:h + 1, :], axis=1, keepdims=True)
               + jnp.sum(s, axis=1, keepdims=True))
        h_ref[u["rows"], h * DV_M:(h + 1) * DV_M] = num / jnp.maximum(jnp.abs(den), jnp.exp(-u["m"]))

    for u in units:
        bi, h, b_col, m0 = u["bi"], u["h"], u["b_col"], u["m0"]
        b_last = b_col[L - 1:L, :]
        a_col = b_last - b_col + u["ig_col"]
        m_end = jnp.maximum(b_last + m0, jnp.max(a_col, axis=0, keepdims=True))
        g_end = jnp.exp(b_last + m0 - m_end)
        kw = jnp.exp(a_col - m_end) * u["k"]
        c_ref[bi, h] = g_end * c_ref[bi, h] + _dot_tn(kw.astype(BF16), u["vb"])
        n_ref[bi, h:h + 1, :] = g_end * n_ref[bi, h:h + 1, :] + jnp.sum(kw, axis=0, keepdims=True)
        m_ref[bi, :, h:h + 1] = m_end


def _mlstm_wide_kernel(q_ref, k_ref, v_ref, g_ref, c0_ref, n0_ref, m0_ref,
                       h_ref, c_ref, n_ref, m_ref, nmat_scr):
    L = HEAD_DIM
    first = pl.program_id(1) == 0
    sub8 = _iota((H_M, L), 0)

    @pl.when(first)
    def _():
        c_ref[...] = c0_ref[...]
        n_ref[...] = n0_ref[...]
        m_ref[...] = m0_ref[...]
        for h in range(H_M):
            nmat_scr[h] = lax.dot_general(n0_ref[...], (sub8 == h).astype(F32), (((0,), (0,)), ((), ())),
                                          precision=HIGHEST, preferred_element_type=F32)

    gates = g_ref[...]
    row = _iota((L, L), 0)
    col = _iota((L, L), 1)
    causal = row >= col
    triu = (row <= col).astype(F32)
    eye = (_iota((2 * H_M, N_GATE_PAD), 0) == _iota((2 * H_M, N_GATE_PAD), 1)).astype(F32)
    gates_t = _dot_nt(eye, gates, precision=HIGHEST)
    ig = gates_t[:H_M]
    b = jnp.dot(gates_t[H_M:], triu, precision=HIGHEST, preferred_element_type=F32)
    a = ig - b
    cm = a
    lane = _iota((H_M, L), 1)
    shift = 1
    while shift < L:
        cm = jnp.maximum(cm, jnp.where(lane >= shift, pltpu.roll(cm, shift, 1), -jnp.inf))
        shift *= 2

    eye8 = _iota((H_M, H_M), 0) == _iota((H_M, H_M), 1)
    m0_col = jnp.sum(jnp.where(eye8, jnp.broadcast_to(m_ref[...], (H_M, H_M)), 0.0), axis=1, keepdims=True)
    mm = jnp.maximum(m0_col, cm)
    mm_last = mm[:, L - 1:L]
    ws_all = jnp.exp(a - mm_last)
    g_end_all = jnp.exp(m0_col - mm_last)
    m_end = b[:, L - 1:L] + mm_last
    m_ref[...] = jnp.sum(jnp.where(eye8, jnp.broadcast_to(m_end, (H_M, H_M)), 0.0), axis=0, keepdims=True)
    def split(x, parts):
        out = []
        for _ in range(parts):
            p = x.astype(BF16)
            out.append(p)
            x = x - p.astype(F32)
        return out

    heads = range(H_M)
    ones_b = jnp.ones((L, L), BF16)

    cols = jnp.concatenate([-mm, -mm - b], axis=0).T
    n_sel = 3 * 2 * H_M
    sel_r = _iota((n_sel, H_M * 2 * L), 0) & (2 * H_M - 1)
    sel_c = _iota((n_sel, H_M * 2 * L), 1)
    sel_h = sel_c >> ((2 * L).bit_length() - 1)
    left = (sel_c & (2 * L - 1)) < L
    sel = (left & (sel_r == sel_h)) | (jnp.logical_not(left) & (sel_r == H_M + sel_h))
    bc = _dot(jnp.concatenate(split(cols, 3), axis=1), sel.astype(F32).astype(BF16))

    qb = [q_ref[:, h * DK_M:(h + 1) * DK_M].astype(BF16) for h in heads]
    kt = [(k_ref[:, h * DK_M:(h + 1) * DK_M].astype(F32) * (DK_M ** -0.5)).T for h in heads]
    vo = [jnp.concatenate([v_ref[:, h * DV_M:(h + 1) * DV_M].astype(BF16), ones_b], axis=1) for h in heads]
    s_raw = [_dot(qb[h], kt[h].astype(BF16)) for h in heads]
    inter = [_dot(qb[h], jnp.concatenate([c_ref[h].astype(BF16), nmat_scr[h].astype(BF16)], axis=1))
             for h in heads]

    for h in heads:
        neg_mm = bc[:, 2 * L * h:2 * L * h + L]
        neg_m = bc[:, 2 * L * h + L:2 * L * (h + 1)]
        w = jnp.where(causal, jnp.exp(neg_mm + a[h:h + 1, :]), 0.0)
        gmat = jnp.exp(neg_mm + m0_col[h:h + 1, :])
        s_hi, s_lo = split(s_raw[h] * w, 2)
        r = _dot(s_hi, vo[h])
        num = gmat * inter[h][:, :L] + r[:, :L]
        den = gmat * inter[h][:, L:] + r[:, L:] + _dot(s_lo, ones_b)
        h_ref[:, h * DV_M:(h + 1) * DV_M] = num / jnp.maximum(jnp.abs(den), jnp.exp(neg_m))

    for h in heads:
        g_end = g_end_all[h:h + 1, :]
        kw_hi, kw_lo = split(kt[h] * ws_all[h:h + 1, :], 2)
        u = _dot(kw_hi, vo[h])
        c_ref[h] = g_end * c_ref[h] + u[:, :L]
        nmat_scr[h] = g_end * nmat_scr[h] + u[:, L:] + _dot(kw_lo, ones_b)

    @pl.when(pl.program_id(1) == pl.num_programs(1) - 1)
    def _():
        for h in heads:
            n_ref[h:h + 1, :] = nmat_scr[h].T[0:1, :]


def _mlstm(main, gates, c0, n0, m0, batch, L, layer=None, nseq=1):
    rows = main.shape[0]
    nc = rows // (batch * L)
    if L == HEAD_DIM:
        assert nseq == 1
        kern, scratch, lead = _mlstm_wide_kernel, [pltpu.VMEM((H_M, DK_M, HEAD_DIM), F32)], None
    else:
        assert nc == 1 and batch % nseq == 0
        kern, scratch, lead = functools.partial(_mlstm_kernel, L=L, nseq=nseq), [], nseq
    m0 = m0.reshape(m0.shape[:-1] + (1, H_M))

    def state_spec(tail, stacked):
        zeros = (0,) * len(tail)
        if stacked:
            return pl.BlockSpec((None, lead) + tail, lambda b, c: (layer, b) + zeros)
        return pl.BlockSpec((lead,) + tail, lambda b, c: (b,) + zeros)

    tails = ((H_M, DK_M, DV_M), (H_M, DK_M), (1, H_M))
    blk = nseq * L
    return pl.pallas_call(
        kern,
        grid=(batch // nseq, nc),
        in_specs=[
            pl.BlockSpec((blk, H_M * DK_M), lambda b, c: (b * nc + c, 0)),
            pl.BlockSpec((blk, H_M * DK_M), lambda b, c: (b * nc + c, 1)),
            pl.BlockSpec((blk, H_M * DV_M), lambda b, c: (b * nc + c, 1)),
            pl.BlockSpec((blk, N_GATE_PAD), lambda b, c: (b * nc + c, 0)),
        ] + [state_spec(t, layer is not None) for t in tails],
        out_specs=[pl.BlockSpec((blk, H_M * DV_M), lambda b, c: (b * nc + c, 0))]
        + [state_spec(t, False) for t in tails],
        out_shape=[
            jax.ShapeDtypeStruct((rows, H_M * DV_M), F32),
            jax.ShapeDtypeStruct((batch, H_M, DK_M, DV_M), F32),
            jax.ShapeDtypeStruct((batch, H_M, DK_M), F32),
            jax.ShapeDtypeStruct((batch, 1, H_M), F32),
        ],
        scratch_shapes=scratch,
        compiler_params=_cparams("arbitrary", "arbitrary"),
        name="mlstm",
    )(main, main, main, gates, c0, n0, m0)


def _attn_prompt_kernel(q_ref, kp_ref, kc_ref, vp_ref, vc_ref, o_ref,
                        qq_scr, kk_scr, vv_scr, num_scr, m_scr, den_scr):
    sb = pl.program_id(0)
    SB = ATT_SUPER
    qq_scr[...] = q_ref[...].astype(F32)
    kk_scr[0:SB, :] = kp_ref[...].astype(F32)
    kk_scr[SB:, :] = kc_ref[...].astype(F32)
    vv_scr[0:SB, :] = vp_ref[...].astype(F32)
    vv_scr[SB:, :] = vc_ref[...].astype(F32)

    shape = (ATT_BLOCK, 2 * ATT_BLOCK)
    qi = _iota(shape, 0)
    ki = _iota(shape, 1)
    window = (ki >= qi) & (ki <= qi + ATT_BLOCK)
    window_first = window & ((ki >= ATT_BLOCK) | (sb > 0))
    ones_b = jnp.ones((2 * ATT_BLOCK, HEAD_DIM), BF16)

    for bi, (_, dil) in enumerate(DILATIONS):
        for n in range(SB // (ATT_BLOCK * dil)):
            valid = window_first if n == 0 else window

            for r in range(dil):
                q_rows = pl.ds(r + n * ATT_BLOCK * dil, ATT_BLOCK, stride=dil)
                k_rows = pl.ds(SB + r + (n - 1) * ATT_BLOCK * dil, 2 * ATT_BLOCK, stride=dil)
                q = qq_scr[q_rows, :].astype(BF16)
                k = kk_scr[k_rows, :].astype(BF16)
                v = vv_scr[k_rows, :].astype(BF16)
                s = jnp.where(valid, _dot_nt(q, k), -jnp.inf)
                m = jnp.max(s, axis=1, keepdims=True)
                p = jnp.exp(s - m).astype(BF16)
                r = _dot(p, jnp.concatenate([v, ones_b], axis=1))
                num_scr[bi, q_rows, :] = r[:, :HEAD_DIM]
                den_scr[bi, q_rows, :] = r[:, HEAD_DIM:]
                m_scr[bi, q_rows, :] = jnp.broadcast_to(m, (ATT_BLOCK, HEAD_DIM))

    rows_per_step = 2 * ATT_BLOCK

    def mix(i, carry):
        rows = pl.ds(pl.multiple_of(i * rows_per_step, rows_per_step), rows_per_step)
        ms = [m_scr[b, rows, :] for b in range(len(DILATIONS))]
        big = ms[0]
        for mm in ms[1:]:
            big = jnp.maximum(big, mm)
        w0 = jnp.exp(ms[0] - big)
        acc_num = w0 * num_scr[0, rows, :]
        acc_den = w0 * den_scr[0, rows, :]
        for b in range(1, len(DILATIONS)):
            w = jnp.exp(ms[b] - big)
            acc_num = acc_num + w * num_scr[b, rows, :]
            acc_den = acc_den + w * den_scr[b, rows, :]
        o_ref[rows, :] = acc_num / acc_den
        return carry

    lax.fori_loop(0, SB // rows_per_step, mix, 0)


def _attn_prompt(main):
    s, n_main = main.shape
    SB = ATT_SUPER
    cols = n_main // HEAD_DIM // 6
    blk = (SB, HEAD_DIM)
    prev = lambda i: jnp.maximum(i - 1, 0)
    nbr = len(DILATIONS)
    return pl.pallas_call(
        _attn_prompt_kernel,
        grid=(s // SB, H_A),
        in_specs=[
            pl.BlockSpec(blk, lambda i, h: (i, 3 * cols + h)),
            pl.BlockSpec(blk, lambda i, h: (prev(i), 4 * cols + h)),
            pl.BlockSpec(blk, lambda i, h: (i, 4 * cols + h)),
            pl.BlockSpec(blk, lambda i, h: (prev(i), 5 * cols + h)),
            pl.BlockSpec(blk, lambda i, h: (i, 5 * cols + h)),
        ],
        out_specs=pl.BlockSpec(blk, lambda i, h: (i, h)),
        out_shape=jax.ShapeDtypeStruct((s, W_HEADS), F32),
        scratch_shapes=[
            pltpu.VMEM((SB, HEAD_DIM), F32),
            pltpu.VMEM((2 * SB, HEAD_DIM), F32),
            pltpu.VMEM((2 * SB, HEAD_DIM), F32),
            pltpu.VMEM((nbr, SB, HEAD_DIM), F32),
            pltpu.VMEM((nbr, SB, HEAD_DIM), F32),
            pltpu.VMEM((nbr, SB, HEAD_DIM), F32),
        ],
        compiler_params=_cparams("arbitrary", "arbitrary"),
        name="attn_prompt",
    )(main, main, main, main, main)


def _branch_count(delta):
    cnt = jnp.zeros(delta.shape, F32)
    for window, dil in DILATIONS:
        hit = (delta >= 0) & (delta <= window) & ((delta & (dil - 1)) == 0)
        cnt = cnt + hit.astype(F32)
    return cnt


def _attn_sample_kernel(q_ref, kn_ref, vn_ref, kfar_ref, knear_ref, vfar_ref, vnear_ref, o_ref,
                        cnt_scr, s_scr, *, T, NB, chunk):
    R = H_A * T
    t_bits = T.bit_length() - 1
    h_bits = H_A.bit_length() - 1
    d_far = DILATIONS[-1][1]
    n_groups, keep = kfar_ref.shape[0], kfar_ref.shape[1]
    keep_bits = keep.bit_length() - 1
    near_pos = knear_ref.shape[0] // H_A
    gpc = chunk // (keep * H_A)
    far_chunks = n_groups // gpc
    n_chunks = far_chunks + near_pos * H_A // chunk
    chunks = [slice(c * chunk, (c + 1) * chunk) for c in range(n_chunks)]

    def rows(far_ref, near_ref, c):
        if c < far_chunks:
            return far_ref[c * gpc:(c + 1) * gpc].reshape(chunk, HEAD_DIM)
        return near_ref[chunks[c - far_chunks], :]

    @pl.when(pl.program_id(0) == 0)
    def _():
        for c, sl in enumerate(chunks):
            row = _iota((R, chunk), 0)
            col = _iota((R, chunk), 1)
            if c < far_chunks:
                pos = (c * gpc + (col >> (h_bits + keep_bits))) * d_far + ((col >> h_bits) & (keep - 1))
            else:
                pos = NB - near_pos + (c - far_chunks) * (chunk // H_A) + (col >> h_bits)
            same_head = (row >> t_bits) == (col & (H_A - 1))
            cnt_scr[:, sl] = jnp.where(same_head, _branch_count(NB + (row & (T - 1)) - pos), 0.0)

    heads = [slice(h * HEAD_DIM, (h + 1) * HEAD_DIM) for h in range(H_A)]
    q = jnp.concatenate([q_ref[:, sl] for sl in heads], axis=0).astype(BF16)
    kn = jnp.concatenate([kn_ref[:, sl] for sl in heads], axis=0).astype(BF16)
    vn = jnp.concatenate([vn_ref[:, sl] for sl in heads], axis=0).astype(BF16)
    row = _iota((R, R), 0)
    col = _iota((R, R), 1)
    cnt_n = jnp.where((row >> t_bits) == (col >> t_bits),
                      _branch_count((row & (T - 1)) - (col & (T - 1))), 0.0)

    s_n = jnp.where(cnt_n > 0, _dot_nt(q, kn), -jnp.inf)
    big = jnp.max(s_n, axis=1, keepdims=True)
    for c, sl in enumerate(chunks):
        s = jnp.where(cnt_scr[:, sl] > 0, _dot_nt(q, rows(kfar_ref, knear_ref, c).astype(BF16)), -jnp.inf)
        s_scr[:, sl] = s
        big = jnp.maximum(big, jnp.max(s, axis=1, keepdims=True))

    p_n = cnt_n * jnp.exp(s_n - big)
    den = jnp.sum(p_n, axis=1, keepdims=True)
    num = _dot(p_n.astype(BF16), vn)
    for c, sl in enumerate(chunks):
        p = cnt_scr[:, sl] * jnp.exp(s_scr[:, sl] - big)
        den = den + jnp.sum(p, axis=1, keepdims=True)
        num = num + _dot(p.astype(BF16), rows(vfar_ref, vnear_ref, c).astype(BF16))
    out = num / den
    for h, sl in enumerate(heads):
        o_ref[:, sl] = out[h * T:(h + 1) * T, :]


def _attn_sample(main, cache_k, cache_v, layer, batch, T, chunk=2048):
    depth, _, nb, n_heads, hd = cache_k.shape
    assert n_heads == H_A and hd == HEAD_DIM and T & (T - 1) == 0 and H_A & (H_A - 1) == 0
    d_far = DILATIONS[-1][1]
    near_pos = DILATIONS[-2][0]
    assert all(w <= near_pos for w, _ in DILATIONS[:-1]) and nb % d_far == 0 and T <= d_far
    assert (nb - near_pos) % d_far == 0 and near_pos <= nb
    n_groups = (nb - near_pos) // d_far
    nf = nb * H_A
    n_keys = (n_groups * T + near_pos) * H_A
    assert (n_groups * T * H_A) % chunk == 0 and (near_pos * H_A) % chunk == 0 and nf % (near_pos * H_A) == 0
    far_view = lambda c: c.reshape(depth, batch, nb // d_far, d_far, H_A, HEAD_DIM)
    near_view = lambda c: c.reshape(depth, batch, nf, HEAD_DIM)
    kern = functools.partial(_attn_sample_kernel, T=T, NB=nb, chunk=chunk)
    far_spec = pl.BlockSpec((None, None, n_groups, T, H_A, HEAD_DIM), lambda b: (layer, b, 0, 0, 0, 0))
    near_spec = pl.BlockSpec((None, None, near_pos * H_A, HEAD_DIM),
                             lambda b: (layer, b, nf // (near_pos * H_A) - 1, 0))
    return pl.pallas_call(
        kern,
        grid=(batch,),
        in_specs=[
            pl.BlockSpec((T, W_HEADS), lambda b: (b, 3)),
            pl.BlockSpec((T, W_HEADS), lambda b: (b, 4)),
            pl.BlockSpec((T, W_HEADS), lambda b: (b, 5)),
            far_spec,
            near_spec,
            far_spec,
            near_spec,
        ],
        out_specs=pl.BlockSpec((T, W_HEADS), lambda b: (b, 0)),
        out_shape=jax.ShapeDtypeStruct((batch * T, W_HEADS), F32),
        scratch_shapes=[pltpu.VMEM((H_A * T, n_keys), F32), pltpu.VMEM((H_A * T, n_keys), F32)],
        compiler_params=_cparams("arbitrary"),
        name="attn_sample",
    )(main, main, main, far_view(cache_k), near_view(cache_k), far_view(cache_v), near_view(cache_v))


def _layer_norm(y, g, b):
    mu = jnp.mean(y, axis=1, keepdims=True)
    yc = y - mu
    var = jnp.mean(yc * yc, axis=1, keepdims=True)
    return yc * lax.rsqrt(var + LN_EPS) * g + b


def _merge_kernel(h_ref, om_ref, att_ref, x_ref, gt_ref, gh_ref, w_ref, lg_ref, lb_ref,
                  o_ref, cat_scr, *, alpha, sub):
    for r0 in range(0, cat_scr.shape[0], sub):
        rs = slice(r0, r0 + sub)
        for h in range(H_M):
            sl = slice(h * DV_M, (h + 1) * DV_M)
            hh = h_ref[rs, sl]
            hn = hh * lax.rsqrt(jnp.mean(hh * hh, axis=1, keepdims=True) + HEAD_NORM_EPS) * gh_ref[:, sl]
            cat_scr[rs, sl] = (hn * jax.nn.sigmoid(om_ref[rs, sl].astype(F32))).astype(BF16)
        cat_scr[rs, H_M * DV_M:] = att_ref[rs, :].astype(BF16)
        mix = _dot(cat_scr[rs, :], w_ref[...])
        gt = gt_ref[...] if gt_ref.shape[0] == 1 else gt_ref[rs, :]
        y = alpha * x_ref[rs, :] + (1.0 + gt) * mix
        o_ref[rs, :] = _layer_norm(y, lg_ref[...], lb_ref[...])


def _merge(h, main, att, x, gt, g_head, w_out, ln_g, ln_b, layer, alpha, tm, sub=256):
    rows, d = x.shape
    rm = gt.shape[0]
    mod_block = (1, d) if rm == 1 else (tm, d)
    mod_map = (lambda i: (0, 0)) if rm == 1 else (lambda i: (i, 0))
    wide = pl.BlockSpec((tm, W_HEADS), lambda i: (i, 0))
    const = lambda shape: pl.BlockSpec(shape, lambda i: (0, 0))
    return pl.pallas_call(
        functools.partial(_merge_kernel, alpha=alpha, sub=min(sub, tm)),
        grid=(rows // tm,),
        in_specs=[
            wide,
            pl.BlockSpec((tm, W_HEADS), lambda i: (i, 2)),
            wide,
            pl.BlockSpec((tm, d), lambda i: (i, 0)),
            pl.BlockSpec(mod_block, mod_map),
            const((1, W_HEADS)),
            pl.BlockSpec((None, d, d), lambda i: (layer, 0, 0)),
            const((1, d)),
            const((1, d)),
        ],
        out_specs=pl.BlockSpec((tm, d), lambda i: (i, 0)),
        out_shape=jax.ShapeDtypeStruct((rows, d), F32),
        scratch_shapes=[pltpu.VMEM((tm, d), BF16)],
        compiler_params=_cparams("arbitrary"),
        name="mixer_merge",
    )(h, main, att, x, gt, g_head, w_out, ln_g, ln_b)


def _ffn_kernel(*refs, alpha, seq_len, n_chunks, carry, sub):
    if carry:
        (x_ref, sc_ref, sh_ref, gt_ref, wg_ref, wv_ref, wd_ref, cw_ref, cb_ref, lg_ref, lb_ref,
         o_ref, tail_ref, u_scr, carry_scr) = refs
    else:
        (x_ref, sc_ref, sh_ref, gt_ref, wg_ref, wv_ref, wd_ref, cw_ref, cb_ref, lg_ref, lb_ref,
         h0_ref, h1_ref, o_ref, tail_ref, u_scr) = refs
    i = pl.program_id(0)
    j = pl.program_id(1)
    tm = u_scr.shape[0]
    assert n_chunks >= 2

    def rows_of(ref, rs):
        return ref[...] if ref.shape[0] == 1 else ref[rs, :]

    if carry:
        @pl.when(i == 0)
        def _():
            carry_scr[j] = jnp.zeros(carry_scr.shape[1:], F32)

    def step(first, last):
        cw = cw_ref[...]
        if carry:
            tail = carry_scr[j]
        for r0 in range(0, tm, sub):
            rs = slice(r0, r0 + sub)
            if first:
                ub = (x_ref[rs, :] * (1.0 + rows_of(sc_ref, rs)) + rows_of(sh_ref, rs)).astype(BF16)
                u_scr[rs, :] = ub
            else:
                ub = u_scr[rs, :]
            g = _dot(ub, wg_ref[...])
            v = _dot(ub, wv_ref[...])
            row = _iota(g.shape, 0)
            if carry:
                hist0, hist1 = tail[6:7, :], tail[7:8, :]
                pos = row
                tail = g[sub - 8:, :]
            else:
                hist0, hist1 = h0_ref[rs, :], h1_ref[rs, :]
                pos = row & (seq_len - 1)
                tail_ref[rs, :] = g
            prev1 = jnp.where(pos == 0, hist1, pltpu.roll(g, 1, 0))
            prev2 = jnp.where(pos == 0, hist0, jnp.where(pos == 1, hist1, pltpu.roll(g, 2, 0)))
            a = prev2 * cw[0:1, :] + prev1 * cw[1:2, :] + g * cw[2:3, :] + cb_ref[...]
            a = a * jax.nn.sigmoid(a) * v
            acc = _dot(a.astype(BF16), wd_ref[...])
            if not first:
                acc = o_ref[rs, :] + acc
            if last:
                y = alpha * x_ref[rs, :] + (1.0 + rows_of(gt_ref, rs)) * acc
                acc = _layer_norm(y, lg_ref[...], lb_ref[...])
            o_ref[rs, :] = acc
        if carry:
            carry_scr[j] = tail
            tail_ref[...] = tail

    pl.when(j == 0)(functools.partial(step, True, False))
    pl.when((j > 0) & (j < n_chunks - 1))(functools.partial(step, False, False))
    pl.when(j == n_chunks - 1)(functools.partial(step, False, True))


def _ffn(x, sc, sh, gt, w_up, w_down, conv_w, conv_b, ln_g, ln_b, layer, alpha, tm,
         hist=None, seq_len=None, tf=512, sub=512):
    rows, d = x.shape
    d_ff = w_down.shape[1]
    nj = d_ff // tf
    rm = sc.shape[0]
    mod_block = (1, d) if rm == 1 else (tm, d)
    mod_map = (lambda i, j: (0, 0)) if rm == 1 else (lambda i, j: (i, 0))
    carry = hist is None
    in_specs = [
        pl.BlockSpec((tm, d), lambda i, j: (i, 0), pipeline_mode=pl.Buffered(1)),
        pl.BlockSpec(mod_block, mod_map),
        pl.BlockSpec(mod_block, mod_map),
        pl.BlockSpec(mod_block, mod_map),
        pl.BlockSpec((None, d, tf), lambda i, j: (layer, 0, j)),
        pl.BlockSpec((None, d, tf), lambda i, j: (layer, 0, nj + j)),
        pl.BlockSpec((None, tf, d), lambda i, j: (layer, j, 0)),
        pl.BlockSpec((CONV_W, tf), lambda i, j: (0, j)),
        pl.BlockSpec((1, tf), lambda i, j: (0, j)),
        pl.BlockSpec((1, d), lambda i, j: (0, 0)),
        pl.BlockSpec((1, d), lambda i, j: (0, 0)),
    ]
    args = [x, sc, sh, gt, w_up, w_up, w_down, conv_w, conv_b, ln_g, ln_b]
    scratch = [pltpu.VMEM((tm, d), BF16)]
    if carry:
        tail_rows = 8
        scratch.append(pltpu.VMEM((nj, 8, tf), F32))
    else:
        tail_rows = tm
        in_specs += [pl.BlockSpec((tm, tf), lambda i, j: (i, j))] * 2
        args += list(hist)
    kern = functools.partial(_ffn_kernel, alpha=alpha, seq_len=seq_len, n_chunks=nj, carry=carry,
                             sub=min(sub, tm))
    return pl.pallas_call(
        kern,
        grid=(rows // tm, nj),
        in_specs=in_specs,
        out_specs=[
            pl.BlockSpec((tm, d), lambda i, j: (i, 0)),
            pl.BlockSpec((tail_rows, tf), lambda i, j: (i, j)),
        ],
        out_shape=[
            jax.ShapeDtypeStruct((rows, d), F32),
            jax.ShapeDtypeStruct((rows // tm * tail_rows, d_ff), F32),
        ],
        scratch_shapes=scratch,
        compiler_params=_cparams("arbitrary", "arbitrary"),
        name="conv_ffn",
    )(*args)


def _rope_tables(pos):
    half = HEAD_DIM // 2
    inv = ROPE_THETA ** (-jnp.arange(half, dtype=F32) / half)
    ang = pos.astype(F32)[:, None] * inv[None, :]
    cos, sin = jnp.cos(ang), jnp.sin(ang)
    return jnp.concatenate([cos, cos], -1), jnp.concatenate([-sin, sin], -1)


def kernel(x_prompt, x_sample, cache_k_win, cache_v_win, state_C, state_n, state_m, state_conv,
           c_prompt, c_sample, w_ada, b_ada, w_in, b_gate, g_head, w_out, ln1_g, ln1_b,
           w_up, conv_w, conv_b, w_down, ln2_g, ln2_b):
    bp, s, d = x_prompt.shape
    bs, t, _ = x_sample.shape
    depth = w_in.shape[0]
    d_ff = w_down.shape[1]
    alpha = (2 * depth) ** 0.25
    assert bp == 1 and d == (H_M + H_A) * HEAD_DIM and s % ATT_SUPER == 0

    n_c = bp + bs
    pad = (-n_c) % 8
    c_all = jnp.concatenate([c_prompt, c_sample, jnp.zeros((pad, d), F32)], 0)
    mod = _modulation(c_all, w_ada, b_ada)

    n_m = 2 * H_M * DK_M + 2 * H_M * DV_M
    w_m = w_in[:, :, :n_m].astype(BF16)
    w_a = w_in[:, :, n_m + 2 * H_M:].astype(BF16)
    w_gate = jnp.pad(w_in[:, :, n_m:n_m + 2 * H_M], ((0, 0), (0, 0), (0, N_GATE_PAD - 2 * H_M))).astype(BF16)
    gate_bias = jnp.pad(b_gate.reshape(depth, 1, 2 * H_M), ((0, 0), (0, 0), (0, N_GATE_PAD - 2 * H_M)))
    w_out_b = w_out.astype(BF16)
    w_up_b = w_up.astype(BF16)
    w_down_b = w_down.astype(BF16)

    cc_p, ss_p = _rope_tables(jnp.arange(s, dtype=jnp.int32))
    pos_s = PAST_LEN + jnp.arange(t, dtype=jnp.int32)
    cc_s, ss_s = (jnp.tile(a, (bs, 1)) for a in _rope_tables(pos_s))

    xp = x_prompt.reshape(s, d)
    xs = x_sample.reshape(bs * t, d)
    rows_s = bs * t
    tm_p = 1024
    outs =[[] for _ in range(12)]
    zeros_c = jnp.zeros((bp, H_M, DK_M, DV_M), F32)
    zeros_n = jnp.zeros((bp, H_M, DK_M), F32)
    zeros_m = jnp.zeros((bp, H_M), F32)
    row2 = lambda v: v.reshape(1, -1)

    for l in range(depth):
        mod_p = [mod[l, 0:1, k * d:(k + 1) * d] for k in range(6)]
        mod_s = [jnp.repeat(mod[l, bp:bp + bs, k * d:(k + 1) * d], t, axis=0) for k in range(6)]
        ln1 = (row2(ln1_g[l]), row2(ln1_b[l]))
        ffn_small = (conv_w[l], row2(conv_b[l]), row2(ln2_g[l]), row2(ln2_b[l]))

        main_p, kv_p, gates_p = _inproj(xp, mod_p[1], mod_p[0], w_m, w_a, w_gate, gate_bias, cc_p, ss_p, l,
                                        tm=tm_p, main_dtype=BF16)
        h_p, c_p, n_p, m_p = _mlstm(main_p, gates_p, zeros_c, zeros_n, zeros_m, batch=bp, L=128)
        att_p = _attn_prompt(main_p)
        x1_p = _merge(h_p, main_p, att_p, xp, mod_p[2], row2(g_head[l]), w_out_b, *ln1, l, alpha, tm=512)
        xp, tail_p = _ffn(x1_p, mod_p[4], mod_p[3], mod_p[5], w_up_b, w_down_b, *ffn_small, l, alpha, tm=tm_p)

        main_s, kv_s, gates_s = _inproj(xs, mod_s[1], mod_s[0], w_m, w_a, w_gate, gate_bias, cc_s, ss_s, l,
                                        tm=rows_s, main_dtype=F32)
        h_s, c_s, n_s, m_s = _mlstm(main_s, gates_s, state_C, state_n, state_m, batch=bs, L=t, layer=l,
                                    nseq=MLSTM_SEQS_PER_STEP)
        att_s = _attn_sample(main_s, cache_k_win, cache_v_win, l, bs, t)
        x1_s = _merge(h_s, main_s, att_s, xs, mod_s[2], row2(g_head[l]), w_out_b, *ln1, l, alpha, tm=rows_s)
        hist = [jnp.repeat(state_conv[l][:, r, :], t, axis=0) for r in range(CONV_W - 1)]
        xs, g_s = _ffn(x1_s, mod_s[4], mod_s[3], mod_s[5], w_up_b, w_down_b, *ffn_small, l, alpha, tm=rows_s,
                       hist=hist, seq_len=t)

        wp = min(DILATIONS[-1][0], s)
        k_cols = slice(0, W_HEADS)
        v_cols = slice(W_HEADS, 2 * W_HEADS)
        outs[0].append(kv_p[s - wp:, k_cols].reshape(bp, wp, H_A, HEAD_DIM))
        outs[1].append(kv_p[s - wp:, v_cols].reshape(bp, wp, H_A, HEAD_DIM))
        outs[2].append(c_p)
        outs[3].append(n_p)
        outs[4].append(m_p.reshape(bp, H_M))
        outs[5].append(tail_p[-8:][8 - (CONV_W - 1):].reshape(bp, CONV_W - 1, d_ff))
        outs[6].append(kv_s[:, k_cols].reshape(bs, t, H_A, HEAD_DIM))
        outs[7].append(kv_s[:, v_cols].reshape(bs, t, H_A, HEAD_DIM))
        outs[8].append(c_s)
        outs[9].append(n_s)
        outs[10].append(m_s.reshape(bs, H_M))
        outs[11].append(g_s.reshape(bs, t, d_ff)[:, t - (CONV_W - 1):])

    return (xp.reshape(bp, s, d), xs.reshape(bs, t, d)) + tuple(jnp.stack(o) for o in outs)
```

```python
import functools

import jax
import jax.numpy as jnp
from jax import lax
from jax.experimental import pallas as pl
from jax.experimental.pallas import tpu as pltpu

F32 = jnp.float32
BF16 = jnp.bfloat16
HIGHEST = lax.Precision.HIGHEST

HEAD_DIM = 128
H_M = 8
H_A = 8
DK_M = 64
DV_M = 128
DILATIONS = ((128, 1), (512, 4), (2048, 16))
ATT_BLOCK = 128
PAST_LEN = 8192
ROPE_THETA = 10000.0
CONV_W = 3
LN_EPS = 1e-5
HEAD_NORM_EPS = 1e-6
N_GATE_PAD = 128
W_HEADS = H_A * HEAD_DIM
ATT_SUPER = ATT_BLOCK * DILATIONS[-1][1]
MLSTM_SEQS_PER_STEP = 4
MXU_COLS = 256
VMEM_LIMIT_BYTES = 60 * 1024 * 1024


def _cparams(*sem):
    return pltpu.CompilerParams(dimension_semantics=sem, vmem_limit_bytes=VMEM_LIMIT_BYTES)


def _dot(a, b):
    return jnp.dot(a, b, preferred_element_type=F32)


def _dot_nt(a, b, precision=None):
    return lax.dot_general(a, b, (((1,), (1,)), ((), ())), precision=precision,
                           preferred_element_type=F32)


def _dot_tn(a, b):
    return lax.dot_general(a, b, (((0,), (0,)), ((), ())), preferred_element_type=F32)


def _iota(shape, dim):
    return lax.broadcasted_iota(jnp.int32, shape, dim)


def _mod_kernel(c_ref, w_ref, b_ref, o_ref):
    c = c_ref[...]
    a = (c * jax.nn.sigmoid(c)).astype(BF16)
    o_ref[...] = _dot(a, w_ref[...].astype(BF16)) + b_ref[...]


def _modulation(c_all, w_ada, b_ada, tn=1024):
    depth, d, n = w_ada.shape
    rows = c_all.shape[0]
    return pl.pallas_call(
        _mod_kernel,
        grid=(depth, n // tn),
        in_specs=[
            pl.BlockSpec((rows, d), lambda l, j: (0, 0)),
            pl.BlockSpec((None, d, tn), lambda l, j: (l, 0, j)),
            pl.BlockSpec((None, 1, tn), lambda l, j: (l, 0, j)),
        ],
        out_specs=pl.BlockSpec((None, rows, tn), lambda l, j: (l, 0, j)),
        out_shape=jax.ShapeDtypeStruct((depth, rows, n), F32),
        compiler_params=_cparams("arbitrary", "arbitrary"),
        name="adaln_mod",
    )(c_all, w_ada, b_ada.reshape(depth, 1, n))


def _inproj_kernel(x_ref, sc_ref, sh_ref, wm_ref, wa_ref, wg_ref, gb_ref, cc_ref, ss_ref,
                   main_ref, kv_ref, gate_ref, u_scr, *, tn, q_lo, k_lo, v_lo, sub, slab):
    j = pl.program_id(1)
    tm = u_scr.shape[0]

    def rows_of(ref, rs):
        return ref[...] if ref.shape[0] == 1 else ref[rs, :]

    def finish(kind, acc, rs, c0):
        width = acc.shape[1]
        cols = slice(c0, c0 + width)
        if kind == "plain":
            main_ref[rs, cols] = acc.astype(main_ref.dtype)
        elif kind == "v":
            main_ref[rs, cols] = acc.astype(main_ref.dtype)
            kv_ref[rs, cols] = acc
        else:
            for g0 in range(0, width, HEAD_DIM):
                a = acc[:, g0:g0 + HEAD_DIM]
                sl = slice(c0 + g0, c0 + g0 + HEAD_DIM)
                y = a * cc_ref[rs, :] + pltpu.roll(a, HEAD_DIM // 2, 1) * ss_ref[rs, :]
                if kind == "q":
                    main_ref[rs, sl] = (y * (HEAD_DIM ** -0.5)).astype(main_ref.dtype)
                else:
                    main_ref[rs, sl] = y.astype(main_ref.dtype)
                    kv_ref[rs, sl] = y

    def step(kind, first):
        w_ref = wm_ref if kind == "plain" else wa_ref
        if first:
            for r0 in range(0, tm, sub):
                rs = slice(r0, r0 + sub)
                ub = (x_ref[rs, :] * (1.0 + rows_of(sc_ref, rs)) + rows_of(sh_ref, rs)).astype(BF16)
                u_scr[rs, :] = ub
                z = _dot(ub, wg_ref[...]) + gb_ref[...]
                lane = _iota(z.shape, 1)
                log_sig = jnp.minimum(z, 0.0) - jnp.log1p(jnp.exp(-jnp.abs(z)))
                gate_ref[rs, :] = jnp.where((lane >= H_M) & (lane < 2 * H_M), log_sig, z)
                finish(kind, _dot(ub, w_ref[...]), rs, 0)
        else:
            ub = u_scr[...]
            for c0 in range(0, tn, slab):
                finish(kind, _dot(ub, w_ref[:, c0:c0 + slab]), slice(None), c0)

    pl.when(j == 0)(functools.partial(step, "plain", True))
    pl.when((j > 0) & (j < q_lo))(functools.partial(step, "plain", False))
    pl.when((j >= q_lo) & (j < k_lo))(functools.partial(step, "q", False))
    pl.when((j >= k_lo) & (j < v_lo))(functools.partial(step, "k", False))
    pl.when(j >= v_lo)(functools.partial(step, "v", False))


def _inproj(x, sc, sh, w_m, w_a, w_gate, gate_bias, rope_cc, rope_ss, layer, tm, main_dtype, tn=1024, sub=512):
    rows, d = x.shape
    n = w_m.shape[2] + w_a.shape[2]
    rm = sc.shape[0]
    mod_block = (1, d) if rm == 1 else (tm, d)
    mod_map = (lambda i, j: (0, 0)) if rm == 1 else (lambda i, j: (i, 0))
    q_lo = (3 * W_HEADS) // tn
    k_lo = (4 * W_HEADS) // tn
    v_lo = (5 * W_HEADS) // tn
    kern = functools.partial(_inproj_kernel, tn=tn, q_lo=q_lo, k_lo=k_lo, v_lo=v_lo, sub=min(sub, tm),
                             slab=MXU_COLS)
    return pl.pallas_call(
        kern,
        grid=(rows // tm, n // tn),
        in_specs=[
            pl.BlockSpec((tm, d), lambda i, j: (i, 0)),
            pl.BlockSpec(mod_block, mod_map),
            pl.BlockSpec(mod_block, mod_map),
            pl.BlockSpec((None, d, tn), lambda i, j: (layer, 0, jnp.minimum(j, q_lo - 1))),
            pl.BlockSpec((None, d, tn), lambda i, j: (layer, 0, jnp.maximum(j - q_lo, 0))),
            pl.BlockSpec((None, d, N_GATE_PAD), lambda i, j: (layer, 0, 0)),
            pl.BlockSpec((None, 1, N_GATE_PAD), lambda i, j: (layer, 0, 0)),
            pl.BlockSpec((tm, HEAD_DIM), lambda i, j: (i, 0)),
            pl.BlockSpec((tm, HEAD_DIM), lambda i, j: (i, 0)),
        ],
        out_specs=[
            pl.BlockSpec((tm, tn), lambda i, j: (i, j)),
            pl.BlockSpec((tm, tn), lambda i, j: (i, jnp.maximum(j - k_lo, 0))),
            pl.BlockSpec((tm, N_GATE_PAD), lambda i, j: (i, 0)),
        ],
        out_shape=[
            jax.ShapeDtypeStruct((rows, n), main_dtype),
            jax.ShapeDtypeStruct((rows, 2 * W_HEADS), F32),
            jax.ShapeDtypeStruct((rows, N_GATE_PAD), F32),
        ],
        scratch_shapes=[pltpu.VMEM((tm, d), BF16)],
        compiler_params=_cparams("arbitrary", "arbitrary"),
        name="inproj",
    )(x, sc, sh, w_m, w_a, w_gate, gate_bias, rope_cc, rope_ss)


def _mlstm_kernel(q_ref, k_ref, v_ref, g_ref, c0_ref, n0_ref, m0_ref,
                  h_ref, c_ref, n_ref, m_ref, *, L, nseq):
    @pl.when(pl.program_id(1) == 0)
    def _():
        c_ref[...] = c0_ref[...]
        n_ref[...] = n0_ref[...]
        m_ref[...] = m0_ref[...]

    row = _iota((L, L), 0)
    col = _iota((L, L), 1)
    causal = row >= col
    tril = causal.astype(F32)
    triu = (row <= col).astype(F32)
    eye = (_iota((2 * H_M, N_GATE_PAD), 0) == _iota((2 * H_M, N_GATE_PAD), 1)).astype(F32)
    hi_dot = functools.partial(jnp.dot, precision=HIGHEST, preferred_element_type=F32)

    seqs = []
    for bi in range(nseq):
        gates = g_ref[bi * L:(bi + 1) * L, :]
        gates_t = _dot_nt(eye, gates, precision=HIGHEST)
        seqs.append((gates, gates_t, hi_dot(tril, gates), hi_dot(gates_t, triu)))

    units = []
    for bi, (gates, gates_t, bcol_all, brow_all) in enumerate(seqs):
        rows = slice(bi * L, (bi + 1) * L)
        for h in range(H_M):
            b_col = bcol_all[:, H_M + h:H_M + h + 1]
            b_row = brow_all[H_M + h:H_M + h + 1, :]
            m0 = m_ref[bi, :, h:h + 1]
            dmat = jnp.where(causal, b_col - b_row + gates_t[h:h + 1, :], -jnp.inf)
            m_inter = b_col + m0
            m = jnp.maximum(m_inter, jnp.max(dmat, axis=1, keepdims=True))
            q = q_ref[rows, h * DK_M:(h + 1) * DK_M].astype(F32)
            k = k_ref[rows, h * DK_M:(h + 1) * DK_M].astype(F32) * (DK_M ** -0.5)
            qb = q.astype(BF16)
            units.append(dict(
                bi=bi, h=h, rows=rows, m0=m0, m=m, w=jnp.exp(dmat - m), g=jnp.exp(m_inter - m),
                q=q, k=k, qb=qb, vb=v_ref[rows, h * DV_M:(h + 1) * DV_M].astype(BF16),
                s_raw=_dot_nt(qb, k.astype(BF16)), b_col=b_col, ig_col=gates[:, h:h + 1]))

    for u in units:
        bi, h = u["bi"], u["h"]
        s = u["s_raw"] * u["w"]
        num = u["g"] * _dot(u["qb"], c_ref[bi, h].astype(BF16)) + _dot(s.astype(BF16), u["vb"])
        den = (u["g"] * jnp.sum(u["q"] * n_ref[bi, h:h + 1, :], axis=1, keepdims=True)
               + jnp.sum(s, axis=1, keepdims=True))
        h_ref[u["rows"], h * DV_M:(h + 1) * DV_M] = num / jnp.maximum(jnp.abs(den), jnp.exp(-u["m"]))

    for u in units:
        bi, h, b_col, m0 = u["bi"], u["h"], u["b_col"], u["m0"]
        b_last = b_col[L - 1:L, :]
        a_col = b_last - b_col + u["ig_col"]
        m_end = jnp.maximum(b_last + m0, jnp.max(a_col, axis=0, keepdims=True))
        g_end = jnp.exp(b_last + m0 - m_end)
        kw = jnp.exp(a_col - m_end) * u["k"]
        c_ref[bi, h] = g_end * c_ref[bi, h] + _dot_tn(kw.astype(BF16), u["vb"])
        n_ref[bi, h:h + 1, :] = g_end * n_ref[bi, h:h + 1, :] + jnp.sum(kw, axis=0, keepdims=True)
        m_ref[bi, :, h:h + 1] = m_end


def _mlstm_wide_kernel(q_ref, k_ref, v_ref, g_ref, c0_ref, n0_ref, m0_ref,
                       h_ref, c_ref, n_ref, m_ref, nmat_scr):
    L = HEAD_DIM
    first = pl.program_id(1) == 0
    sub8 = _iota((H_M, L), 0)

    @pl.when(first)
    def _():
        c_ref[...] = c0_ref[...]
        n_ref[...] = n0_ref[...]
        m_ref[...] = m0_ref[...]
        for h in range(H_M):
            nmat_scr[h] = lax.dot_general(n0_ref[...], (sub8 == h).astype(F32), (((0,), (0,)), ((), ())),
                                          precision=HIGHEST, preferred_element_type=F32)

    gates = g_ref[...]
    row = _iota((L, L), 0)
    col = _iota((L, L), 1)
    causal = row >= col
    triu = (row <= col).astype(F32)
    eye = (_iota((2 * H_M, N_GATE_PAD), 0) == _iota((2 * H_M, N_GATE_PAD), 1)).astype(F32)
    gates_t = _dot_nt(eye, gates, precision=HIGHEST)
    ig = gates_t[:H_M]
    b = jnp.dot(gates_t[H_M:], triu, precision=HIGHEST, preferred_element_type=F32)
    a = ig - b
    cm = a
    lane = _iota((H_M, L), 1)
    shift = 1
    while shift < L:
        cm = jnp.maximum(cm, jnp.where(lane >= shift, pltpu.roll(cm, shift, 1), -jnp.inf))
        shift *= 2

    eye8 = _iota((H_M, H_M), 0) == _iota((H_M, H_M), 1)
    m0_col = jnp.sum(jnp.where(eye8, jnp.broadcast_to(m_ref[...], (H_M, H_M)), 0.0), axis=1, keepdims=True)
    mm = jnp.maximum(m0_col, cm)
    mm_last = mm[:, L - 1:L]
    ws_all = jnp.exp(a - mm_last)
    g_end_all = jnp.exp(m0_col - mm_last)
    m_end = b[:, L - 1:L] + mm_last
    m_ref[...] = jnp.sum(jnp.where(eye8, jnp.broadcast_to(m_end, (H_M, H_M)), 0.0), axis=0, keepdims=True)
    def split(x, parts):
        out = []
        for _ in range(parts):
            p = x.astype(BF16)
            out.append(p)
            x = x - p.astype(F32)
        return out

    heads = range(H_M)
    ones_b = jnp.ones((L, L), BF16)

    cols = jnp.concatenate([-mm, -mm - b], axis=0).T
    n_sel = 3 * 2 * H_M
    sel_r = _iota((n_sel, H_M * 2 * L), 0) & (2 * H_M - 1)
    sel_c = _iota((n_sel, H_M * 2 * L), 1)
    sel_h = sel_c >> ((2 * L).bit_length() - 1)
    left = (sel_c & (2 * L - 1)) < L
    sel = (left & (sel_r == sel_h)) | (jnp.logical_not(left) & (sel_r == H_M + sel_h))
    bc = _dot(jnp.concatenate(split(cols, 3), axis=1), sel.astype(F32).astype(BF16))

    qb = [q_ref[:, h * DK_M:(h + 1) * DK_M].astype(BF16) for h in heads]
    kt = [(k_ref[:, h * DK_M:(h + 1) * DK_M].astype(F32) * (DK_M ** -0.5)).T for h in heads]
    vo = [jnp.concatenate([v_ref[:, h * DV_M:(h + 1) * DV_M].astype(BF16), ones_b], axis=1) for h in heads]
    s_raw = [_dot(qb[h], kt[h].astype(BF16)) for h in heads]
    inter = [_dot(qb[h], jnp.concatenate([c_ref[h].astype(BF16), nmat_scr[h].astype(BF16)], axis=1))
             for h in heads]

    for h in heads:
        neg_mm = bc[:, 2 * L * h:2 * L * h + L]
        neg_m = bc[:, 2 * L * h + L:2 * L * (h + 1)]
        w = jnp.where(causal, jnp.exp(neg_mm + a[h:h + 1, :]), 0.0)
        gmat = jnp.exp(neg_mm + m0_col[h:h + 1, :])
        s_hi, s_lo = split(s_raw[h] * w, 2)
        r = _dot(s_hi, vo[h])
        num = gmat * inter[h][:, :L] + r[:, :L]
        den = gmat * inter[h][:, L:] + r[:, L:] + _dot(s_lo, ones_b)
        h_ref[:, h * DV_M:(h + 1) * DV_M] = num / jnp.maximum(jnp.abs(den), jnp.exp(neg_m))

    for h in heads:
        g_end = g_end_all[h:h + 1, :]
        kw_hi, kw_lo = split(kt[h] * ws_all[h:h + 1, :], 2)
        u = _dot(kw_hi, vo[h])
        c_ref[h] = g_end * c_ref[h] + u[:, :L]
        nmat_scr[h] = g_end * nmat_scr[h] + u[:, L:] + _dot(kw_lo, ones_b)

    @pl.when(pl.program_id(1) == pl.num_programs(1) - 1)
    def _():
        for h in heads:
            n_ref[h:h + 1, :] = nmat_scr[h].T[0:1, :]


def _mlstm(main, gates, c0, n0, m0, batch, L, layer=None, nseq=1):
    rows = main.shape[0]
    nc = rows // (batch * L)
    if L == HEAD_DIM:
        assert nseq == 1
        kern, scratch, lead = _mlstm_wide_kernel, [pltpu.VMEM((H_M, DK_M, HEAD_DIM), F32)], None
    else:
        assert nc == 1 and batch % nseq == 0
        kern, scratch, lead = functools.partial(_mlstm_kernel, L=L, nseq=nseq), [], nseq
    m0 = m0.reshape(m0.shape[:-1] + (1, H_M))

    def state_spec(tail, stacked):
        zeros = (0,) * len(tail)
        if stacked:
            return pl.BlockSpec((None, lead) + tail, lambda b, c: (layer, b) + zeros)
        return pl.BlockSpec((lead,) + tail, lambda b, c: (b,) + zeros)

    tails = ((H_M, DK_M, DV_M), (H_M, DK_M), (1, H_M))
    blk = nseq * L
    return pl.pallas_call(
        kern,
        grid=(batch // nseq, nc),
        in_specs=[
            pl.BlockSpec((blk, H_M * DK_M), lambda b, c: (b * nc + c, 0)),
            pl.BlockSpec((blk, H_M * DK_M), lambda b, c: (b * nc + c, 1)),
            pl.BlockSpec((blk, H_M * DV_M), lambda b, c: (b * nc + c, 1)),
            pl.BlockSpec((blk, N_GATE_PAD), lambda b, c: (b * nc + c, 0)),
        ] + [state_spec(t, layer is not None) for t in tails],
        out_specs=[pl.BlockSpec((blk, H_M * DV_M), lambda b, c: (b * nc + c, 0))]
        + [state_spec(t, False) for t in tails],
        out_shape=[
            jax.ShapeDtypeStruct((rows, H_M * DV_M), F32),
            jax.ShapeDtypeStruct((batch, H_M, DK_M, DV_M), F32),
            jax.ShapeDtypeStruct((batch, H_M, DK_M), F32),
            jax.ShapeDtypeStruct((batch, 1, H_M), F32),
        ],
        scratch_shapes=scratch,
        compiler_params=_cparams("arbitrary", "arbitrary"),
        name="mlstm",
    )(main, main, main, gates, c0, n0, m0)


def _attn_prompt_kernel(q_ref, kp_ref, kc_ref, vp_ref, vc_ref, o_ref,
                        qq_scr, kk_scr, vv_scr, num_scr, m_scr, den_scr):
    sb = pl.program_id(0)
    SB = ATT_SUPER
    qq_scr[...] = q_ref[...].astype(F32)
    kk_scr[0:SB, :] = kp_ref[...].astype(F32)
    kk_scr[SB:, :] = kc_ref[...].astype(F32)
    vv_scr[0:SB, :] = vp_ref[...].astype(F32)
    vv_scr[SB:, :] = vc_ref[...].astype(F32)

    shape = (ATT_BLOCK, 2 * ATT_BLOCK)
    qi = _iota(shape, 0)
    ki = _iota(shape, 1)
    window = (ki >= qi) & (ki <= qi + ATT_BLOCK)
    window_first = window & ((ki >= ATT_BLOCK) | (sb > 0))
    ones_b = jnp.ones((2 * ATT_BLOCK, HEAD_DIM), BF16)

    for bi, (_, dil) in enumerate(DILATIONS):
        for n in range(SB // (ATT_BLOCK * dil)):
            valid = window_first if n == 0 else window

            for r in range(dil):
                q_rows = pl.ds(r + n * ATT_BLOCK * dil, ATT_BLOCK, stride=dil)
                k_rows = pl.ds(SB + r + (n - 1) * ATT_BLOCK * dil, 2 * ATT_BLOCK, stride=dil)
                q = qq_scr[q_rows, :].astype(BF16)
                k = kk_scr[k_rows, :].astype(BF16)
                v = vv_scr[k_rows, :].astype(BF16)
                s = jnp.where(valid, _dot_nt(q, k), -jnp.inf)
                m = jnp.max(s, axis=1, keepdims=True)
                p = jnp.exp(s - m).astype(BF16)
                r = _dot(p, jnp.concatenate([v, ones_b], axis=1))
                num_scr[bi, q_rows, :] = r[:, :HEAD_DIM]
                den_scr[bi, q_rows, :] = r[:, HEAD_DIM:]
                m_scr[bi, q_rows, :] = jnp.broadcast_to(m, (ATT_BLOCK, HEAD_DIM))

    rows_per_step = 2 * ATT_BLOCK

    def mix(i, carry):
        rows = pl.ds(pl.multiple_of(i * rows_per_step, rows_per_step), rows_per_step)
        ms = [m_scr[b, rows, :] for b in range(len(DILATIONS))]
        big = ms[0]
        for mm in ms[1:]:
            big = jnp.maximum(big, mm)
        w0 = jnp.exp(ms[0] - big)
        acc_num = w0 * num_scr[0, rows, :]
        acc_den = w0 * den_scr[0, rows, :]
        for b in range(1, len(DILATIONS)):
            w = jnp.exp(ms[b] - big)
            acc_num = acc_num + w * num_scr[b, rows, :]
            acc_den = acc_den + w * den_scr[b, rows, :]
        o_ref[rows, :] = acc_num / acc_den
        return carry

    lax.fori_loop(0, SB // rows_per_step, mix, 0)


def _attn_prompt(main):
    s, n_main = main.shape
    SB = ATT_SUPER
    cols = n_main // HEAD_DIM // 6
    blk = (SB, HEAD_DIM)
    prev = lambda i: jnp.maximum(i - 1, 0)
    nbr = len(DILATIONS)
    return pl.pallas_call(
        _attn_prompt_kernel,
        grid=(s // SB, H_A),
        in_specs=[
            pl.BlockSpec(blk, lambda i, h: (i, 3 * cols + h)),
            pl.BlockSpec(blk, lambda i, h: (prev(i), 4 * cols + h)),
            pl.BlockSpec(blk, lambda i, h: (i, 4 * cols + h)),
            pl.BlockSpec(blk, lambda i, h: (prev(i), 5 * cols + h)),
            pl.BlockSpec(blk, lambda i, h: (i, 5 * cols + h)),
        ],
        out_specs=pl.BlockSpec(blk, lambda i, h: (i, h)),
        out_shape=jax.ShapeDtypeStruct((s, W_HEADS), F32),
        scratch_shapes=[
            pltpu.VMEM((SB, HEAD_DIM), F32),
            pltpu.VMEM((2 * SB, HEAD_DIM), F32),
            pltpu.VMEM((2 * SB, HEAD_DIM), F32),
            pltpu.VMEM((nbr, SB, HEAD_DIM), F32),
            pltpu.VMEM((nbr, SB, HEAD_DIM), F32),
            pltpu.VMEM((nbr, SB, HEAD_DIM), F32),
        ],
        compiler_params=_cparams("arbitrary", "arbitrary"),
        name="attn_prompt",
    )(main, main, main, main, main)


def _branch_count(delta):
    cnt = jnp.zeros(delta.shape, F32)
    for window, dil in DILATIONS:
        hit = (delta >= 0) & (delta <= window) & ((delta & (dil - 1)) == 0)
        cnt = cnt + hit.astype(F32)
    return cnt


def _attn_sample_kernel(q_ref, kn_ref, vn_ref, kfar_ref, knear_ref, vfar_ref, vnear_ref, o_ref,
                        cnt_scr, s_scr, *, T, NB, chunk):
    R = H_A * T
    t_bits = T.bit_length() - 1
    h_bits = H_A.bit_length() - 1
    d_far = DILATIONS[-1][1]
    n_groups, keep = kfar_ref.shape[0], kfar_ref.shape[1]
    keep_bits = keep.bit_length() - 1
    near_pos = knear_ref.shape[0] // H_A
    gpc = chunk // (keep * H_A)
    far_chunks = n_groups // gpc
    n_chunks = far_chunks + near_pos * H_A // chunk
    chunks = [slice(c * chunk, (c + 1) * chunk) for c in range(n_chunks)]

    def rows(far_ref, near_ref, c):
        if c < far_chunks:
            return far_ref[c * gpc:(c + 1) * gpc].reshape(chunk, HEAD_DIM)
        return near_ref[chunks[c - far_chunks], :]

    @pl.when(pl.program_id(0) == 0)
    def _():
        for c, sl in enumerate(chunks):
            row = _iota((R, chunk), 0)
            col = _iota((R, chunk), 1)
            if c < far_chunks:
                pos = (c * gpc + (col >> (h_bits + keep_bits))) * d_far + ((col >> h_bits) & (keep - 1))
            else:
                pos = NB - near_pos + (c - far_chunks) * (chunk // H_A) + (col >> h_bits)
            same_head = (row >> t_bits) == (col & (H_A - 1))
            cnt_scr[:, sl] = jnp.where(same_head, _branch_count(NB + (row & (T - 1)) - pos), 0.0)

    heads = [slice(h * HEAD_DIM, (h + 1) * HEAD_DIM) for h in range(H_A)]
    q = jnp.concatenate([q_ref[:, sl] for sl in heads], axis=0).astype(BF16)
    kn = jnp.concatenate([kn_ref[:, sl] for sl in heads], axis=0).astype(BF16)
    vn = jnp.concatenate([vn_ref[:, sl] for sl in heads], axis=0).astype(BF16)
    row = _iota((R, R), 0)
    col = _iota((R, R), 1)
    cnt_n = jnp.where((row >> t_bits) == (col >> t_bits),
                      _branch_count((row & (T - 1)) - (col & (T - 1))), 0.0)

    s_n = jnp.where(cnt_n > 0, _dot_nt(q, kn), -jnp.inf)
    big = jnp.max(s_n, axis=1, keepdims=True)
    for c, sl in enumerate(chunks):
        s = jnp.where(cnt_scr[:, sl] > 0, _dot_nt(q, rows(kfar_ref, knear_ref, c).astype(BF16)), -jnp.inf)
        s_scr[:, sl] = s
        big = jnp.maximum(big, jnp.max(s, axis=1, keepdims=True))

    p_n = cnt_n * jnp.exp(s_n - big)
    den = jnp.sum(p_n, axis=1, keepdims=True)
    num = _dot(p_n.astype(BF16), vn)
    for c, sl in enumerate(chunks):
        p = cnt_scr[:, sl] * jnp.exp(s_scr[:, sl] - big)
        den = den + jnp.sum(p, axis=1, keepdims=True)
        num = num + _dot(p.astype(BF16), rows(vfar_ref, vnear_ref, c).astype(BF16))
    out = num / den
    for h, sl in enumerate(heads):
        o_ref[:, sl] = out[h * T:(h + 1) * T, :]


def _attn_sample(main, cache_k, cache_v, layer, batch, T, chunk=2048):
    depth, _, nb, n_heads, hd = cache_k.shape
    assert n_heads == H_A and hd == HEAD_DIM and T & (T - 1) == 0 and H_A & (H_A - 1) == 0
    d_far = DILATIONS[-1][1]
    near_pos = DILATIONS[-2][0]
    assert all(w <= near_pos for w, _ in DILATIONS[:-1]) and nb % d_far == 0 and T <= d_far
    assert (nb - near_pos) % d_far == 0 and near_pos <= nb
    n_groups = (nb - near_pos) // d_far
    nf = nb * H_A
    n_keys = (n_groups * T + near_pos) * H_A
    assert (n_groups * T * H_A) % chunk == 0 and (near_pos * H_A) % chunk == 0 and nf % (near_pos * H_A) == 0
    far_view = lambda c: c.reshape(depth, batch, nb // d_far, d_far, H_A, HEAD_DIM)
    near_view = lambda c: c.reshape(depth, batch, nf, HEAD_DIM)
    kern = functools.partial(_attn_sample_kernel, T=T, NB=nb, chunk=chunk)
    far_spec = pl.BlockSpec((None, None, n_groups, T, H_A, HEAD_DIM), lambda b: (layer, b, 0, 0, 0, 0))
    near_spec = pl.BlockSpec((None, None, near_pos * H_A, HEAD_DIM),
                             lambda b: (layer, b, nf // (near_pos * H_A) - 1, 0))
    return pl.pallas_call(
        kern,
        grid=(batch,),
        in_specs=[
            pl.BlockSpec((T, W_HEADS), lambda b: (b, 3)),
            pl.BlockSpec((T, W_HEADS), lambda b: (b, 4)),
            pl.BlockSpec((T, W_HEADS), lambda b: (b, 5)),
            far_spec,
            near_spec,
            far_spec,
            near_spec,
        ],
        out_specs=pl.BlockSpec((T, W_HEADS), lambda b: (b, 0)),
        out_shape=jax.ShapeDtypeStruct((batch * T, W_HEADS), F32),
        scratch_shapes=[pltpu.VMEM((H_A * T, n_keys), F32), pltpu.VMEM((H_A * T, n_keys), F32)],
        compiler_params=_cparams("arbitrary"),
        name="attn_sample",
    )(main, main, main, far_view(cache_k), near_view(cache_k), far_view(cache_v), near_view(cache_v))


def _layer_norm(y, g, b):
    mu = jnp.mean(y, axis=1, keepdims=True)
    yc = y - mu
    var = jnp.mean(yc * yc, axis=1, keepdims=True)
    return yc * lax.rsqrt(var + LN_EPS) * g + b


def _merge_kernel(h_ref, om_ref, att_ref, x_ref, gt_ref, gh_ref, w_ref, lg_ref, lb_ref,
                  o_ref, cat_scr, *, alpha, sub):
    for r0 in range(0, cat_scr.shape[0], sub):
        rs = slice(r0, r0 + sub)
        for h in range(H_M):
            sl = slice(h * DV_M, (h + 1) * DV_M)
            hh = h_ref[rs, sl]
            hn = hh * lax.rsqrt(jnp.mean(hh * hh, axis=1, keepdims=True) + HEAD_NORM_EPS) * gh_ref[:, sl]
            cat_scr[rs, sl] = (hn * jax.nn.sigmoid(om_ref[rs, sl].astype(F32))).astype(BF16)
        cat_scr[rs, H_M * DV_M:] = att_ref[rs, :].astype(BF16)
        mix = _dot(cat_scr[rs, :], w_ref[...])
        gt = gt_ref[...] if gt_ref.shape[0] == 1 else gt_ref[rs, :]
        y = alpha * x_ref[rs, :] + (1.0 + gt) * mix
        o_ref[rs, :] = _layer_norm(y, lg_ref[...], lb_ref[...])


def _merge(h, main, att, x, gt, g_head, w_out, ln_g, ln_b, layer, alpha, tm, sub=256):
    rows, d = x.shape
    rm = gt.shape[0]
    mod_block = (1, d) if rm == 1 else (tm, d)
    mod_map = (lambda i: (0, 0)) if rm == 1 else (lambda i: (i, 0))
    wide = pl.BlockSpec((tm, W_HEADS), lambda i: (i, 0))
    const = lambda shape: pl.BlockSpec(shape, lambda i: (0, 0))
    return pl.pallas_call(
        functools.partial(_merge_kernel, alpha=alpha, sub=min(sub, tm)),
        grid=(rows // tm,),
        in_specs=[
            wide,
            pl.BlockSpec((tm, W_HEADS), lambda i: (i, 2)),
            wide,
            pl.BlockSpec((tm, d), lambda i: (i, 0)),
            pl.BlockSpec(mod_block, mod_map),
            const((1, W_HEADS)),
            pl.BlockSpec((None, d, d), lambda i: (layer, 0, 0)),
            const((1, d)),
            const((1, d)),
        ],
        out_specs=pl.BlockSpec((tm, d), lambda i: (i, 0)),
        out_shape=jax.ShapeDtypeStruct((rows, d), F32),
        scratch_shapes=[pltpu.VMEM((tm, d), BF16)],
        compiler_params=_cparams("arbitrary"),
        name="mixer_merge",
    )(h, main, att, x, gt, g_head, w_out, ln_g, ln_b)


def _ffn_kernel(*refs, alpha, seq_len, n_chunks, carry, sub, slab):
    if carry:
        (x_ref, sc_ref, sh_ref, gt_ref, wg_ref, wv_ref, wd_ref, cw_ref, cb_ref, lg_ref, lb_ref,
         o_ref, tail_ref, u_scr, carry_scr) = refs
    else:
        (x_ref, sc_ref, sh_ref, gt_ref, wg_ref, wv_ref, wd_ref, cw_ref, cb_ref, lg_ref, lb_ref,
         h0_ref, h1_ref, o_ref, tail_ref, u_scr) = refs
    i = pl.program_id(0)
    j = pl.program_id(1)
    tm = u_scr.shape[0]
    assert n_chunks >= 2

    def rows_of(ref, rs):
        return ref[...] if ref.shape[0] == 1 else ref[rs, :]

    if carry:
        @pl.when(i == 0)
        def _():
            carry_scr[j] = jnp.zeros(carry_scr.shape[1:], F32)

    def step(first, last):
        cw = cw_ref[...]
        if carry:
            tail = carry_scr[j]
        for r0 in range(0, tm, sub):
            rs = slice(r0, r0 + sub)
            if first:
                ub = (x_ref[rs, :] * (1.0 + rows_of(sc_ref, rs)) + rows_of(sh_ref, rs)).astype(BF16)
                u_scr[rs, :] = ub
            else:
                ub = u_scr[rs, :]
            g = _dot(ub, wg_ref[...])
            v = _dot(ub, wv_ref[...])
            row = _iota(g.shape, 0)
            if carry:
                hist0, hist1 = tail[6:7, :], tail[7:8, :]
                pos = row
                tail = g[sub - 8:, :]
            else:
                hist0, hist1 = h0_ref[rs, :], h1_ref[rs, :]
                pos = row & (seq_len - 1)
                tail_ref[rs, :] = g
            prev1 = jnp.where(pos == 0, hist1, pltpu.roll(g, 1, 0))
            prev2 = jnp.where(pos == 0, hist0, jnp.where(pos == 1, hist1, pltpu.roll(g, 2, 0)))
            a = prev2 * cw[0:1, :] + prev1 * cw[1:2, :] + g * cw[2:3, :] + cb_ref[...]
            a = a * jax.nn.sigmoid(a) * v
            acc = _dot(a.astype(BF16), wd_ref[...])
            if not first:
                acc = o_ref[rs, :] + acc
            if last:
                y = alpha * x_ref[rs, :] + (1.0 + rows_of(gt_ref, rs)) * acc
                acc = _layer_norm(y, lg_ref[...], lb_ref[...])
            o_ref[rs, :] = acc
        if carry:
            carry_scr[j] = tail
            tail_ref[...] = tail

    def step_slabs():
        cw = cw_ref[...]
        cb = cb_ref[...]
        ub = u_scr[...]
        if carry:
            tail = carry_scr[j]
        parts, tails = [], []
        for c0 in range(0, cw.shape[1], slab):
            cs = slice(c0, c0 + slab)
            g = _dot(ub, wg_ref[:, cs])
            v = _dot(ub, wv_ref[:, cs])
            row = _iota(g.shape, 0)
            if carry:
                hist0, hist1 = tail[6:7, cs], tail[7:8, cs]
                pos = row
                tails.append(g[tm - 8:, :])
            else:
                hist0, hist1 = h0_ref[:, cs], h1_ref[:, cs]
                pos = row & (seq_len - 1)
                tail_ref[:, cs] = g
            prev1 = jnp.where(pos == 0, hist1, pltpu.roll(g, 1, 0))
            prev2 = jnp.where(pos == 0, hist0, jnp.where(pos == 1, hist1, pltpu.roll(g, 2, 0)))
            a = prev2 * cw[0:1, cs] + prev1 * cw[1:2, cs] + g * cw[2:3, cs] + cb[:, cs]
            parts.append((a * jax.nn.sigmoid(a) * v).astype(BF16))
        o_ref[...] = o_ref[...] + _dot(jnp.concatenate(parts, axis=1), wd_ref[...])
        if carry:
            tail = jnp.concatenate(tails, axis=1)
            carry_scr[j] = tail
            tail_ref[...] = tail

    middle = step_slabs if tm > sub else functools.partial(step, False, False)
    pl.when(j == 0)(functools.partial(step, True, False))
    pl.when((j > 0) & (j < n_chunks - 1))(middle)
    pl.when(j == n_chunks - 1)(functools.partial(step, False, True))


def _ffn(x, sc, sh, gt, w_up, w_down, conv_w, conv_b, ln_g, ln_b, layer, alpha, tm,
         hist=None, seq_len=None, tf=512, sub=512):
    rows, d = x.shape
    d_ff = w_down.shape[1]
    nj = d_ff // tf
    rm = sc.shape[0]
    mod_block = (1, d) if rm == 1 else (tm, d)
    mod_map = (lambda i, j: (0, 0)) if rm == 1 else (lambda i, j: (i, 0))
    carry = hist is None
    in_specs = [
        pl.BlockSpec((tm, d), lambda i, j: (i, 0), pipeline_mode=pl.Buffered(1)),
        pl.BlockSpec(mod_block, mod_map),
        pl.BlockSpec(mod_block, mod_map),
        pl.BlockSpec(mod_block, mod_map),
        pl.BlockSpec((None, d, tf), lambda i, j: (layer, 0, j)),
        pl.BlockSpec((None, d, tf), lambda i, j: (layer, 0, nj + j)),
        pl.BlockSpec((None, tf, d), lambda i, j: (layer, j, 0)),
        pl.BlockSpec((CONV_W, tf), lambda i, j: (0, j)),
        pl.BlockSpec((1, tf), lambda i, j: (0, j)),
        pl.BlockSpec((1, d), lambda i, j: (0, 0)),
        pl.BlockSpec((1, d), lambda i, j: (0, 0)),
    ]
    args = [x, sc, sh, gt, w_up, w_up, w_down, conv_w, conv_b, ln_g, ln_b]
    scratch = [pltpu.VMEM((tm, d), BF16)]
    if carry:
        tail_rows = 8
        scratch.append(pltpu.VMEM((nj, 8, tf), F32))
    else:
        tail_rows = tm
        in_specs += [pl.BlockSpec((tm, tf), lambda i, j: (i, j))] * 2
        args += list(hist)
    kern = functools.partial(_ffn_kernel, alpha=alpha, seq_len=seq_len, n_chunks=nj, carry=carry,
                             sub=min(sub, tm), slab=MXU_COLS)
    return pl.pallas_call(
        kern,
        grid=(rows // tm, nj),
        in_specs=in_specs,
        out_specs=[
            pl.BlockSpec((tm, d), lambda i, j: (i, 0)),
            pl.BlockSpec((tail_rows, tf), lambda i, j: (i, j)),
        ],
        out_shape=[
            jax.ShapeDtypeStruct((rows, d), F32),
            jax.ShapeDtypeStruct((rows // tm * tail_rows, d_ff), F32),
        ],
        scratch_shapes=scratch,
        compiler_params=_cparams("arbitrary", "arbitrary"),
        name="conv_ffn",
    )(*args)


def _rope_tables(pos):
    half = HEAD_DIM // 2
    inv = ROPE_THETA ** (-jnp.arange(half, dtype=F32) / half)
    ang = pos.astype(F32)[:, None] * inv[None, :]
    cos, sin = jnp.cos(ang), jnp.sin(ang)
    return jnp.concatenate([cos, cos], -1), jnp.concatenate([-sin, sin], -1)


def kernel(x_prompt, x_sample, cache_k_win, cache_v_win, state_C, state_n, state_m, state_conv,
           c_prompt, c_sample, w_ada, b_ada, w_in, b_gate, g_head, w_out, ln1_g, ln1_b,
           w_up, conv_w, conv_b, w_down, ln2_g, ln2_b):
    bp, s, d = x_prompt.shape
    bs, t, _ = x_sample.shape
    depth = w_in.shape[0]
    d_ff = w_down.shape[1]
    alpha = (2 * depth) ** 0.25
    assert bp == 1 and d == (H_M + H_A) * HEAD_DIM and s % ATT_SUPER == 0

    n_c = bp + bs
    pad = (-n_c) % 8
    c_all = jnp.concatenate([c_prompt, c_sample, jnp.zeros((pad, d), F32)], 0)
    mod = _modulation(c_all, w_ada, b_ada)

    n_m = 2 * H_M * DK_M + 2 * H_M * DV_M
    w_m = w_in[:, :, :n_m].astype(BF16)
    w_a = w_in[:, :, n_m + 2 * H_M:].astype(BF16)
    w_gate = jnp.pad(w_in[:, :, n_m:n_m + 2 * H_M], ((0, 0), (0, 0), (0, N_GATE_PAD - 2 * H_M))).astype(BF16)
    gate_bias = jnp.pad(b_gate.reshape(depth, 1, 2 * H_M), ((0, 0), (0, 0), (0, N_GATE_PAD - 2 * H_M)))
    w_out_b = w_out.astype(BF16)
    w_up_b = w_up.astype(BF16)
    w_down_b = w_down.astype(BF16)

    cc_p, ss_p = _rope_tables(jnp.arange(s, dtype=jnp.int32))
    pos_s = PAST_LEN + jnp.arange(t, dtype=jnp.int32)
    cc_s, ss_s = (jnp.tile(a, (bs, 1)) for a in _rope_tables(pos_s))

    xp = x_prompt.reshape(s, d)
    xs = x_sample.reshape(bs * t, d)
    rows_s = bs * t
    tm_p = 1024
    outs =[[] for _ in range(12)]
    zeros_c = jnp.zeros((bp, H_M, DK_M, DV_M), F32)
    zeros_n = jnp.zeros((bp, H_M, DK_M), F32)
    zeros_m = jnp.zeros((bp, H_M), F32)
    row2 = lambda v: v.reshape(1, -1)

    for l in range(depth):
        mod_p = [mod[l, 0:1, k * d:(k + 1) * d] for k in range(6)]
        mod_s = [jnp.repeat(mod[l, bp:bp + bs, k * d:(k + 1) * d], t, axis=0) for k in range(6)]
        ln1 = (row2(ln1_g[l]), row2(ln1_b[l]))
        ffn_small = (conv_w[l], row2(conv_b[l]), row2(ln2_g[l]), row2(ln2_b[l]))

        main_p, kv_p, gates_p = _inproj(xp, mod_p[1], mod_p[0], w_m, w_a, w_gate, gate_bias, cc_p, ss_p, l,
                                        tm=tm_p, main_dtype=BF16)
        h_p, c_p, n_p, m_p = _mlstm(main_p, gates_p, zeros_c, zeros_n, zeros_m, batch=bp, L=128)
        att_p = _attn_prompt(main_p)
        x1_p = _merge(h_p, main_p, att_p, xp, mod_p[2], row2(g_head[l]), w_out_b, *ln1, l, alpha, tm=512)
        xp, tail_p = _ffn(x1_p, mod_p[4], mod_p[3], mod_p[5], w_up_b, w_down_b, *ffn_small, l, alpha, tm=tm_p)

        main_s, kv_s, gates_s = _inproj(xs, mod_s[1], mod_s[0], w_m, w_a, w_gate, gate_bias, cc_s, ss_s, l,
                                        tm=rows_s, main_dtype=F32)
        h_s, c_s, n_s, m_s = _mlstm(main_s, gates_s, state_C, state_n, state_m, batch=bs, L=t, layer=l,
                                    nseq=MLSTM_SEQS_PER_STEP)
        att_s = _attn_sample(main_s, cache_k_win, cache_v_win, l, bs, t)
        x1_s = _merge(h_s, main_s, att_s, xs, mod_s[2], row2(g_head[l]), w_out_b, *ln1, l, alpha, tm=rows_s)
        hist = [jnp.repeat(state_conv[l][:, r, :], t, axis=0) for r in range(CONV_W - 1)]
        xs, g_s = _ffn(x1_s, mod_s[4], mod_s[3], mod_s[5], w_up_b, w_down_b, *ffn_small, l, alpha, tm=rows_s,
                       hist=hist, seq_len=t)

        wp = min(DILATIONS[-1][0], s)
        k_cols = slice(0, W_HEADS)
        v_cols = slice(W_HEADS, 2 * W_HEADS)
        outs[0].append(kv_p[s - wp:, k_cols].reshape(bp, wp, H_A, HEAD_DIM))
        outs[1].append(kv_p[s - wp:, v_cols].reshape(bp, wp, H_A, HEAD_DIM))
        outs[2].append(c_p)
        outs[3].append(n_p)
        outs[4].append(m_p.reshape(bp, H_M))
        outs[5].append(tail_p[-8:][8 - (CONV_W - 1):].reshape(bp, CONV_W - 1, d_ff))
        outs[6].append(kv_s[:, k_cols].reshape(bs, t, H_A, HEAD_DIM))
        outs[7].append(kv_s[:, v_cols].reshape(bs, t, H_A, HEAD_DIM))
        outs[8].append(c_s)
        outs[9].append(n_s)
        outs[10].append(m_s.reshape(bs, H_M))
        outs[11].append(g_s.reshape(bs, t, d_ff)[:, t - (CONV_W - 1):])

    return (xp.reshape(bp, s, d), xs.reshape(bs, t, d)) + tuple(jnp.stack(o) for o in outs)
```

```python
import functools

import jax
import jax.numpy as jnp
from jax import lax
from jax.experimental import pallas as pl
from jax.experimental.pallas import tpu as pltpu

F32 = jnp.float32
BF16 = jnp.bfloat16
HIGHEST = lax.Precision.HIGHEST

HEAD_DIM = 128
H_M = 8
H_A = 8
DK_M = 64
DV_M = 128
DILATIONS = ((128, 1), (512, 4), (2048, 16))
ATT_BLOCK = 128
PAST_LEN = 8192
ROPE_THETA = 10000.0
CONV_W = 3
LN_EPS = 1e-5
HEAD_NORM_EPS = 1e-6
N_GATE_PAD = 128
W_HEADS = H_A * HEAD_DIM
ATT_SUPER = ATT_BLOCK * DILATIONS[-1][1]
MLSTM_SEQS_PER_STEP = 4
VMEM_LIMIT_BYTES = 56 * 1024 * 1024


def _cparams(*sem):
    return pltpu.CompilerParams(dimension_semantics=sem, vmem_limit_bytes=VMEM_LIMIT_BYTES)


def _dot(a, b):
    return jnp.dot(a, b, preferred_element_type=F32)


def _dot_nt(a, b, precision=None):
    return lax.dot_general(a, b, (((1,), (1,)), ((), ())), precision=precision,
                           preferred_element_type=F32)


def _dot_tn(a, b):
    return lax.dot_general(a, b, (((0,), (0,)), ((), ())), preferred_element_type=F32)


def _iota(shape, dim):
    return lax.broadcasted_iota(jnp.int32, shape, dim)


def _mod_kernel(c_ref, w_ref, b_ref, o_ref):
    c = c_ref[...]
    a = (c * jax.nn.sigmoid(c)).astype(BF16)
    o_ref[...] = _dot(a, w_ref[...].astype(BF16)) + b_ref[...]


def _modulation(c_all, w_ada, b_ada, tn=1024):
    depth, d, n = w_ada.shape
    rows = c_all.shape[0]
    return pl.pallas_call(
        _mod_kernel,
        grid=(depth, n // tn),
        in_specs=[
            pl.BlockSpec((rows, d), lambda l, j: (0, 0)),
            pl.BlockSpec((None, d, tn), lambda l, j: (l, 0, j)),
            pl.BlockSpec((None, 1, tn), lambda l, j: (l, 0, j)),
        ],
        out_specs=pl.BlockSpec((None, rows, tn), lambda l, j: (l, 0, j)),
        out_shape=jax.ShapeDtypeStruct((depth, rows, n), F32),
        compiler_params=_cparams("arbitrary", "arbitrary"),
        name="adaln_mod",
    )(c_all, w_ada, b_ada.reshape(depth, 1, n))


def _inproj_kernel(x_ref, sc_ref, sh_ref, wm_ref, wa_ref, wg_ref, gb_ref, cc_ref, ss_ref,
                   main_ref, kv_ref, gate_ref, u_scr, *, tn, q_lo, k_lo, v_lo, sub):
    j = pl.program_id(1)
    tm = u_scr.shape[0]

    def rows_of(ref, rs):
        return ref[...] if ref.shape[0] == 1 else ref[rs, :]

    def finish(kind, acc, rs, c0):
        width = acc.shape[1]
        cols = slice(c0, c0 + width)
        if kind == "plain":
            main_ref[rs, cols] = acc.astype(main_ref.dtype)
        elif kind == "v":
            main_ref[rs, cols] = acc.astype(main_ref.dtype)
            kv_ref[rs, cols] = acc
        else:
            for g0 in range(0, width, HEAD_DIM):
                a = acc[:, g0:g0 + HEAD_DIM]
                sl = slice(c0 + g0, c0 + g0 + HEAD_DIM)
                y = a * cc_ref[rs, :] + pltpu.roll(a, HEAD_DIM // 2, 1) * ss_ref[rs, :]
                if kind == "q":
                    main_ref[rs, sl] = (y * (HEAD_DIM ** -0.5)).astype(main_ref.dtype)
                else:
                    main_ref[rs, sl] = y.astype(main_ref.dtype)
                    kv_ref[rs, sl] = y

    def step(kind, first):
        w_ref = wm_ref if kind == "plain" else wa_ref
        for r0 in range(0, tm, sub):
            rs = slice(r0, r0 + sub)
            if first:
                ub = (x_ref[rs, :] * (1.0 + rows_of(sc_ref, rs)) + rows_of(sh_ref, rs)).astype(BF16)
                u_scr[rs, :] = ub
                z = _dot(ub, wg_ref[...]) + gb_ref[...]
                lane = _iota(z.shape, 1)
                log_sig = jnp.minimum(z, 0.0) - jnp.log1p(jnp.exp(-jnp.abs(z)))
                gate_ref[rs, :] = jnp.where((lane >= H_M) & (lane < 2 * H_M), log_sig, z)
            else:
                ub = u_scr[rs, :]
            finish(kind, _dot(ub, w_ref[...]), rs, 0)

    pl.when(j == 0)(functools.partial(step, "plain", True))
    pl.when((j > 0) & (j < q_lo))(functools.partial(step, "plain", False))
    pl.when((j >= q_lo) & (j < k_lo))(functools.partial(step, "q", False))
    pl.when((j >= k_lo) & (j < v_lo))(functools.partial(step, "k", False))
    pl.when(j >= v_lo)(functools.partial(step, "v", False))


def _inproj(x, sc, sh, w_m, w_a, w_gate, gate_bias, rope_cc, rope_ss, layer, tm, main_dtype, tn=1024, sub=512):
    rows, d = x.shape
    n = w_m.shape[2] + w_a.shape[2]
    rm = sc.shape[0]
    mod_block = (1, d) if rm == 1 else (tm, d)
    mod_map = (lambda i, j: (0, 0)) if rm == 1 else (lambda i, j: (i, 0))
    q_lo = (3 * W_HEADS) // tn
    k_lo = (4 * W_HEADS) // tn
    v_lo = (5 * W_HEADS) // tn
    kern = functools.partial(_inproj_kernel, tn=tn, q_lo=q_lo, k_lo=k_lo, v_lo=v_lo, sub=min(sub, tm))
    return pl.pallas_call(
        kern,
        grid=(rows // tm, n // tn),
        in_specs=[
            pl.BlockSpec((tm, d), lambda i, j: (i, 0)),
            pl.BlockSpec(mod_block, mod_map),
            pl.BlockSpec(mod_block, mod_map),
            pl.BlockSpec((None, d, tn), lambda i, j: (layer, 0, jnp.minimum(j, q_lo - 1))),
            pl.BlockSpec((None, d, tn), lambda i, j: (layer, 0, jnp.maximum(j - q_lo, 0))),
            pl.BlockSpec((None, d, N_GATE_PAD), lambda i, j: (layer, 0, 0)),
            pl.BlockSpec((None, 1, N_GATE_PAD), lambda i, j: (layer, 0, 0)),
            pl.BlockSpec((tm, HEAD_DIM), lambda i, j: (i, 0)),
            pl.BlockSpec((tm, HEAD_DIM), lambda i, j: (i, 0)),
        ],
        out_specs=[
            pl.BlockSpec((tm, tn), lambda i, j: (i, j)),
            pl.BlockSpec((tm, tn), lambda i, j: (i, jnp.maximum(j - k_lo, 0))),
            pl.BlockSpec((tm, N_GATE_PAD), lambda i, j: (i, 0)),
        ],
        out_shape=[
            jax.ShapeDtypeStruct((rows, n), main_dtype),
            jax.ShapeDtypeStruct((rows, 2 * W_HEADS), F32),
            jax.ShapeDtypeStruct((rows, N_GATE_PAD), F32),
        ],
        scratch_shapes=[pltpu.VMEM((tm, d), BF16)],
        compiler_params=_cparams("arbitrary", "arbitrary"),
        name="inproj",
    )(x, sc, sh, w_m, w_a, w_gate, gate_bias, rope_cc, rope_ss)


def _mlstm_kernel(q_ref, k_ref, v_ref, g_ref, c0_ref, n0_ref, m0_ref,
                  h_ref, c_ref, n_ref, m_ref, *, L, nseq):
    @pl.when(pl.program_id(1) == 0)
    def _():
        c_ref[...] = c0_ref[...]
        n_ref[...] = n0_ref[...]
        m_ref[...] = m0_ref[...]

    row = _iota((L, L), 0)
    col = _iota((L, L), 1)
    causal = row >= col
    tril = causal.astype(F32)
    triu = (row <= col).astype(F32)
    eye = (_iota((2 * H_M, N_GATE_PAD), 0) == _iota((2 * H_M, N_GATE_PAD), 1)).astype(F32)
    hi_dot = functools.partial(jnp.dot, precision=HIGHEST, preferred_element_type=F32)

    seqs = []
    for bi in range(nseq):
        gates = g_ref[bi * L:(bi + 1) * L, :]
        gates_t = _dot_nt(eye, gates, precision=HIGHEST)
        seqs.append((gates, gates_t, hi_dot(tril, gates), hi_dot(gates_t, triu)))

    units = []
    for bi, (gates, gates_t, bcol_all, brow_all) in enumerate(seqs):
        rows = slice(bi * L, (bi + 1) * L)
        for h in range(H_M):
            b_col = bcol_all[:, H_M + h:H_M + h + 1]
            b_row = brow_all[H_M + h:H_M + h + 1, :]
            m0 = m_ref[bi, :, h:h + 1]
            dmat = jnp.where(causal, b_col - b_row + gates_t[h:h + 1, :], -jnp.inf)
            m_inter = b_col + m0
            m = jnp.maximum(m_inter, jnp.max(dmat, axis=1, keepdims=True))
            q = q_ref[rows, h * DK_M:(h + 1) * DK_M].astype(F32)
            k = k_ref[rows, h * DK_M:(h + 1) * DK_M].astype(F32) * (DK_M ** -0.5)
            qb = q.astype(BF16)
            units.append(dict(
                bi=bi, h=h, rows=rows, m0=m0, m=m, w=jnp.exp(dmat - m), g=jnp.exp(m_inter - m),
                q=q, k=k, qb=qb, vb=v_ref[rows, h * DV_M:(h + 1) * DV_M].astype(BF16),
                s_raw=_dot_nt(qb, k.astype(BF16)), b_col=b_col, ig_col=gates[:, h:h + 1]))

    for u in units:
        bi, h = u["bi"], u["h"]
        s = u["s_raw"] * u["w"]
        num = u["g"] * _dot(u["qb"], c_ref[bi, h].astype(BF16)) + _dot(s.astype(BF16), u["vb"])
        den = (u["g"] * jnp.sum(u["q"] * n_ref[bi, h:h + 1, :], axis=1, keepdims=True)
               + jnp.sum(s, axis=1, keepdims=True))
        h_ref[u["rows"], h * DV_M:(h + 1) * DV_M] = num / jnp.maximum(jnp.abs(den), jnp.exp(-u["m"]))

    for u in units:
        bi, h, b_col, m0 = u["bi"], u["h"], u["b_col"], u["m0"]
        b_last = b_col[L - 1:L, :]
        a_col = b_last - b_col + u["ig_col"]
        m_end = jnp.maximum(b_last + m0, jnp.max(a_col, axis=0, keepdims=True))
        g_end = jnp.exp(b_last + m0 - m_end)
        kw = jnp.exp(a_col - m_end) * u["k"]
        c_ref[bi, h] = g_end * c_ref[bi, h] + _dot_tn(kw.astype(BF16), u["vb"])
        n_ref[bi, h:h + 1, :] = g_end * n_ref[bi, h:h + 1, :] + jnp.sum(kw, axis=0, keepdims=True)
        m_ref[bi, :, h:h + 1] = m_end


def _mlstm_wide_kernel(q_ref, k_ref, v_ref, g_ref, c0_ref, n0_ref, m0_ref,
                       h_ref, c_ref, n_ref, m_ref, nmat_scr):
    L = HEAD_DIM
    first = pl.program_id(1) == 0
    sub8 = _iota((H_M, L), 0)

    @pl.when(first)
    def _():
        c_ref[...] = c0_ref[...]
        n_ref[...] = n0_ref[...]
        m_ref[...] = m0_ref[...]
        for h in range(H_M):
            nmat_scr[h] = lax.dot_general(n0_ref[...], (sub8 == h).astype(F32), (((0,), (0,)), ((), ())),
                                          precision=HIGHEST, preferred_element_type=F32)

    gates = g_ref[...]
    row = _iota((L, L), 0)
    col = _iota((L, L), 1)
    causal = row >= col
    triu = (row <= col).astype(F32)
    eye = (_iota((2 * H_M, N_GATE_PAD), 0) == _iota((2 * H_M, N_GATE_PAD), 1)).astype(F32)
    gates_t = _dot_nt(eye, gates, precision=HIGHEST)
    ig = gates_t[:H_M]
    b = jnp.dot(gates_t[H_M:], triu, precision=HIGHEST, preferred_element_type=F32)
    a = ig - b
    cm = a
    lane = _iota((H_M, L), 1)
    shift = 1
    while shift < L:
        cm = jnp.maximum(cm, jnp.where(lane >= shift, pltpu.roll(cm, shift, 1), -jnp.inf))
        shift *= 2

    eye8 = _iota((H_M, H_M), 0) == _iota((H_M, H_M), 1)
    m0_col = jnp.sum(jnp.where(eye8, jnp.broadcast_to(m_ref[...], (H_M, H_M)), 0.0), axis=1, keepdims=True)
    mm = jnp.maximum(m0_col, cm)
    mm_last = mm[:, L - 1:L]
    ws_all = jnp.exp(a - mm_last)
    g_end_all = jnp.exp(m0_col - mm_last)
    m_end = b[:, L - 1:L] + mm_last
    m_ref[...] = jnp.sum(jnp.where(eye8, jnp.broadcast_to(m_end, (H_M, H_M)), 0.0), axis=0, keepdims=True)
    def split(x, parts):
        out = []
        for _ in range(parts):
            p = x.astype(BF16)
            out.append(p)
            x = x - p.astype(F32)
        return out

    heads = range(H_M)
    ones_b = jnp.ones((L, L), BF16)

    cols = jnp.concatenate([-mm, -mm - b], axis=0).T
    n_sel = 3 * 2 * H_M
    sel_r = _iota((n_sel, H_M * 2 * L), 0) & (2 * H_M - 1)
    sel_c = _iota((n_sel, H_M * 2 * L), 1)
    sel_h = sel_c >> ((2 * L).bit_length() - 1)
    left = (sel_c & (2 * L - 1)) < L
    sel = (left & (sel_r == sel_h)) | (jnp.logical_not(left) & (sel_r == H_M + sel_h))
    bc = _dot(jnp.concatenate(split(cols, 3), axis=1), sel.astype(F32).astype(BF16))

    qb = [q_ref[:, h * DK_M:(h + 1) * DK_M].astype(BF16) for h in heads]
    kt = [(k_ref[:, h * DK_M:(h + 1) * DK_M].astype(F32) * (DK_M ** -0.5)).T for h in heads]
    vo = [jnp.concatenate([v_ref[:, h * DV_M:(h + 1) * DV_M].astype(BF16), ones_b], axis=1) for h in heads]
    s_raw = [_dot(qb[h], kt[h].astype(BF16)) for h in heads]
    inter = [_dot(qb[h], jnp.concatenate([c_ref[h].astype(BF16), nmat_scr[h].astype(BF16)], axis=1))
             for h in heads]

    for h in heads:
        neg_mm = bc[:, 2 * L * h:2 * L * h + L]
        neg_m = bc[:, 2 * L * h + L:2 * L * (h + 1)]
        w = jnp.where(causal, jnp.exp(neg_mm + a[h:h + 1, :]), 0.0)
        gmat = jnp.exp(neg_mm + m0_col[h:h + 1, :])
        s_hi, s_lo = split(s_raw[h] * w, 2)
        r = _dot(s_hi, vo[h])
        num = gmat * inter[h][:, :L] + r[:, :L]
        den = gmat * inter[h][:, L:] + r[:, L:] + _dot(s_lo, ones_b)
        h_ref[:, h * DV_M:(h + 1) * DV_M] = num / jnp.maximum(jnp.abs(den), jnp.exp(neg_m))

    for h in heads:
        g_end = g_end_all[h:h + 1, :]
        kw_hi, kw_lo = split(kt[h] * ws_all[h:h + 1, :], 2)
        u = _dot(kw_hi, vo[h])
        c_ref[h] = g_end * c_ref[h] + u[:, :L]
        nmat_scr[h] = g_end * nmat_scr[h] + u[:, L:] + _dot(kw_lo, ones_b)

    @pl.when(pl.program_id(1) == pl.num_programs(1) - 1)
    def _():
        for h in heads:
            n_ref[h:h + 1, :] = nmat_scr[h].T[0:1, :]


def _mlstm(main, gates, c0, n0, m0, batch, L, layer=None, nseq=1):
    rows = main.shape[0]
    nc = rows // (batch * L)
    if L == HEAD_DIM:
        assert nseq == 1
        kern, scratch, lead = _mlstm_wide_kernel, [pltpu.VMEM((H_M, DK_M, HEAD_DIM), F32)], None
    else:
        assert nc == 1 and batch % nseq == 0
        kern, scratch, lead = functools.partial(_mlstm_kernel, L=L, nseq=nseq), [], nseq
    m0 = m0.reshape(m0.shape[:-1] + (1, H_M))

    def state_spec(tail, stacked):
        zeros = (0,) * len(tail)
        if stacked:
            return pl.BlockSpec((None, lead) + tail, lambda b, c: (layer, b) + zeros)
        return pl.BlockSpec((lead,) + tail, lambda b, c: (b,) + zeros)

    tails = ((H_M, DK_M, DV_M), (H_M, DK_M), (1, H_M))
    blk = nseq * L
    return pl.pallas_call(
        kern,
        grid=(batch // nseq, nc),
        in_specs=[
            pl.BlockSpec((blk, H_M * DK_M), lambda b, c: (b * nc + c, 0)),
            pl.BlockSpec((blk, H_M * DK_M), lambda b, c: (b * nc + c, 1)),
            pl.BlockSpec((blk, H_M * DV_M), lambda b, c: (b * nc + c, 1)),
            pl.BlockSpec((blk, N_GATE_PAD), lambda b, c: (b * nc + c, 0)),
        ] + [state_spec(t, layer is not None) for t in tails],
        out_specs=[pl.BlockSpec((blk, H_M * DV_M), lambda b, c: (b * nc + c, 0))]
        + [state_spec(t, False) for t in tails],
        out_shape=[
            jax.ShapeDtypeStruct((rows, H_M * DV_M), F32),
            jax.ShapeDtypeStruct((batch, H_M, DK_M, DV_M), F32),
            jax.ShapeDtypeStruct((batch, H_M, DK_M), F32),
            jax.ShapeDtypeStruct((batch, 1, H_M), F32),
        ],
        scratch_shapes=scratch,
        compiler_params=_cparams("arbitrary", "arbitrary"),
        name="mlstm",
    )(main, main, main, gates, c0, n0, m0)


def _attn_prompt_kernel(q_ref, kp_ref, kc_ref, vp_ref, vc_ref, o_ref,
                        qq_scr, kk_scr, vv_scr, num_scr, m_scr, den_scr):
    sb = pl.program_id(0)
    SB = ATT_SUPER
    qq_scr[...] = q_ref[...].astype(F32)
    kk_scr[0:SB, :] = kp_ref[...].astype(F32)
    kk_scr[SB:, :] = kc_ref[...].astype(F32)
    vv_scr[0:SB, :] = vp_ref[...].astype(F32)
    vv_scr[SB:, :] = vc_ref[...].astype(F32)

    shape = (ATT_BLOCK, 2 * ATT_BLOCK)
    qi = _iota(shape, 0)
    ki = _iota(shape, 1)
    window = (ki >= qi) & (ki <= qi + ATT_BLOCK)
    window_first = window & ((ki >= ATT_BLOCK) | (sb > 0))
    ones_b = jnp.ones((2 * ATT_BLOCK, HEAD_DIM), BF16)

    for bi, (_, dil) in enumerate(DILATIONS):
        for n in range(SB // (ATT_BLOCK * dil)):
            valid = window_first if n == 0 else window

            for r in range(dil):
                q_rows = pl.ds(r + n * ATT_BLOCK * dil, ATT_BLOCK, stride=dil)
                k_rows = pl.ds(SB + r + (n - 1) * ATT_BLOCK * dil, 2 * ATT_BLOCK, stride=dil)
                q = qq_scr[q_rows, :].astype(BF16)
                k = kk_scr[k_rows, :].astype(BF16)
                v = vv_scr[k_rows, :].astype(BF16)
                s = jnp.where(valid, _dot_nt(q, k), -jnp.inf)
                m = jnp.max(s, axis=1, keepdims=True)
                p = jnp.exp(s - m).astype(BF16)
                r = _dot(p, jnp.concatenate([v, ones_b], axis=1))
                num_scr[bi, q_rows, :] = r[:, :HEAD_DIM]
                den_scr[bi, q_rows, :] = r[:, HEAD_DIM:]
                m_scr[bi, q_rows, :] = jnp.broadcast_to(m, (ATT_BLOCK, HEAD_DIM))

    rows_per_step = 2 * ATT_BLOCK

    def mix(i, carry):
        rows = pl.ds(pl.multiple_of(i * rows_per_step, rows_per_step), rows_per_step)
        ms = [m_scr[b, rows, :] for b in range(len(DILATIONS))]
        big = ms[0]
        for mm in ms[1:]:
            big = jnp.maximum(big, mm)
        w0 = jnp.exp(ms[0] - big)
        acc_num = w0 * num_scr[0, rows, :]
        acc_den = w0 * den_scr[0, rows, :]
        for b in range(1, len(DILATIONS)):
            w = jnp.exp(ms[b] - big)
            acc_num = acc_num + w * num_scr[b, rows, :]
            acc_den = acc_den + w * den_scr[b, rows, :]
        o_ref[rows, :] = acc_num / acc_den
        return carry

    lax.fori_loop(0, SB // rows_per_step, mix, 0)


def _attn_prompt(main):
    s, n_main = main.shape
    SB = ATT_SUPER
    cols = n_main // HEAD_DIM // 6
    blk = (SB, HEAD_DIM)
    prev = lambda i: jnp.maximum(i - 1, 0)
    nbr = len(DILATIONS)
    return pl.pallas_call(
        _attn_prompt_kernel,
        grid=(s // SB, H_A),
        in_specs=[
            pl.BlockSpec(blk, lambda i, h: (i, 3 * cols + h)),
            pl.BlockSpec(blk, lambda i, h: (prev(i), 4 * cols + h)),
            pl.BlockSpec(blk, lambda i, h: (i, 4 * cols + h)),
            pl.BlockSpec(blk, lambda i, h: (prev(i), 5 * cols + h)),
            pl.BlockSpec(blk, lambda i, h: (i, 5 * cols + h)),
        ],
        out_specs=pl.BlockSpec(blk, lambda i, h: (i, h)),
        out_shape=jax.ShapeDtypeStruct((s, W_HEADS), F32),
        scratch_shapes=[
            pltpu.VMEM((SB, HEAD_DIM), F32),
            pltpu.VMEM((2 * SB, HEAD_DIM), F32),
            pltpu.VMEM((2 * SB, HEAD_DIM), F32),
            pltpu.VMEM((nbr, SB, HEAD_DIM), F32),
            pltpu.VMEM((nbr, SB, HEAD_DIM), F32),
            pltpu.VMEM((nbr, SB, HEAD_DIM), F32),
        ],
        compiler_params=_cparams("arbitrary", "arbitrary"),
        name="attn_prompt",
    )(main, main, main, main, main)


def _branch_count(delta):
    cnt = jnp.zeros(delta.shape, F32)
    for window, dil in DILATIONS:
        hit = (delta >= 0) & (delta <= window) & ((delta & (dil - 1)) == 0)
        cnt = cnt + hit.astype(F32)
    return cnt


def _attn_sample_kernel(q_ref, kn_ref, vn_ref, kfar_ref, knear_ref, vfar_ref, vnear_ref, o_ref,
                        cnt_scr, s_scr, *, T, NB, chunk):
    R = H_A * T
    t_bits = T.bit_length() - 1
    h_bits = H_A.bit_length() - 1
    d_far = DILATIONS[-1][1]
    n_groups, keep = kfar_ref.shape[0], kfar_ref.shape[1]
    keep_bits = keep.bit_length() - 1
    near_pos = knear_ref.shape[0] // H_A
    gpc = chunk // (keep * H_A)
    far_chunks = n_groups // gpc
    n_chunks = far_chunks + near_pos * H_A // chunk
    chunks = [slice(c * chunk, (c + 1) * chunk) for c in range(n_chunks)]

    def rows(far_ref, near_ref, c):
        if c < far_chunks:
            return far_ref[c * gpc:(c + 1) * gpc].reshape(chunk, HEAD_DIM)
        return near_ref[chunks[c - far_chunks], :]

    @pl.when(pl.program_id(0) == 0)
    def _():
        for c, sl in enumerate(chunks):
            row = _iota((R, chunk), 0)
            col = _iota((R, chunk), 1)
            if c < far_chunks:
                pos = (c * gpc + (col >> (h_bits + keep_bits))) * d_far + ((col >> h_bits) & (keep - 1))
            else:
                pos = NB - near_pos + (c - far_chunks) * (chunk // H_A) + (col >> h_bits)
            same_head = (row >> t_bits) == (col & (H_A - 1))
            cnt_scr[:, sl] = jnp.where(same_head, _branch_count(NB + (row & (T - 1)) - pos), 0.0)

    heads = [slice(h * HEAD_DIM, (h + 1) * HEAD_DIM) for h in range(H_A)]
    q = jnp.concatenate([q_ref[:, sl] for sl in heads], axis=0).astype(BF16)
    kn = jnp.concatenate([kn_ref[:, sl] for sl in heads], axis=0).astype(BF16)
    vn = jnp.concatenate([vn_ref[:, sl] for sl in heads], axis=0).astype(BF16)
    row = _iota((R, R), 0)
    col = _iota((R, R), 1)
    cnt_n = jnp.where((row >> t_bits) == (col >> t_bits),
                      _branch_count((row & (T - 1)) - (col & (T - 1))), 0.0)

    s_n = jnp.where(cnt_n > 0, _dot_nt(q, kn), -jnp.inf)
    big = jnp.max(s_n, axis=1, keepdims=True)
    for c, sl in enumerate(chunks):
        s = jnp.where(cnt_scr[:, sl] > 0, _dot_nt(q, rows(kfar_ref, knear_ref, c).astype(BF16)), -jnp.inf)
        s_scr[:, sl] = s
        big = jnp.maximum(big, jnp.max(s, axis=1, keepdims=True))

    p_n = cnt_n * jnp.exp(s_n - big)
    den = jnp.sum(p_n, axis=1, keepdims=True)
    num = _dot(p_n.astype(BF16), vn)
    for c, sl in enumerate(chunks):
        p = cnt_scr[:, sl] * jnp.exp(s_scr[:, sl] - big)
        den = den + jnp.sum(p, axis=1, keepdims=True)
        num = num + _dot(p.astype(BF16), rows(vfar_ref, vnear_ref, c).astype(BF16))
    out = num / den
    for h, sl in enumerate(heads):
        o_ref[:, sl] = out[h * T:(h + 1) * T, :]


def _attn_sample(main, cache_k, cache_v, layer, batch, T, chunk=2048):
    depth, _, nb, n_heads, hd = cache_k.shape
    assert n_heads == H_A and hd == HEAD_DIM and T & (T - 1) == 0 and H_A & (H_A - 1) == 0
    d_far = DILATIONS[-1][1]
    near_pos = DILATIONS[-2][0]
    assert all(w <= near_pos for w, _ in DILATIONS[:-1]) and nb % d_far == 0 and T <= d_far
    assert (nb - near_pos) % d_far == 0 and near_pos <= nb
    n_groups = (nb - near_pos) // d_far
    nf = nb * H_A
    n_keys = (n_groups * T + near_pos) * H_A
    assert (n_groups * T * H_A) % chunk == 0 and (near_pos * H_A) % chunk == 0 and nf % (near_pos * H_A) == 0
    far_view = lambda c: c.reshape(depth, batch, nb // d_far, d_far, H_A, HEAD_DIM)
    near_view = lambda c: c.reshape(depth, batch, nf, HEAD_DIM)
    kern = functools.partial(_attn_sample_kernel, T=T, NB=nb, chunk=chunk)
    far_spec = pl.BlockSpec((None, None, n_groups, T, H_A, HEAD_DIM), lambda b: (layer, b, 0, 0, 0, 0))
    near_spec = pl.BlockSpec((None, None, near_pos * H_A, HEAD_DIM),
                             lambda b: (layer, b, nf // (near_pos * H_A) - 1, 0))
    return pl.pallas_call(
        kern,
        grid=(batch,),
        in_specs=[
            pl.BlockSpec((T, W_HEADS), lambda b: (b, 3)),
            pl.BlockSpec((T, W_HEADS), lambda b: (b, 4)),
            pl.BlockSpec((T, W_HEADS), lambda b: (b, 5)),
            far_spec,
            near_spec,
            far_spec,
            near_spec,
        ],
        out_specs=pl.BlockSpec((T, W_HEADS), lambda b: (b, 0)),
        out_shape=jax.ShapeDtypeStruct((batch * T, W_HEADS), F32),
        scratch_shapes=[pltpu.VMEM((H_A * T, n_keys), F32), pltpu.VMEM((H_A * T, n_keys), F32)],
        compiler_params=_cparams("arbitrary"),
        name="attn_sample",
    )(main, main, main, far_view(cache_k), near_view(cache_k), far_view(cache_v), near_view(cache_v))


def _layer_norm(y, g, b):
    mu = jnp.mean(y, axis=1, keepdims=True)
    yc = y - mu
    var = jnp.mean(yc * yc, axis=1, keepdims=True)
    return yc * lax.rsqrt(var + LN_EPS) * g + b


def _merge_kernel(h_ref, om_ref, att_ref, x_ref, gt_ref, gh_ref, w_ref, lg_ref, lb_ref,
                  o_ref, cat_scr, *, alpha, sub):
    for r0 in range(0, cat_scr.shape[0], sub):
        rs = slice(r0, r0 + sub)
        for h in range(H_M):
            sl = slice(h * DV_M, (h + 1) * DV_M)
            hh = h_ref[rs, sl]
            hn = hh * lax.rsqrt(jnp.mean(hh * hh, axis=1, keepdims=True) + HEAD_NORM_EPS) * gh_ref[:, sl]
            cat_scr[rs, sl] = (hn * jax.nn.sigmoid(om_ref[rs, sl].astype(F32))).astype(BF16)
        cat_scr[rs, H_M * DV_M:] = att_ref[rs, :].astype(BF16)
        mix = _dot(cat_scr[rs, :], w_ref[...])
        gt = gt_ref[...] if gt_ref.shape[0] == 1 else gt_ref[rs, :]
        y = alpha * x_ref[rs, :] + (1.0 + gt) * mix
        o_ref[rs, :] = _layer_norm(y, lg_ref[...], lb_ref[...])


def _merge(h, main, att, x, gt, g_head, w_out, ln_g, ln_b, layer, alpha, tm, sub=256):
    rows, d = x.shape
    rm = gt.shape[0]
    mod_block = (1, d) if rm == 1 else (tm, d)
    mod_map = (lambda i: (0, 0)) if rm == 1 else (lambda i: (i, 0))
    wide = pl.BlockSpec((tm, W_HEADS), lambda i: (i, 0))
    const = lambda shape: pl.BlockSpec(shape, lambda i: (0, 0))
    return pl.pallas_call(
        functools.partial(_merge_kernel, alpha=alpha, sub=min(sub, tm)),
        grid=(rows // tm,),
        in_specs=[
            wide,
            pl.BlockSpec((tm, W_HEADS), lambda i: (i, 2)),
            wide,
            pl.BlockSpec((tm, d), lambda i: (i, 0)),
            pl.BlockSpec(mod_block, mod_map),
            const((1, W_HEADS)),
            pl.BlockSpec((None, d, d), lambda i: (layer, 0, 0)),
            const((1, d)),
            const((1, d)),
        ],
        out_specs=pl.BlockSpec((tm, d), lambda i: (i, 0)),
        out_shape=jax.ShapeDtypeStruct((rows, d), F32),
        scratch_shapes=[pltpu.VMEM((tm, d), BF16)],
        compiler_params=_cparams("arbitrary"),
        name="mixer_merge",
    )(h, main, att, x, gt, g_head, w_out, ln_g, ln_b)


def _ffn_kernel(*refs, alpha, seq_len, n_chunks, carry, sub):
    if carry:
        (x_ref, sc_ref, sh_ref, gt_ref, wg_ref, wv_ref, wd_ref, cw_ref, cb_ref, lg_ref, lb_ref,
         o_ref, tail_ref, u_scr, carry_scr) = refs
        w_copies = None
    else:
        (x_ref, sc_ref, sh_ref, gt_ref, wg_ref, wv_ref, wd_ref, cw_ref, cb_ref, lg_ref, lb_ref,
         h0_ref, h1_ref, o_ref, tail_ref, wgb_ref, wvb_ref, wdb_ref, u_scr) = refs
        w_copies = (wgb_ref, wvb_ref, wdb_ref)
    i = pl.program_id(0)
    j = pl.program_id(1)
    tm = u_scr.shape[0]
    assert n_chunks >= 2

    def rows_of(ref, rs):
        return ref[...] if ref.shape[0] == 1 else ref[rs, :]

    if carry:
        @pl.when(i == 0)
        def _():
            carry_scr[j] = jnp.zeros(carry_scr.shape[1:], F32)

    def step(first, last):
        cw = cw_ref[...]
        if carry:
            tail = carry_scr[j]
        if w_copies is not None:
            for copy_ref, w_ref in zip(w_copies, (wg_ref, wv_ref, wd_ref)):
                copy_ref[...] = w_ref[...].astype(BF16)
            wg, wv, wd = w_copies
        else:
            wg, wv, wd = wg_ref, wv_ref, wd_ref
        for r0 in range(0, tm, sub):
            rs = slice(r0, r0 + sub)
            if first:
                ub = (x_ref[rs, :] * (1.0 + rows_of(sc_ref, rs)) + rows_of(sh_ref, rs)).astype(BF16)
                u_scr[rs, :] = ub
            else:
                ub = u_scr[rs, :]
            g = _dot(ub, wg[...])
            v = _dot(ub, wv[...])
            row = _iota(g.shape, 0)
            if carry:
                hist0, hist1 = tail[6:7, :], tail[7:8, :]
                pos = row
                tail = g[sub - 8:, :]
            else:
                hist0, hist1 = h0_ref[rs, :], h1_ref[rs, :]
                pos = row & (seq_len - 1)
                tail_ref[rs, :] = g
            prev1 = jnp.where(pos == 0, hist1, pltpu.roll(g, 1, 0))
            prev2 = jnp.where(pos == 0, hist0, jnp.where(pos == 1, hist1, pltpu.roll(g, 2, 0)))
            a = prev2 * cw[0:1, :] + prev1 * cw[1:2, :] + g * cw[2:3, :] + cb_ref[...]
            a = a * jax.nn.sigmoid(a) * v
            acc = _dot(a.astype(BF16), wd[...])
            if not first:
                acc = o_ref[rs, :] + acc
            if last:
                y = alpha * x_ref[rs, :] + (1.0 + rows_of(gt_ref, rs)) * acc
                acc = _layer_norm(y, lg_ref[...], lb_ref[...])
            o_ref[rs, :] = acc
        if carry:
            carry_scr[j] = tail
            tail_ref[...] = tail

    pl.when(j == 0)(functools.partial(step, True, False))
    pl.when((j > 0) & (j < n_chunks - 1))(functools.partial(step, False, False))
    pl.when(j == n_chunks - 1)(functools.partial(step, False, True))


def _ffn(x, sc, sh, gt, weights, conv_w, conv_b, ln_g, ln_b, alpha, tm, layer=None,
         hist=None, seq_len=None, tf=512, sub=512):
    rows, d = x.shape
    carry = hist is None
    d_ff = weights[-1].shape[-2]
    nj = d_ff // tf
    rm = sc.shape[0]
    mod_block = (1, d) if rm == 1 else (tm, d)
    mod_map = (lambda i, j: (0, 0)) if rm == 1 else (lambda i, j: (i, 0))
    up_tile = pl.BlockSpec((d, tf), lambda i, j: (0, j))
    down_tile = pl.BlockSpec((tf, d), lambda i, j: (j, 0))
    if carry:
        w_specs = [up_tile, up_tile, down_tile]
        w_args = list(weights)
    else:
        assert rows == tm
        w_up, w_down = weights
        w_specs = [
            pl.BlockSpec((None, d, tf), lambda i, j: (layer, 0, j)),
            pl.BlockSpec((None, d, tf), lambda i, j: (layer, 0, nj + j)),
            pl.BlockSpec((None, tf, d), lambda i, j: (layer, j, 0)),
        ]
        w_args = [w_up, w_up, w_down]
    in_specs = [
        pl.BlockSpec((tm, d), lambda i, j: (i, 0), pipeline_mode=pl.Buffered(1)),
        pl.BlockSpec(mod_block, mod_map),
        pl.BlockSpec(mod_block, mod_map),
        pl.BlockSpec(mod_block, mod_map),
    ] + w_specs + [
        pl.BlockSpec((CONV_W, tf), lambda i, j: (0, j)),
        pl.BlockSpec((1, tf), lambda i, j: (0, j)),
        pl.BlockSpec((1, d), lambda i, j: (0, 0)),
        pl.BlockSpec((1, d), lambda i, j: (0, 0)),
    ]
    args = [x, sc, sh, gt] + w_args + [conv_w, conv_b, ln_g, ln_b]
    scratch = [pltpu.VMEM((tm, d), BF16)]
    out_specs = [pl.BlockSpec((tm, d), lambda i, j: (i, 0))]
    out_shape = [jax.ShapeDtypeStruct((rows, d), F32)]
    if carry:
        scratch.append(pltpu.VMEM((nj, 8, tf), F32))
        out_specs.append(pl.BlockSpec((8, tf), lambda i, j: (i, j)))
        out_shape.append(jax.ShapeDtypeStruct((rows // tm * 8, d_ff), F32))
    else:
        in_specs += [pl.BlockSpec((tm, tf), lambda i, j: (i, j))] * 2
        args += list(hist)
        out_specs += [pl.BlockSpec((tm, tf), lambda i, j: (i, j)), up_tile, up_tile, down_tile]
        out_shape += [jax.ShapeDtypeStruct((rows, d_ff), F32),
                      jax.ShapeDtypeStruct((d, d_ff), BF16), jax.ShapeDtypeStruct((d, d_ff), BF16),
                      jax.ShapeDtypeStruct((d_ff, d), BF16)]
    kern = functools.partial(_ffn_kernel, alpha=alpha, seq_len=seq_len, n_chunks=nj, carry=carry,
                             sub=min(sub, tm))
    return pl.pallas_call(
        kern,
        grid=(rows // tm, nj),
        in_specs=in_specs,
        out_specs=out_specs,
        out_shape=out_shape,
        scratch_shapes=scratch,
        compiler_params=_cparams("arbitrary", "arbitrary"),
        name="conv_ffn",
    )(*args)


def _rope_tables(pos):
    half = HEAD_DIM // 2
    inv = ROPE_THETA ** (-jnp.arange(half, dtype=F32) / half)
    ang = pos.astype(F32)[:, None] * inv[None, :]
    cos, sin = jnp.cos(ang), jnp.sin(ang)
    return jnp.concatenate([cos, cos], -1), jnp.concatenate([-sin, sin], -1)


def kernel(x_prompt, x_sample, cache_k_win, cache_v_win, state_C, state_n, state_m, state_conv,
           c_prompt, c_sample, w_ada, b_ada, w_in, b_gate, g_head, w_out, ln1_g, ln1_b,
           w_up, conv_w, conv_b, w_down, ln2_g, ln2_b):
    bp, s, d = x_prompt.shape
    bs, t, _ = x_sample.shape
    depth = w_in.shape[0]
    d_ff = w_down.shape[1]
    alpha = (2 * depth) ** 0.25
    assert bp == 1 and d == (H_M + H_A) * HEAD_DIM and s % ATT_SUPER == 0

    n_c = bp + bs
    pad = (-n_c) % 8
    c_all = jnp.concatenate([c_prompt, c_sample, jnp.zeros((pad, d), F32)], 0)
    mod = _modulation(c_all, w_ada, b_ada)

    n_m = 2 * H_M * DK_M + 2 * H_M * DV_M
    w_m = w_in[:, :, :n_m].astype(BF16)
    w_a = w_in[:, :, n_m + 2 * H_M:].astype(BF16)
    w_gate = jnp.pad(w_in[:, :, n_m:n_m + 2 * H_M], ((0, 0), (0, 0), (0, N_GATE_PAD - 2 * H_M))).astype(BF16)
    gate_bias = jnp.pad(b_gate.reshape(depth, 1, 2 * H_M), ((0, 0), (0, 0), (0, N_GATE_PAD - 2 * H_M)))
    w_out_b = w_out.astype(BF16)

    cc_p, ss_p = _rope_tables(jnp.arange(s, dtype=jnp.int32))
    pos_s = PAST_LEN + jnp.arange(t, dtype=jnp.int32)
    cc_s, ss_s = (jnp.tile(a, (bs, 1)) for a in _rope_tables(pos_s))

    xp = x_prompt.reshape(s, d)
    xs = x_sample.reshape(bs * t, d)
    rows_s = bs * t
    tm_p = 1024
    outs =[[] for _ in range(12)]
    zeros_c = jnp.zeros((bp, H_M, DK_M, DV_M), F32)
    zeros_n = jnp.zeros((bp, H_M, DK_M), F32)
    zeros_m = jnp.zeros((bp, H_M), F32)
    row2 = lambda v: v.reshape(1, -1)

    for l in range(depth):
        mod_p = [mod[l, 0:1, k * d:(k + 1) * d] for k in range(6)]
        mod_s = [jnp.repeat(mod[l, bp:bp + bs, k * d:(k + 1) * d], t, axis=0) for k in range(6)]
        ln1 = (row2(ln1_g[l]), row2(ln1_b[l]))
        ffn_small = (conv_w[l], row2(conv_b[l]), row2(ln2_g[l]), row2(ln2_b[l]))

        main_s, kv_s, gates_s = _inproj(xs, mod_s[1], mod_s[0], w_m, w_a, w_gate, gate_bias, cc_s, ss_s, l,
                                        tm=rows_s, main_dtype=F32)
        h_s, c_s, n_s, m_s = _mlstm(main_s, gates_s, state_C, state_n, state_m, batch=bs, L=t, layer=l,
                                    nseq=MLSTM_SEQS_PER_STEP)
        att_s = _attn_sample(main_s, cache_k_win, cache_v_win, l, bs, t)
        x1_s = _merge(h_s, main_s, att_s, xs, mod_s[2], row2(g_head[l]), w_out_b, *ln1, l, alpha, tm=rows_s)
        hist = [jnp.repeat(state_conv[l][:, r, :], t, axis=0) for r in range(CONV_W - 1)]
        xs, g_s, *ffn_w = _ffn(x1_s, mod_s[4], mod_s[3], mod_s[5], (w_up, w_down), *ffn_small, alpha,
                               tm=rows_s, layer=l, hist=hist, seq_len=t)

        main_p, kv_p, gates_p = _inproj(xp, mod_p[1], mod_p[0], w_m, w_a, w_gate, gate_bias, cc_p, ss_p, l,
                                        tm=tm_p, main_dtype=BF16)
        h_p, c_p, n_p, m_p = _mlstm(main_p, gates_p, zeros_c, zeros_n, zeros_m, batch=bp, L=128)
        att_p = _attn_prompt(main_p)
        x1_p = _merge(h_p, main_p, att_p, xp, mod_p[2], row2(g_head[l]), w_out_b, *ln1, l, alpha, tm=512)
        xp, tail_p = _ffn(x1_p, mod_p[4], mod_p[3], mod_p[5], ffn_w, *ffn_small, alpha, tm=tm_p)

        wp = min(DILATIONS[-1][0], s)
        k_cols = slice(0, W_HEADS)
        v_cols = slice(W_HEADS, 2 * W_HEADS)
        outs[0].append(kv_p[s - wp:, k_cols].reshape(bp, wp, H_A, HEAD_DIM))
        outs[1].append(kv_p[s - wp:, v_cols].reshape(bp, wp, H_A, HEAD_DIM))
        outs[2].append(c_p)
        outs[3].append(n_p)
        outs[4].append(m_p.reshape(bp, H_M))
        outs[5].append(tail_p[-8:][8 - (CONV_W - 1):].reshape(bp, CONV_W - 1, d_ff))
        outs[6].append(kv_s[:, k_cols].reshape(bs, t, H_A, HEAD_DIM))
        outs[7].append(kv_s[:, v_cols].reshape(bs, t, H_A, HEAD_DIM))
        outs[8].append(c_s)
        outs[9].append(n_s)
        outs[10].append(m_s.reshape(bs, H_M))
        outs[11].append(g_s.reshape(bs, t, d_ff)[:, t - (CONV_W - 1):])

    return (xp.reshape(bp, s, d), xs.reshape(bs, t, d)) + tuple(jnp.stack(o) for o in outs)
```

```python
import functools

import jax
import jax.numpy as jnp
from jax import lax
from jax.experimental import pallas as pl
from jax.experimental.pallas import tpu as pltpu

F32 = jnp.float32
BF16 = jnp.bfloat16
HIGHEST = lax.Precision.HIGHEST

HEAD_DIM = 128
H_M = 8
H_A = 8
DK_M = 64
DV_M = 128
DILATIONS = ((128, 1), (512, 4), (2048, 16))
ATT_BLOCK = 128
PAST_LEN = 8192
ROPE_THETA = 10000.0
CONV_W = 3
LN_EPS = 1e-5
HEAD_NORM_EPS = 1e-6
N_GATE_PAD = 128
W_HEADS = H_A * HEAD_DIM
ATT_SUPER = ATT_BLOCK * DILATIONS[-1][1]
MLSTM_SEQS_PER_STEP = 8

VMEM_LIMIT_BYTES = 56 * 1024 * 1024


def _cparams(*sem):
    return pltpu.CompilerParams(dimension_semantics=sem, vmem_limit_bytes=VMEM_LIMIT_BYTES)


def _dot(a, b):
    return jnp.dot(a, b, preferred_element_type=F32)


def _dot_nt(a, b, precision=None):
    return lax.dot_general(a, b, (((1,), (1,)), ((), ())), precision=precision,
                           preferred_element_type=F32)


def _dot_tn(a, b):
    return lax.dot_general(a, b, (((0,), (0,)), ((), ())), preferred_element_type=F32)


def _iota(shape, dim):
    return lax.broadcasted_iota(jnp.int32, shape, dim)


def _mod_kernel(c_ref, w_ref, b_ref, o_ref):
    c = c_ref[...]
    a = (c * jax.nn.sigmoid(c)).astype(BF16)
    o_ref[...] = _dot(a, w_ref[...].astype(BF16)) + b_ref[...]


def _modulation(c_all, w_ada, b_ada, tn=1024):
    depth, d, n = w_ada.shape
    rows = c_all.shape[0]
    return pl.pallas_call(
        _mod_kernel,
        grid=(depth, n // tn),
        in_specs=[
            pl.BlockSpec((rows, d), lambda l, j: (0, 0)),
            pl.BlockSpec((None, d, tn), lambda l, j: (l, 0, j)),
            pl.BlockSpec((None, 1, tn), lambda l, j: (l, 0, j)),
        ],
        out_specs=pl.BlockSpec((None, rows, tn), lambda l, j: (l, 0, j)),
        out_shape=jax.ShapeDtypeStruct((depth, rows, n), F32),
        compiler_params=_cparams("arbitrary", "arbitrary"),
        name="adaln_mod",
    )(c_all, w_ada, b_ada.reshape(depth, 1, n))


def _inproj_kernel(x_ref, sc_ref, sh_ref, wm_ref, wa_ref, wg_ref, gb_ref, cc_ref, ss_ref,
                   main_ref, kv_ref, gate_ref, u_scr, *, tn, q_lo, k_lo, v_lo, sub):
    j = pl.program_id(1)
    tm = u_scr.shape[0]

    def rows_of(ref, rs):
        return ref[...] if ref.shape[0] == 1 else ref[rs, :]

    def finish(kind, acc, rs, c0):
        width = acc.shape[1]
        cols = slice(c0, c0 + width)
        if kind == "plain":
            main_ref[rs, cols] = acc.astype(main_ref.dtype)
        elif kind == "v":
            main_ref[rs, cols] = acc.astype(main_ref.dtype)
            kv_ref[rs, cols] = acc
        else:
            for g0 in range(0, width, HEAD_DIM):
                a = acc[:, g0:g0 + HEAD_DIM]
                sl = slice(c0 + g0, c0 + g0 + HEAD_DIM)
                y = a * cc_ref[rs, :] + pltpu.roll(a, HEAD_DIM // 2, 1) * ss_ref[rs, :]
                if kind == "q":
                    main_ref[rs, sl] = (y * (HEAD_DIM ** -0.5)).astype(main_ref.dtype)
                else:
                    main_ref[rs, sl] = y.astype(main_ref.dtype)
                    kv_ref[rs, sl] = y

    def step(kind, first):
        w_ref = wm_ref if kind == "plain" else wa_ref
        for r0 in range(0, tm, sub):
            rs = slice(r0, r0 + sub)
            if first:
                ub = (x_ref[rs, :] * (1.0 + rows_of(sc_ref, rs)) + rows_of(sh_ref, rs)).astype(BF16)
                u_scr[rs, :] = ub
                z = _dot(ub, wg_ref[...]) + gb_ref[...]
                lane = _iota(z.shape, 1)
                log_sig = jnp.minimum(z, 0.0) - jnp.log1p(jnp.exp(-jnp.abs(z)))
                gate_ref[rs, :] = jnp.where((lane >= H_M) & (lane < 2 * H_M), log_sig, z)
            else:
                ub = u_scr[rs, :]
            finish(kind, _dot(ub, w_ref[...]), rs, 0)

    pl.when(j == 0)(functools.partial(step, "plain", True))
    pl.when((j > 0) & (j < q_lo))(functools.partial(step, "plain", False))
    pl.when((j >= q_lo) & (j < k_lo))(functools.partial(step, "q", False))
    pl.when((j >= k_lo) & (j < v_lo))(functools.partial(step, "k", False))
    pl.when(j >= v_lo)(functools.partial(step, "v", False))


def _inproj(x, sc, sh, w_m, w_a, w_gate, gate_bias, rope_cc, rope_ss, layer, tm, main_dtype, tn=1024, sub=512):
    rows, d = x.shape
    n = w_m.shape[2] + w_a.shape[2]
    rm = sc.shape[0]
    mod_block = (1, d) if rm == 1 else (tm, d)
    mod_map = (lambda i, j: (0, 0)) if rm == 1 else (lambda i, j: (i, 0))
    q_lo = (3 * W_HEADS) // tn
    k_lo = (4 * W_HEADS) // tn
    v_lo = (5 * W_HEADS) // tn
    kern = functools.partial(_inproj_kernel, tn=tn, q_lo=q_lo, k_lo=k_lo, v_lo=v_lo, sub=min(sub, tm))
    return pl.pallas_call(
        kern,
        grid=(rows // tm, n // tn),
        in_specs=[
            pl.BlockSpec((tm, d), lambda i, j: (i, 0)),
            pl.BlockSpec(mod_block, mod_map),
            pl.BlockSpec(mod_block, mod_map),
            pl.BlockSpec((None, d, tn), lambda i, j: (layer, 0, jnp.minimum(j, q_lo - 1))),
            pl.BlockSpec((None, d, tn), lambda i, j: (layer, 0, jnp.maximum(j - q_lo, 0))),
            pl.BlockSpec((None, d, N_GATE_PAD), lambda i, j: (layer, 0, 0)),
            pl.BlockSpec((None, 1, N_GATE_PAD), lambda i, j: (layer, 0, 0)),
            pl.BlockSpec((tm, HEAD_DIM), lambda i, j: (i, 0)),
            pl.BlockSpec((tm, HEAD_DIM), lambda i, j: (i, 0)),
        ],
        out_specs=[
            pl.BlockSpec((tm, tn), lambda i, j: (i, j)),
            pl.BlockSpec((tm, tn), lambda i, j: (i, jnp.maximum(j - k_lo, 0))),
            pl.BlockSpec((tm, N_GATE_PAD), lambda i, j: (i, 0)),
        ],
        out_shape=[
            jax.ShapeDtypeStruct((rows, n), main_dtype),
            jax.ShapeDtypeStruct((rows, 2 * W_HEADS), F32),
            jax.ShapeDtypeStruct((rows, N_GATE_PAD), F32),
        ],
        scratch_shapes=[pltpu.VMEM((tm, d), BF16)],
        compiler_params=_cparams("arbitrary", "arbitrary"),
        name="inproj",
    )(x, sc, sh, w_m, w_a, w_gate, gate_bias, rope_cc, rope_ss)


def _mlstm_kernel(q_ref, k_ref, v_ref, g_ref, c0_ref, n0_ref, m0_ref,
                  h_ref, c_ref, n_ref, m_ref, *, L, nseq):
    @pl.when(pl.program_id(1) == 0)
    def _():
        c_ref[...] = c0_ref[...]
        n_ref[...] = n0_ref[...]
        m_ref[...] = m0_ref[...]

    row = _iota((L, L), 0)
    col = _iota((L, L), 1)
    causal = row >= col
    tril = causal.astype(F32)
    triu = (row <= col).astype(F32)
    eye = (_iota((2 * H_M, N_GATE_PAD), 0) == _iota((2 * H_M, N_GATE_PAD), 1)).astype(F32)
    hi_dot = functools.partial(jnp.dot, precision=HIGHEST, preferred_element_type=F32)

    seqs = []
    for bi in range(nseq):
        gates = g_ref[bi * L:(bi + 1) * L, :]
        gates_t = _dot_nt(eye, gates, precision=HIGHEST)
        seqs.append((gates, gates_t, hi_dot(tril, gates), hi_dot(gates_t, triu)))

    units = []
    for bi, (gates, gates_t, bcol_all, brow_all) in enumerate(seqs):
        rows = slice(bi * L, (bi + 1) * L)
        for h in range(H_M):
            b_col = bcol_all[:, H_M + h:H_M + h + 1]
            b_row = brow_all[H_M + h:H_M + h + 1, :]
            m0 = m_ref[bi, :, h:h + 1]
            dmat = jnp.where(causal, b_col - b_row + gates_t[h:h + 1, :], -jnp.inf)
            m_inter = b_col + m0
            m = jnp.maximum(m_inter, jnp.max(dmat, axis=1, keepdims=True))
            q = q_ref[rows, h * DK_M:(h + 1) * DK_M].astype(F32)
            k = k_ref[rows, h * DK_M:(h + 1) * DK_M].astype(F32) * (DK_M ** -0.5)
            qb = q.astype(BF16)
            units.append(dict(
                bi=bi, h=h, rows=rows, m0=m0, m=m, w=jnp.exp(dmat - m), g=jnp.exp(m_inter - m),
                q=q, k=k, qb=qb, vb=v_ref[rows, h * DV_M:(h + 1) * DV_M].astype(BF16),
                s_raw=_dot_nt(qb, k.astype(BF16)), b_col=b_col, ig_col=gates[:, h:h + 1]))

    for u in units:
        bi, h = u["bi"], u["h"]
        s = u["s_raw"] * u["w"]
        num = u["g"] * _dot(u["qb"], c_ref[bi, h].astype(BF16)) + _dot(s.astype(BF16), u["vb"])
        den = (u["g"] * jnp.sum(u["q"] * n_ref[bi, h:h + 1, :], axis=1, keepdims=True)
               + jnp.sum(s, axis=1, keepdims=True))
        h_ref[u["rows"], h * DV_M:(h + 1) * DV_M] = num / jnp.maximum(jnp.abs(den), jnp.exp(-u["m"]))

    for u in units:
        bi, h, b_col, m0 = u["bi"], u["h"], u["b_col"], u["m0"]
        b_last = b_col[L - 1:L, :]
        a_col = b_last - b_col + u["ig_col"]
        m_end = jnp.maximum(b_last + m0, jnp.max(a_col, axis=0, keepdims=True))
        g_end = jnp.exp(b_last + m0 - m_end)
        kw = jnp.exp(a_col - m_end) * u["k"]
        c_ref[bi, h] = g_end * c_ref[bi, h] + _dot_tn(kw.astype(BF16), u["vb"])
        n_ref[bi, h:h + 1, :] = g_end * n_ref[bi, h:h + 1, :] + jnp.sum(kw, axis=0, keepdims=True)
        m_ref[bi, :, h:h + 1] = m_end


def _mlstm_wide_kernel(q_ref, k_ref, v_ref, g_ref, c0_ref, n0_ref, m0_ref,
                       h_ref, c_ref, n_ref, m_ref, nmat_scr):
    L = HEAD_DIM
    first = pl.program_id(1) == 0
    sub8 = _iota((H_M, L), 0)

    @pl.when(first)
    def _():
        c_ref[...] = c0_ref[...]
        n_ref[...] = n0_ref[...]
        m_ref[...] = m0_ref[...]
        for h in range(H_M):
            nmat_scr[h] = lax.dot_general(n0_ref[...], (sub8 == h).astype(F32), (((0,), (0,)), ((), ())),
                                          precision=HIGHEST, preferred_element_type=F32)

    gates = g_ref[...]
    row = _iota((L, L), 0)
    col = _iota((L, L), 1)
    causal = row >= col
    triu = (row <= col).astype(F32)
    eye = (_iota((2 * H_M, N_GATE_PAD), 0) == _iota((2 * H_M, N_GATE_PAD), 1)).astype(F32)
    gates_t = _dot_nt(eye, gates, precision=HIGHEST)
    ig = gates_t[:H_M]
    b = jnp.dot(gates_t[H_M:], triu, precision=HIGHEST, preferred_element_type=F32)
    a = ig - b
    cm = a
    lane = _iota((H_M, L), 1)
    shift = 1
    while shift < L:
        cm = jnp.maximum(cm, jnp.where(lane >= shift, pltpu.roll(cm, shift, 1), -jnp.inf))
        shift *= 2

    eye8 = _iota((H_M, H_M), 0) == _iota((H_M, H_M), 1)
    m0_col = jnp.sum(jnp.where(eye8, jnp.broadcast_to(m_ref[...], (H_M, H_M)), 0.0), axis=1, keepdims=True)
    mm = jnp.maximum(m0_col, cm)
    mm_last = mm[:, L - 1:L]
    ws_all = jnp.exp(a - mm_last)
    g_end_all = jnp.exp(m0_col - mm_last)
    m_end = b[:, L - 1:L] + mm_last
    m_ref[...] = jnp.sum(jnp.where(eye8, jnp.broadcast_to(m_end, (H_M, H_M)), 0.0), axis=0, keepdims=True)
    def split(x, parts):
        out = []
        for _ in range(parts):
            p = x.astype(BF16)
            out.append(p)
            x = x - p.astype(F32)
        return out

    heads = range(H_M)
    ones_b = jnp.ones((L, L), BF16)

    cols = jnp.concatenate([-mm, -mm - b], axis=0).T
    n_sel = 3 * 2 * H_M
    sel_r = _iota((n_sel, H_M * 2 * L), 0) & (2 * H_M - 1)
    sel_c = _iota((n_sel, H_M * 2 * L), 1)
    sel_h = sel_c >> ((2 * L).bit_length() - 1)
    left = (sel_c & (2 * L - 1)) < L
    sel = (left & (sel_r == sel_h)) | (jnp.logical_not(left) & (sel_r == H_M + sel_h))
    bc = _dot(jnp.concatenate(split(cols, 3), axis=1), sel.astype(F32).astype(BF16))

    qb = [q_ref[:, h * DK_M:(h + 1) * DK_M].astype(BF16) for h in heads]
    kt = [(k_ref[:, h * DK_M:(h + 1) * DK_M].astype(F32) * (DK_M ** -0.5)).T for h in heads]
    vo = [jnp.concatenate([v_ref[:, h * DV_M:(h + 1) * DV_M].astype(BF16), ones_b], axis=1) for h in heads]
    s_raw = [_dot(qb[h], kt[h].astype(BF16)) for h in heads]
    inter = [_dot(qb[h], jnp.concatenate([c_ref[h].astype(BF16), nmat_scr[h].astype(BF16)], axis=1))
             for h in heads]

    for h in heads:
        neg_mm = bc[:, 2 * L * h:2 * L * h + L]
        neg_m = bc[:, 2 * L * h + L:2 * L * (h + 1)]
        w = jnp.where(causal, jnp.exp(neg_mm + a[h:h + 1, :]), 0.0)
        gmat = jnp.exp(neg_mm + m0_col[h:h + 1, :])
        s_hi, s_lo = split(s_raw[h] * w, 2)
        r = _dot(s_hi, vo[h])
        num = gmat * inter[h][:, :L] + r[:, :L]
        den = gmat * inter[h][:, L:] + r[:, L:] + _dot(s_lo, ones_b)
        h_ref[:, h * DV_M:(h + 1) * DV_M] = num / jnp.maximum(jnp.abs(den), jnp.exp(neg_m))

    for h in heads:
        g_end = g_end_all[h:h + 1, :]
        kw_hi, kw_lo = split(kt[h] * ws_all[h:h + 1, :], 2)
        u = _dot(kw_hi, vo[h])
        c_ref[h] = g_end * c_ref[h] + u[:, :L]
        nmat_scr[h] = g_end * nmat_scr[h] + u[:, L:] + _dot(kw_lo, ones_b)

    @pl.when(pl.program_id(1) == pl.num_programs(1) - 1)
    def _():
        for h in heads:
            n_ref[h:h + 1, :] = nmat_scr[h].T[0:1, :]


def _mlstm(main, gates, c0, n0, m0, batch, L, layer=None, nseq=1):
    rows = main.shape[0]
    nc = rows // (batch * L)
    if L == HEAD_DIM:
        assert nseq == 1
        kern, scratch, lead = _mlstm_wide_kernel, [pltpu.VMEM((H_M, DK_M, HEAD_DIM), F32)], None
    else:
        assert nc == 1 and batch % nseq == 0
        kern, scratch, lead = functools.partial(_mlstm_kernel, L=L, nseq=nseq), [], nseq
    m0 = m0.reshape(m0.shape[:-1] + (1, H_M))

    def state_spec(tail, stacked):
        zeros = (0,) * len(tail)
        if stacked:
            return pl.BlockSpec((None, lead) + tail, lambda b, c: (layer, b) + zeros)
        return pl.BlockSpec((lead,) + tail, lambda b, c: (b,) + zeros)

    tails = ((H_M, DK_M, DV_M), (H_M, DK_M), (1, H_M))
    blk = nseq * L
    return pl.pallas_call(
        kern,
        grid=(batch // nseq, nc),
        in_specs=[
            pl.BlockSpec((blk, H_M * DK_M), lambda b, c: (b * nc + c, 0)),
            pl.BlockSpec((blk, H_M * DK_M), lambda b, c: (b * nc + c, 1)),
            pl.BlockSpec((blk, H_M * DV_M), lambda b, c: (b * nc + c, 1)),
            pl.BlockSpec((blk, N_GATE_PAD), lambda b, c: (b * nc + c, 0)),
        ] + [state_spec(t, layer is not None) for t in tails],
        out_specs=[pl.BlockSpec((blk, H_M * DV_M), lambda b, c: (b * nc + c, 0))]
        + [state_spec(t, False) for t in tails],
        out_shape=[
            jax.ShapeDtypeStruct((rows, H_M * DV_M), F32),
            jax.ShapeDtypeStruct((batch, H_M, DK_M, DV_M), F32),
            jax.ShapeDtypeStruct((batch, H_M, DK_M), F32),
            jax.ShapeDtypeStruct((batch, 1, H_M), F32),
        ],
        scratch_shapes=scratch,
        compiler_params=_cparams("arbitrary", "arbitrary"),
        name="mlstm",
    )(main, main, main, gates, c0, n0, m0)


def _attn_prompt_kernel(q_ref, kp_ref, kc_ref, vp_ref, vc_ref, o_ref,
                        qq_scr, kk_scr, vv_scr, num_scr, m_scr, den_scr):
    sb = pl.program_id(0)
    SB = ATT_SUPER
    qq_scr[...] = q_ref[...].astype(F32)
    kk_scr[0:SB, :] = kp_ref[...].astype(F32)
    kk_scr[SB:, :] = kc_ref[...].astype(F32)
    vv_scr[0:SB, :] = vp_ref[...].astype(F32)
    vv_scr[SB:, :] = vc_ref[...].astype(F32)

    shape = (ATT_BLOCK, 2 * ATT_BLOCK)
    qi = _iota(shape, 0)
    ki = _iota(shape, 1)
    window = (ki >= qi) & (ki <= qi + ATT_BLOCK)
    window_first = window & ((ki >= ATT_BLOCK) | (sb > 0))
    ones_b = jnp.ones((2 * ATT_BLOCK, HEAD_DIM), BF16)

    for bi, (_, dil) in enumerate(DILATIONS):
        for n in range(SB // (ATT_BLOCK * dil)):
            valid = window_first if n == 0 else window

            for r in range(dil):
                q_rows = pl.ds(r + n * ATT_BLOCK * dil, ATT_BLOCK, stride=dil)
                k_rows = pl.ds(SB + r + (n - 1) * ATT_BLOCK * dil, 2 * ATT_BLOCK, stride=dil)
                q = qq_scr[q_rows, :].astype(BF16)
                k = kk_scr[k_rows, :].astype(BF16)
                v = vv_scr[k_rows, :].astype(BF16)
                s = jnp.where(valid, _dot_nt(q, k), -jnp.inf)
                m = jnp.max(s, axis=1, keepdims=True)
                p = jnp.exp(s - m).astype(BF16)
                r = _dot(p, jnp.concatenate([v, ones_b], axis=1))
                num_scr[bi, q_rows, :] = r[:, :HEAD_DIM]
                den_scr[bi, q_rows, :] = r[:, HEAD_DIM:]
                m_scr[bi, q_rows, :] = jnp.broadcast_to(m, (ATT_BLOCK, HEAD_DIM))

    rows_per_step = 2 * ATT_BLOCK

    def mix(i, carry):
        rows = pl.ds(pl.multiple_of(i * rows_per_step, rows_per_step), rows_per_step)
        ms = [m_scr[b, rows, :] for b in range(len(DILATIONS))]
        big = ms[0]
        for mm in ms[1:]:
            big = jnp.maximum(big, mm)
        w0 = jnp.exp(ms[0] - big)
        acc_num = w0 * num_scr[0, rows, :]
        acc_den = w0 * den_scr[0, rows, :]
        for b in range(1, len(DILATIONS)):
            w = jnp.exp(ms[b] - big)
            acc_num = acc_num + w * num_scr[b, rows, :]
            acc_den = acc_den + w * den_scr[b, rows, :]
        o_ref[rows, :] = acc_num / acc_den
        return carry

    lax.fori_loop(0, SB // rows_per_step, mix, 0)


def _attn_prompt(main):
    s, n_main = main.shape
    SB = ATT_SUPER
    cols = n_main // HEAD_DIM // 6
    blk = (SB, HEAD_DIM)
    prev = lambda i: jnp.maximum(i - 1, 0)
    nbr = len(DILATIONS)
    return pl.pallas_call(
        _attn_prompt_kernel,
        grid=(s // SB, H_A),
        in_specs=[
            pl.BlockSpec(blk, lambda i, h: (i, 3 * cols + h)),
            pl.BlockSpec(blk, lambda i, h: (prev(i), 4 * cols + h)),
            pl.BlockSpec(blk, lambda i, h: (i, 4 * cols + h)),
            pl.BlockSpec(blk, lambda i, h: (prev(i), 5 * cols + h)),
            pl.BlockSpec(blk, lambda i, h: (i, 5 * cols + h)),
        ],
        out_specs=pl.BlockSpec(blk, lambda i, h: (i, h)),
        out_shape=jax.ShapeDtypeStruct((s, W_HEADS), F32),
        scratch_shapes=[
            pltpu.VMEM((SB, HEAD_DIM), F32),
            pltpu.VMEM((2 * SB, HEAD_DIM), F32),
            pltpu.VMEM((2 * SB, HEAD_DIM), F32),
            pltpu.VMEM((nbr, SB, HEAD_DIM), F32),
            pltpu.VMEM((nbr, SB, HEAD_DIM), F32),
            pltpu.VMEM((nbr, SB, HEAD_DIM), F32),
        ],
        compiler_params=_cparams("arbitrary", "arbitrary"),
        name="attn_prompt",
    )(main, main, main, main, main)


def _branch_count(delta):
    cnt = jnp.zeros(delta.shape, F32)
    for window, dil in DILATIONS:
        hit = (delta >= 0) & (delta <= window) & ((delta & (dil - 1)) == 0)
        cnt = cnt + hit.astype(F32)
    return cnt


def _attn_sample_kernel(q_ref, kn_ref, vn_ref, kfar_ref, knear_ref, vfar_ref, vnear_ref, o_ref,
                        cnt_scr, s_scr, *, T, NB, chunk):
    R = H_A * T
    t_bits = T.bit_length() - 1
    h_bits = H_A.bit_length() - 1
    d_far = DILATIONS[-1][1]
    n_groups, keep = kfar_ref.shape[0], kfar_ref.shape[1]
    keep_bits = keep.bit_length() - 1
    near_pos = knear_ref.shape[0] // H_A
    gpc = chunk // (keep * H_A)
    far_chunks = n_groups // gpc
    n_chunks = far_chunks + near_pos * H_A // chunk
    chunks = [slice(c * chunk, (c + 1) * chunk) for c in range(n_chunks)]

    def rows(far_ref, near_ref, c):
        if c < far_chunks:
            return far_ref[c * gpc:(c + 1) * gpc].reshape(chunk, HEAD_DIM)
        return near_ref[chunks[c - far_chunks], :]

    @pl.when(pl.program_id(0) == 0)
    def _():
        for c, sl in enumerate(chunks):
            row = _iota((R, chunk), 0)
            col = _iota((R, chunk), 1)
            if c < far_chunks:
                pos = (c * gpc + (col >> (h_bits + keep_bits))) * d_far + ((col >> h_bits) & (keep - 1))
            else:
                pos = NB - near_pos + (c - far_chunks) * (chunk // H_A) + (col >> h_bits)
            same_head = (row >> t_bits) == (col & (H_A - 1))
            cnt_scr[:, sl] = jnp.where(same_head, _branch_count(NB + (row & (T - 1)) - pos), 0.0)

    heads = [slice(h * HEAD_DIM, (h + 1) * HEAD_DIM) for h in range(H_A)]
    q = jnp.concatenate([q_ref[:, sl] for sl in heads], axis=0).astype(BF16)
    kn = jnp.concatenate([kn_ref[:, sl] for sl in heads], axis=0).astype(BF16)
    vn = jnp.concatenate([vn_ref[:, sl] for sl in heads], axis=0).astype(BF16)
    row = _iota((R, R), 0)
    col = _iota((R, R), 1)
    cnt_n = jnp.where((row >> t_bits) == (col >> t_bits),
                      _branch_count((row & (T - 1)) - (col & (T - 1))), 0.0)

    s_n = jnp.where(cnt_n > 0, _dot_nt(q, kn), -jnp.inf)
    big = jnp.max(s_n, axis=1, keepdims=True)
    for c, sl in enumerate(chunks):
        s = jnp.where(cnt_scr[:, sl] > 0, _dot_nt(q, rows(kfar_ref, knear_ref, c).astype(BF16)), -jnp.inf)
        s_scr[:, sl] = s
        big = jnp.maximum(big, jnp.max(s, axis=1, keepdims=True))

    p_n = cnt_n * jnp.exp(s_n - big)
    den = jnp.sum(p_n, axis=1, keepdims=True)
    num = _dot(p_n.astype(BF16), vn)
    for c, sl in enumerate(chunks):
        p = cnt_scr[:, sl] * jnp.exp(s_scr[:, sl] - big)
        den = den + jnp.sum(p, axis=1, keepdims=True)
        num = num + _dot(p.astype(BF16), rows(vfar_ref, vnear_ref, c).astype(BF16))
    out = num / den
    for h, sl in enumerate(heads):
        o_ref[:, sl] = out[h * T:(h + 1) * T, :]


def _attn_sample(main, cache_k, cache_v, layer, batch, T, chunk=2048):
    depth, _, nb, n_heads, hd = cache_k.shape
    assert n_heads == H_A and hd == HEAD_DIM and T & (T - 1) == 0 and H_A & (H_A - 1) == 0
    d_far = DILATIONS[-1][1]
    near_pos = DILATIONS[-2][0]
    assert all(w <= near_pos for w, _ in DILATIONS[:-1]) and nb % d_far == 0 and T <= d_far
    assert (nb - near_pos) % d_far == 0 and near_pos <= nb
    n_groups = (nb - near_pos) // d_far
    nf = nb * H_A
    n_keys = (n_groups * T + near_pos) * H_A
    assert (n_groups * T * H_A) % chunk == 0 and (near_pos * H_A) % chunk == 0 and nf % (near_pos * H_A) == 0
    far_view = lambda c: c.reshape(depth, batch, nb // d_far, d_far, H_A, HEAD_DIM)
    near_view = lambda c: c.reshape(depth, batch, nf, HEAD_DIM)
    kern = functools.partial(_attn_sample_kernel, T=T, NB=nb, chunk=chunk)
    far_spec = pl.BlockSpec((None, None, n_groups, T, H_A, HEAD_DIM), lambda b: (layer, b, 0, 0, 0, 0))
    near_spec = pl.BlockSpec((None, None, near_pos * H_A, HEAD_DIM),
                             lambda b: (layer, b, nf // (near_pos * H_A) - 1, 0))
    return pl.pallas_call(
        kern,
        grid=(batch,),
        in_specs=[
            pl.BlockSpec((T, W_HEADS), lambda b: (b, 3)),
            pl.BlockSpec((T, W_HEADS), lambda b: (b, 4)),
            pl.BlockSpec((T, W_HEADS), lambda b: (b, 5)),
            far_spec,
            near_spec,
            far_spec,
            near_spec,
        ],
        out_specs=pl.BlockSpec((T, W_HEADS), lambda b: (b, 0)),
        out_shape=jax.ShapeDtypeStruct((batch * T, W_HEADS), F32),
        scratch_shapes=[pltpu.VMEM((H_A * T, n_keys), F32), pltpu.VMEM((H_A * T, n_keys), F32)],
        compiler_params=_cparams("arbitrary"),
        name="attn_sample",
    )(main, main, main, far_view(cache_k), near_view(cache_k), far_view(cache_v), near_view(cache_v))


def _layer_norm(y, g, b):
    mu = jnp.mean(y, axis=1, keepdims=True)
    yc = y - mu
    var = jnp.mean(yc * yc, axis=1, keepdims=True)
    return yc * lax.rsqrt(var + LN_EPS) * g + b


def _merge_kernel(h_ref, om_ref, att_ref, x_ref, gt_ref, gh_ref, w_ref, lg_ref, lb_ref,
                  o_ref, *rest, alpha, sub):
    w_copy_ref, cat_scr = rest if len(rest) == 2 else (None, rest[0])
    if w_copy_ref is not None:
        w_copy_ref[...] = w_ref[...].astype(BF16)
        w_ref = w_copy_ref
    for r0 in range(0, cat_scr.shape[0], sub):
        rs = slice(r0, r0 + sub)
        for h in range(H_M):
            sl = slice(h * DV_M, (h + 1) * DV_M)
            hh = h_ref[rs, sl]
            hn = hh * lax.rsqrt(jnp.mean(hh * hh, axis=1, keepdims=True) + HEAD_NORM_EPS) * gh_ref[:, sl]
            cat_scr[rs, sl] = (hn * jax.nn.sigmoid(om_ref[rs, sl].astype(F32))).astype(BF16)
        cat_scr[rs, H_M * DV_M:] = att_ref[rs, :].astype(BF16)
        mix = _dot(cat_scr[rs, :], w_ref[...])
        gt = gt_ref[...] if gt_ref.shape[0] == 1 else gt_ref[rs, :]
        y = alpha * x_ref[rs, :] + (1.0 + gt) * mix
        o_ref[rs, :] = _layer_norm(y, lg_ref[...], lb_ref[...])


def _merge(h, main, att, x, gt, g_head, w_out, ln_g, ln_b, alpha, tm, layer=None, sub=256):
    rows, d = x.shape
    rm = gt.shape[0]
    mod_block = (1, d) if rm == 1 else (tm, d)
    mod_map = (lambda i: (0, 0)) if rm == 1 else (lambda i: (i, 0))
    wide = pl.BlockSpec((tm, W_HEADS), lambda i: (i, 0))
    const = lambda shape: pl.BlockSpec(shape, lambda i: (0, 0))
    emit = w_out.ndim == 3
    assert not emit or rows == tm
    w_spec = pl.BlockSpec((None, d, d), lambda i: (layer, 0, 0)) if emit else const((d, d))
    row_out = (pl.BlockSpec((tm, d), lambda i: (i, 0)), jax.ShapeDtypeStruct((rows, d), F32))
    outs = [row_out, (const((d, d)), jax.ShapeDtypeStruct((d, d), BF16))] if emit else [row_out]
    return pl.pallas_call(
        functools.partial(_merge_kernel, alpha=alpha, sub=min(sub, tm)),
        grid=(rows // tm,),
        in_specs=[
            wide,
            pl.BlockSpec((tm, W_HEADS), lambda i: (i, 2)),
            wide,
            pl.BlockSpec((tm, d), lambda i: (i, 0)),
            pl.BlockSpec(mod_block, mod_map),
            const((1, W_HEADS)),
            w_spec,
            const((1, d)),
            const((1, d)),
        ],
        out_specs=[spec for spec, _ in outs],
        out_shape=[shape for _, shape in outs],
        scratch_shapes=[pltpu.VMEM((tm, d), BF16)],
        compiler_params=_cparams("arbitrary"),
        name="mixer_merge",
    )(h, main, att, x, gt, g_head, w_out, ln_g, ln_b)


def _ffn_kernel(*refs, alpha, seq_len, n_chunks, carry, sub):
    if carry:
        (x_ref, sc_ref, sh_ref, gt_ref, wg_ref, wv_ref, wd_ref, cw_ref, cb_ref, lg_ref, lb_ref,
         o_ref, tail_ref, u_scr, carry_scr) = refs
        w_copies = None
    else:
        (x_ref, sc_ref, sh_ref, gt_ref, wg_ref, wv_ref, wd_ref, cw_ref, cb_ref, lg_ref, lb_ref,
         h0_ref, h1_ref, o_ref, tail_ref, wgb_ref, wvb_ref, wdb_ref, u_scr) = refs
        w_copies = (wgb_ref, wvb_ref, wdb_ref)
    i = pl.program_id(0)
    j = pl.program_id(1)
    tm = u_scr.shape[0]
    assert n_chunks >= 2

    def rows_of(ref, rs):
        return ref[...] if ref.shape[0] == 1 else ref[rs, :]

    if carry:
        @pl.when(i == 0)
        def _():
            carry_scr[j] = jnp.zeros(carry_scr.shape[1:], F32)

    def step(first, last):
        cw = cw_ref[...]
        if carry:
            tail = carry_scr[j]
        if w_copies is not None:
            for copy_ref, w_ref in zip(w_copies, (wg_ref, wv_ref, wd_ref)):
                copy_ref[...] = w_ref[...].astype(BF16)
            wg, wv, wd = w_copies
        else:
            wg, wv, wd = wg_ref, wv_ref, wd_ref
        for r0 in range(0, tm, sub):
            rs = slice(r0, r0 + sub)
            if first:
                ub = (x_ref[rs, :] * (1.0 + rows_of(sc_ref, rs)) + rows_of(sh_ref, rs)).astype(BF16)
                u_scr[rs, :] = ub
            else:
                ub = u_scr[rs, :]
            g = _dot(ub, wg[...])
            v = _dot(ub, wv[...])
            row = _iota(g.shape, 0)
            if carry:
                hist0, hist1 = tail[6:7, :], tail[7:8, :]
                pos = row
                tail = g[sub - 8:, :]
            else:
                hist0, hist1 = h0_ref[rs, :], h1_ref[rs, :]
                pos = row & (seq_len - 1)
                tail_ref[rs, :] = g
            prev1 = jnp.where(pos == 0, hist1, pltpu.roll(g, 1, 0))
            prev2 = jnp.where(pos == 0, hist0, jnp.where(pos == 1, hist1, pltpu.roll(g, 2, 0)))
            a = prev2 * cw[0:1, :] + prev1 * cw[1:2, :] + g * cw[2:3, :] + cb_ref[...]
            a = a * jax.nn.sigmoid(a) * v
            acc = _dot(a.astype(BF16), wd[...])
            if not first:
                acc = o_ref[rs, :] + acc
            if last:
                y = alpha * x_ref[rs, :] + (1.0 + rows_of(gt_ref, rs)) * acc
                acc = _layer_norm(y, lg_ref[...], lb_ref[...])
            o_ref[rs, :] = acc
        if carry:
            carry_scr[j] = tail
            tail_ref[...] = tail

    pl.when(j == 0)(functools.partial(step, True, False))
    pl.when((j > 0) & (j < n_chunks - 1))(functools.partial(step, False, False))
    pl.when(j == n_chunks - 1)(functools.partial(step, False, True))


def _ffn(x, sc, sh, gt, weights, conv_w, conv_b, ln_g, ln_b, alpha, tm, layer=None,
         hist=None, seq_len=None, tf=512, sub=512):
    rows, d = x.shape
    carry = hist is None
    d_ff = weights[-1].shape[-2]
    nj = d_ff // tf
    rm = sc.shape[0]
    mod_block = (1, d) if rm == 1 else (tm, d)
    mod_map = (lambda i, j: (0, 0)) if rm == 1 else (lambda i, j: (i, 0))
    up_tile = pl.BlockSpec((d, tf), lambda i, j: (0, j))
    down_tile = pl.BlockSpec((tf, d), lambda i, j: (j, 0))
    if carry:
        w_specs = [up_tile, up_tile, down_tile]
        w_args = list(weights)
    else:
        assert rows == tm
        w_up, w_down = weights
        w_specs = [
            pl.BlockSpec((None, d, tf), lambda i, j: (layer, 0, j)),
            pl.BlockSpec((None, d, tf), lambda i, j: (layer, 0, nj + j)),
            pl.BlockSpec((None, tf, d), lambda i, j: (layer, j, 0)),
        ]
        w_args = [w_up, w_up, w_down]
    in_specs = [
        pl.BlockSpec((tm, d), lambda i, j: (i, 0), pipeline_mode=pl.Buffered(1)),
        pl.BlockSpec(mod_block, mod_map),
        pl.BlockSpec(mod_block, mod_map),
        pl.BlockSpec(mod_block, mod_map),
    ] + w_specs + [
        pl.BlockSpec((CONV_W, tf), lambda i, j: (0, j)),
        pl.BlockSpec((1, tf), lambda i, j: (0, j)),
        pl.BlockSpec((1, d), lambda i, j: (0, 0)),
        pl.BlockSpec((1, d), lambda i, j: (0, 0)),
    ]
    args = [x, sc, sh, gt] + w_args + [conv_w, conv_b, ln_g, ln_b]
    scratch = [pltpu.VMEM((tm, d), BF16)]
    out_specs = [pl.BlockSpec((tm, d), lambda i, j: (i, 0))]
    out_shape = [jax.ShapeDtypeStruct((rows, d), F32)]
    if carry:
        scratch.append(pltpu.VMEM((nj, 8, tf), F32))
        out_specs.append(pl.BlockSpec((8, tf), lambda i, j: (i, j)))
        out_shape.append(jax.ShapeDtypeStruct((rows // tm * 8, d_ff), F32))
    else:
        in_specs += [pl.BlockSpec((tm, tf), lambda i, j: (i, j))] * 2
        args += list(hist)
        out_specs += [pl.BlockSpec((tm, tf), lambda i, j: (i, j)), up_tile, up_tile, down_tile]
        out_shape += [jax.ShapeDtypeStruct((rows, d_ff), F32),
                      jax.ShapeDtypeStruct((d, d_ff), BF16), jax.ShapeDtypeStruct((d, d_ff), BF16),
                      jax.ShapeDtypeStruct((d_ff, d), BF16)]
    kern = functools.partial(_ffn_kernel, alpha=alpha, seq_len=seq_len, n_chunks=nj, carry=carry,
                             sub=min(sub, tm))
    return pl.pallas_call(
        kern,
        grid=(rows // tm, nj),
        in_specs=in_specs,
        out_specs=out_specs,
        out_shape=out_shape,
        scratch_shapes=scratch,
        compiler_params=_cparams("arbitrary", "arbitrary"),
        name="conv_ffn",
    )(*args)


def _rope_tables(pos):
    half = HEAD_DIM // 2
    inv = ROPE_THETA ** (-jnp.arange(half, dtype=F32) / half)
    ang = pos.astype(F32)[:, None] * inv[None, :]
    cos, sin = jnp.cos(ang), jnp.sin(ang)
    return jnp.concatenate([cos, cos], -1), jnp.concatenate([-sin, sin], -1)


def kernel(x_prompt, x_sample, cache_k_win, cache_v_win, state_C, state_n, state_m, state_conv,
           c_prompt, c_sample, w_ada, b_ada, w_in, b_gate, g_head, w_out, ln1_g, ln1_b,
           w_up, conv_w, conv_b, w_down, ln2_g, ln2_b):
    bp, s, d = x_prompt.shape
    bs, t, _ = x_sample.shape
    depth = w_in.shape[0]
    d_ff = w_down.shape[1]
    alpha = (2 * depth) ** 0.25
    assert bp == 1 and d == (H_M + H_A) * HEAD_DIM and s % ATT_SUPER == 0

    n_c = bp + bs
    pad = (-n_c) % 8
    c_all = jnp.concatenate([c_prompt, c_sample, jnp.zeros((pad, d), F32)], 0)
    mod = _modulation(c_all, w_ada, b_ada)

    n_m = 2 * H_M * DK_M + 2 * H_M * DV_M
    w_m = w_in[:, :, :n_m].astype(BF16)
    w_a = w_in[:, :, n_m + 2 * H_M:].astype(BF16)
    w_gate = jnp.pad(w_in[:, :, n_m:n_m + 2 * H_M], ((0, 0), (0, 0), (0, N_GATE_PAD - 2 * H_M))).astype(BF16)
    gate_bias = jnp.pad(b_gate.reshape(depth, 1, 2 * H_M), ((0, 0), (0, 0), (0, N_GATE_PAD - 2 * H_M)))

    cc_p, ss_p = _rope_tables(jnp.arange(s, dtype=jnp.int32))
    pos_s = PAST_LEN + jnp.arange(t, dtype=jnp.int32)
    cc_s, ss_s = (jnp.tile(a, (bs, 1)) for a in _rope_tables(pos_s))

    xp = x_prompt.reshape(s, d)
    xs = x_sample.reshape(bs * t, d)
    rows_s = bs * t
    tm_p = 1024
    outs =[[] for _ in range(12)]
    zeros_c = jnp.zeros((bp, H_M, DK_M, DV_M), F32)
    zeros_n = jnp.zeros((bp, H_M, DK_M), F32)
    zeros_m = jnp.zeros((bp, H_M), F32)
    row2 = lambda v: v.reshape(1, -1)

    for l in range(depth):
        mod_p = [mod[l, 0:1, k * d:(k + 1) * d] for k in range(6)]
        mod_s = [jnp.repeat(mod[l, bp:bp + bs, k * d:(k + 1) * d], t, axis=0) for k in range(6)]
        ln1 = (row2(ln1_g[l]), row2(ln1_b[l]))
        ffn_small = (conv_w[l], row2(conv_b[l]), row2(ln2_g[l]), row2(ln2_b[l]))

        main_s, kv_s, gates_s = _inproj(xs, mod_s[1], mod_s[0], w_m, w_a, w_gate, gate_bias, cc_s, ss_s, l,
                                        tm=rows_s, main_dtype=F32)
        h_s, c_s, n_s, m_s = _mlstm(main_s, gates_s, state_C, state_n, state_m, batch=bs, L=t, layer=l,
                                    nseq=MLSTM_SEQS_PER_STEP)
        att_s = _attn_sample(main_s, cache_k_win, cache_v_win, l, bs, t)
        x1_s, w_out_b = _merge(h_s, main_s, att_s, xs, mod_s[2], row2(g_head[l]), w_out, *ln1, alpha,
                               tm=rows_s, layer=l)
        hist = [jnp.repeat(state_conv[l][:, r, :], t, axis=0) for r in range(CONV_W - 1)]
        xs, g_s, *ffn_w = _ffn(x1_s, mod_s[4], mod_s[3], mod_s[5], (w_up, w_down), *ffn_small, alpha,
                               tm=rows_s, layer=l, hist=hist, seq_len=t)

        main_p, kv_p, gates_p = _inproj(xp, mod_p[1], mod_p[0], w_m, w_a, w_gate, gate_bias, cc_p, ss_p, l,
                                        tm=tm_p, main_dtype=BF16)
        h_p, c_p, n_p, m_p = _mlstm(main_p, gates_p, zeros_c, zeros_n, zeros_m, batch=bp, L=128)
        att_p = _attn_prompt(main_p)
        x1_p, = _merge(h_p, main_p, att_p, xp, mod_p[2], row2(g_head[l]), w_out_b, *ln1, alpha, tm=512)
        xp, tail_p = _ffn(x1_p, mod_p[4], mod_p[3], mod_p[5], ffn_w, *ffn_small, alpha, tm=tm_p)

        wp = min(DILATIONS[-1][0], s)
        k_cols = slice(0, W_HEADS)
        v_cols = slice(W_HEADS, 2 * W_HEADS)
        outs[0].append(kv_p[s - wp:, k_cols].reshape(bp, wp, H_A, HEAD_DIM))
        outs[1].append(kv_p[s - wp:, v_cols].reshape(bp, wp, H_A, HEAD_DIM))
        outs[2].append(c_p)
        outs[3].append(n_p)
        outs[4].append(m_p.reshape(bp, H_M))
        outs[5].append(tail_p[-8:][8 - (CONV_W - 1):].reshape(bp, CONV_W - 1, d_ff))
        outs[6].append(kv_s[:, k_cols].reshape(bs, t, H_A, HEAD_DIM))
        outs[7].append(kv_s[:, v_cols].reshape(bs, t, H_A, HEAD_DIM))
        outs[8].append(c_s)
        outs[9].append(n_s)
        outs[10].append(m_s.reshape(bs, H_M))
        outs[11].append(g_s.reshape(bs, t, d_ff)[:, t - (CONV_W - 1):])

    return (xp.reshape(bp, s, d), xs.reshape(bs, t, d)) + tuple(jnp.stack(o) for o in outs)
```

```python
import functools

import jax
import jax.numpy as jnp
from jax import lax
from jax.experimental import pallas as pl
from jax.experimental.pallas import tpu as pltpu

F32 = jnp.float32
BF16 = jnp.bfloat16
HIGHEST = lax.Precision.HIGHEST

HEAD_DIM = 128
H_M = 8
H_A = 8
DK_M = 64
DV_M = 128
DILATIONS = ((128, 1), (512, 4), (2048, 16))
ATT_BLOCK = 128
PAST_LEN = 8192
ROPE_THETA = 10000.0
CONV_W = 3
LN_EPS = 1e-5
HEAD_NORM_EPS = 1e-6
N_GATE_PAD = 128
W_HEADS = H_A * HEAD_DIM
ATT_SUPER = ATT_BLOCK * DILATIONS[-1][1]
MLSTM_SEQS_PER_STEP = 4

VMEM_LIMIT_BYTES = 60 * 1024 * 1024


def _cparams(*sem):
    return pltpu.CompilerParams(dimension_semantics=sem, vmem_limit_bytes=VMEM_LIMIT_BYTES)


def _dot(a, b):
    return jnp.dot(a, b, preferred_element_type=F32)


def _dot_nt(a, b, precision=None):
    return lax.dot_general(a, b, (((1,), (1,)), ((), ())), precision=precision,
                           preferred_element_type=F32)


def _dot_tn(a, b):
    return lax.dot_general(a, b, (((0,), (0,)), ((), ())), preferred_element_type=F32)


def _iota(shape, dim):
    return lax.broadcasted_iota(jnp.int32, shape, dim)


def _mod_kernel(c_ref, w_ref, b_ref, o_ref):
    c = c_ref[...]
    a = (c * jax.nn.sigmoid(c)).astype(BF16)
    o_ref[...] = _dot(a, w_ref[...].astype(BF16)) + b_ref[...]


def _modulation(c_all, w_ada, b_ada, tn=1024):
    depth, d, n = w_ada.shape
    rows = c_all.shape[0]
    return pl.pallas_call(
        _mod_kernel,
        grid=(depth, n // tn),
        in_specs=[
            pl.BlockSpec((rows, d), lambda l, j: (0, 0)),
            pl.BlockSpec((None, d, tn), lambda l, j: (l, 0, j)),
            pl.BlockSpec((None, 1, tn), lambda l, j: (l, 0, j)),
        ],
        out_specs=pl.BlockSpec((None, rows, tn), lambda l, j: (l, 0, j)),
        out_shape=jax.ShapeDtypeStruct((depth, rows, n), F32),
        compiler_params=_cparams("arbitrary", "arbitrary"),
        name="adaln_mod",
    )(c_all, w_ada, b_ada.reshape(depth, 1, n))


def _inproj_kernel(x_ref, sc_ref, sh_ref, wm_ref, wa_ref, wg_ref, gb_ref, cc_ref, ss_ref,
                   main_ref, kv_ref, gate_ref, u_scr, *, tn, q_lo, k_lo, v_lo, sub):
    j = pl.program_id(1)
    tm = u_scr.shape[0]

    def rows_of(ref, rs):
        return ref[...] if ref.shape[0] == 1 else ref[rs, :]

    def finish(kind, acc, rs, c0):
        width = acc.shape[1]
        cols = slice(c0, c0 + width)
        if kind == "plain":
            main_ref[rs, cols] = acc.astype(main_ref.dtype)
        elif kind == "v":
            main_ref[rs, cols] = acc.astype(main_ref.dtype)
            kv_ref[rs, cols] = acc
        else:
            for g0 in range(0, width, HEAD_DIM):
                a = acc[:, g0:g0 + HEAD_DIM]
                sl = slice(c0 + g0, c0 + g0 + HEAD_DIM)
                y = a * cc_ref[rs, :] + pltpu.roll(a, HEAD_DIM // 2, 1) * ss_ref[rs, :]
                if kind == "q":
                    main_ref[rs, sl] = (y * (HEAD_DIM ** -0.5)).astype(main_ref.dtype)
                else:
                    main_ref[rs, sl] = y.astype(main_ref.dtype)
                    kv_ref[rs, sl] = y

    def step(kind, first):
        w_ref = wm_ref if kind == "plain" else wa_ref
        for r0 in range(0, tm, sub):
            rs = slice(r0, r0 + sub)
            if first:
                ub = (x_ref[rs, :] * (1.0 + rows_of(sc_ref, rs)) + rows_of(sh_ref, rs)).astype(BF16)
                u_scr[rs, :] = ub
                z = _dot(ub, wg_ref[...]) + gb_ref[...]
                lane = _iota(z.shape, 1)
                log_sig = jnp.minimum(z, 0.0) - jnp.log1p(jnp.exp(-jnp.abs(z)))
                gate_ref[rs, :] = jnp.where((lane >= H_M) & (lane < 2 * H_M), log_sig, z)
            else:
                ub = u_scr[rs, :]
            finish(kind, _dot(ub, w_ref[...]), rs, 0)

    pl.when(j == 0)(functools.partial(step, "plain", True))
    pl.when((j > 0) & (j < q_lo))(functools.partial(step, "plain", False))
    pl.when((j >= q_lo) & (j < k_lo))(functools.partial(step, "q", False))
    pl.when((j >= k_lo) & (j < v_lo))(functools.partial(step, "k", False))
    pl.when(j >= v_lo)(functools.partial(step, "v", False))


def _inproj(x, sc, sh, w_m, w_a, w_gate, gate_bias, rope_cc, rope_ss, layer, tm, main_dtype, tn=1024, sub=512):
    rows, d = x.shape
    n = w_m.shape[2] + w_a.shape[2]
    rm = sc.shape[0]
    mod_block = (1, d) if rm == 1 else (tm, d)
    mod_map = (lambda i, j: (0, 0)) if rm == 1 else (lambda i, j: (i, 0))
    q_lo = (3 * W_HEADS) // tn
    k_lo = (4 * W_HEADS) // tn
    v_lo = (5 * W_HEADS) // tn
    kern = functools.partial(_inproj_kernel, tn=tn, q_lo=q_lo, k_lo=k_lo, v_lo=v_lo, sub=min(sub, tm))
    return pl.pallas_call(
        kern,
        grid=(rows // tm, n // tn),
        in_specs=[
            pl.BlockSpec((tm, d), lambda i, j: (i, 0)),
            pl.BlockSpec(mod_block, mod_map),
            pl.BlockSpec(mod_block, mod_map),
            pl.BlockSpec((None, d, tn), lambda i, j: (layer, 0, jnp.minimum(j, q_lo - 1))),
            pl.BlockSpec((None, d, tn), lambda i, j: (layer, 0, jnp.maximum(j - q_lo, 0))),
            pl.BlockSpec((None, d, N_GATE_PAD), lambda i, j: (layer, 0, 0)),
            pl.BlockSpec((None, 1, N_GATE_PAD), lambda i, j: (layer, 0, 0)),
            pl.BlockSpec((tm, HEAD_DIM), lambda i, j: (i, 0)),
            pl.BlockSpec((tm, HEAD_DIM), lambda i, j: (i, 0)),
        ],
        out_specs=[
            pl.BlockSpec((tm, tn), lambda i, j: (i, j)),
            pl.BlockSpec((tm, tn), lambda i, j: (i, jnp.maximum(j - k_lo, 0))),
            pl.BlockSpec((tm, N_GATE_PAD), lambda i, j: (i, 0)),
        ],
        out_shape=[
            jax.ShapeDtypeStruct((rows, n), main_dtype),
            jax.ShapeDtypeStruct((rows, 2 * W_HEADS), F32),
            jax.ShapeDtypeStruct((rows, N_GATE_PAD), F32),
        ],
        scratch_shapes=[pltpu.VMEM((tm, d), BF16)],
        compiler_params=_cparams("arbitrary", "arbitrary"),
        name="inproj",
    )(x, sc, sh, w_m, w_a, w_gate, gate_bias, rope_cc, rope_ss)


def _mlstm_kernel(q_ref, k_ref, v_ref, g_ref, c0_ref, n0_ref, m0_ref,
                  h_ref, c_ref, n_ref, m_ref, *, L, nseq):
    @pl.when(pl.program_id(1) == 0)
    def _():
        c_ref[...] = c0_ref[...]
        n_ref[...] = n0_ref[...]
        m_ref[...] = m0_ref[...]

    row = _iota((L, L), 0)
    col = _iota((L, L), 1)
    causal = row >= col
    tril = causal.astype(F32)
    triu = (row <= col).astype(F32)
    eye = (_iota((2 * H_M, N_GATE_PAD), 0) == _iota((2 * H_M, N_GATE_PAD), 1)).astype(F32)
    hi_dot = functools.partial(jnp.dot, precision=HIGHEST, preferred_element_type=F32)

    seqs = []
    for bi in range(nseq):
        gates = g_ref[bi * L:(bi + 1) * L, :]
        gates_t = _dot_nt(eye, gates, precision=HIGHEST)
        seqs.append((gates, gates_t, hi_dot(tril, gates), hi_dot(gates_t, triu)))

    units = []
    for bi, (gates, gates_t, bcol_all, brow_all) in enumerate(seqs):
        rows = slice(bi * L, (bi + 1) * L)
        for h in range(H_M):
            b_col = bcol_all[:, H_M + h:H_M + h + 1]
            b_row = brow_all[H_M + h:H_M + h + 1, :]
            m0 = m_ref[bi, :, h:h + 1]
            dmat = jnp.where(causal, b_col - b_row + gates_t[h:h + 1, :], -jnp.inf)
            m_inter = b_col + m0
            m = jnp.maximum(m_inter, jnp.max(dmat, axis=1, keepdims=True))
            q = q_ref[rows, h * DK_M:(h + 1) * DK_M].astype(F32)
            k = k_ref[rows, h * DK_M:(h + 1) * DK_M].astype(F32) * (DK_M ** -0.5)
            qb = q.astype(BF16)
            units.append(dict(
                bi=bi, h=h, rows=rows, m0=m0, m=m, w=jnp.exp(dmat - m), g=jnp.exp(m_inter - m),
                q=q, k=k, qb=qb, vb=v_ref[rows, h * DV_M:(h + 1) * DV_M].astype(BF16),
                s_raw=_dot_nt(qb, k.astype(BF16)), b_col=b_col, ig_col=gates[:, h:h + 1]))

    for u in units:
        bi, h = u["bi"], u["h"]
        s = u["s_raw"] * u["w"]
        num = u["g"] * _dot(u["qb"], c_ref[bi, h].astype(BF16)) + _dot(s.astype(BF16), u["vb"])
        den = (u["g"] * jnp.sum(u["q"] * n_ref[bi, h:h + 1, :], axis=1, keepdims=True)
               + jnp.sum(s, axis=1, keepdims=True))
        h_ref[u["rows"], h * DV_M:(h + 1) * DV_M] = num / jnp.maximum(jnp.abs(den), jnp.exp(-u["m"]))

    for u in units:
        bi, h, b_col, m0 = u["bi"], u["h"], u["b_col"], u["m0"]
        b_last = b_col[L - 1:L, :]
        a_col = b_last - b_col + u["ig_col"]
        m_end = jnp.maximum(b_last + m0, jnp.max(a_col, axis=0, keepdims=True))
        g_end = jnp.exp(b_last + m0 - m_end)
        kw = jnp.exp(a_col - m_end) * u["k"]
        c_ref[bi, h] = g_end * c_ref[bi, h] + _dot_tn(kw.astype(BF16), u["vb"])
        n_ref[bi, h:h + 1, :] = g_end * n_ref[bi, h:h + 1, :] + jnp.sum(kw, axis=0, keepdims=True)
        m_ref[bi, :, h:h + 1] = m_end


def _mlstm_wide_kernel(q_ref, k_ref, v_ref, g_ref, c0_ref, n0_ref, m0_ref,
                       h_ref, c_ref, n_ref, m_ref, nmat_scr):
    L = HEAD_DIM
    first = pl.program_id(1) == 0
    sub8 = _iota((H_M, L), 0)

    @pl.when(first)
    def _():
        c_ref[...] = c0_ref[...]
        n_ref[...] = n0_ref[...]
        m_ref[...] = m0_ref[...]
        for h in range(H_M):
            nmat_scr[h] = lax.dot_general(n0_ref[...], (sub8 == h).astype(F32), (((0,), (0,)), ((), ())),
                                          precision=HIGHEST, preferred_element_type=F32)

    gates = g_ref[...]
    row = _iota((L, L), 0)
    col = _iota((L, L), 1)
    causal = row >= col
    triu = (row <= col).astype(F32)
    eye = (_iota((2 * H_M, N_GATE_PAD), 0) == _iota((2 * H_M, N_GATE_PAD), 1)).astype(F32)
    gates_t = _dot_nt(eye, gates, precision=HIGHEST)
    ig = gates_t[:H_M]
    b = jnp.dot(gates_t[H_M:], triu, precision=HIGHEST, preferred_element_type=F32)
    a = ig - b
    cm = a
    lane = _iota((H_M, L), 1)
    shift = 1
    while shift < L:
        cm = jnp.maximum(cm, jnp.where(lane >= shift, pltpu.roll(cm, shift, 1), -jnp.inf))
        shift *= 2

    eye8 = _iota((H_M, H_M), 0) == _iota((H_M, H_M), 1)
    m0_col = jnp.sum(jnp.where(eye8, jnp.broadcast_to(m_ref[...], (H_M, H_M)), 0.0), axis=1, keepdims=True)
    mm = jnp.maximum(m0_col, cm)
    mm_last = mm[:, L - 1:L]
    ws_all = jnp.exp(a - mm_last)
    g_end_all = jnp.exp(m0_col - mm_last)
    m_end = b[:, L - 1:L] + mm_last
    m_ref[...] = jnp.sum(jnp.where(eye8, jnp.broadcast_to(m_end, (H_M, H_M)), 0.0), axis=0, keepdims=True)
    def split(x, parts):
        out = []
        for _ in range(parts):
            p = x.astype(BF16)
            out.append(p)
            x = x - p.astype(F32)
        return out

    heads = range(H_M)
    ones_b = jnp.ones((L, L), BF16)

    cols = jnp.concatenate([-mm, -mm - b], axis=0).T
    n_sel = 3 * 2 * H_M
    sel_r = _iota((n_sel, H_M * 2 * L), 0) & (2 * H_M - 1)
    sel_c = _iota((n_sel, H_M * 2 * L), 1)
    sel_h = sel_c >> ((2 * L).bit_length() - 1)
    left = (sel_c & (2 * L - 1)) < L
    sel = (left & (sel_r == sel_h)) | (jnp.logical_not(left) & (sel_r == H_M + sel_h))
    bc = _dot(jnp.concatenate(split(cols, 3), axis=1), sel.astype(F32).astype(BF16))

    qb = [q_ref[:, h * DK_M:(h + 1) * DK_M].astype(BF16) for h in heads]
    kt = [(k_ref[:, h * DK_M:(h + 1) * DK_M].astype(F32) * (DK_M ** -0.5)).T for h in heads]
    vo = [jnp.concatenate([v_ref[:, h * DV_M:(h + 1) * DV_M].astype(BF16), ones_b], axis=1) for h in heads]
    s_raw = [_dot(qb[h], kt[h].astype(BF16)) for h in heads]
    inter = [_dot(qb[h], jnp.concatenate([c_ref[h].astype(BF16), nmat_scr[h].astype(BF16)], axis=1))
             for h in heads]

    for h in heads:
        neg_mm = bc[:, 2 * L * h:2 * L * h + L]
        neg_m = bc[:, 2 * L * h + L:2 * L * (h + 1)]
        w = jnp.where(causal, jnp.exp(neg_mm + a[h:h + 1, :]), 0.0)
        gmat = jnp.exp(neg_mm + m0_col[h:h + 1, :])
        s_hi, s_lo = split(s_raw[h] * w, 2)
        r = _dot(s_hi, vo[h])
        num = gmat * inter[h][:, :L] + r[:, :L]
        den = gmat * inter[h][:, L:] + r[:, L:] + _dot(s_lo, ones_b)
        h_ref[:, h * DV_M:(h + 1) * DV_M] = num / jnp.maximum(jnp.abs(den), jnp.exp(neg_m))

    for h in heads:
        g_end = g_end_all[h:h + 1, :]
        kw_hi, kw_lo = split(kt[h] * ws_all[h:h + 1, :], 2)
        u = _dot(kw_hi, vo[h])
        c_ref[h] = g_end * c_ref[h] + u[:, :L]
        nmat_scr[h] = g_end * nmat_scr[h] + u[:, L:] + _dot(kw_lo, ones_b)

    @pl.when(pl.program_id(1) == pl.num_programs(1) - 1)
    def _():
        for h in heads:
            n_ref[h:h + 1, :] = nmat_scr[h].T[0:1, :]


def _mlstm(main, gates, c0, n0, m0, batch, L, layer=None, nseq=1):
    rows = main.shape[0]
    nc = rows // (batch * L)
    if L == HEAD_DIM:
        assert nseq == 1
        kern, scratch, lead = _mlstm_wide_kernel, [pltpu.VMEM((H_M, DK_M, HEAD_DIM), F32)], None
    else:
        assert nc == 1 and batch % nseq == 0
        kern, scratch, lead = functools.partial(_mlstm_kernel, L=L, nseq=nseq), [], nseq
    m0 = m0.reshape(m0.shape[:-1] + (1, H_M))

    def state_spec(tail, stacked):
        zeros = (0,) * len(tail)
        if stacked:
            return pl.BlockSpec((None, lead) + tail, lambda b, c: (layer, b) + zeros)
        return pl.BlockSpec((lead,) + tail, lambda b, c: (b,) + zeros)

    tails = ((H_M, DK_M, DV_M), (H_M, DK_M), (1, H_M))
    blk = nseq * L
    return pl.pallas_call(
        kern,
        grid=(batch // nseq, nc),
        in_specs=[
            pl.BlockSpec((blk, H_M * DK_M), lambda b, c: (b * nc + c, 0)),
            pl.BlockSpec((blk, H_M * DK_M), lambda b, c: (b * nc + c, 1)),
            pl.BlockSpec((blk, H_M * DV_M), lambda b, c: (b * nc + c, 1)),
            pl.BlockSpec((blk, N_GATE_PAD), lambda b, c: (b * nc + c, 0)),
        ] + [state_spec(t, layer is not None) for t in tails],
        out_specs=[pl.BlockSpec((blk, H_M * DV_M), lambda b, c: (b * nc + c, 0))]
        + [state_spec(t, False) for t in tails],
        out_shape=[
            jax.ShapeDtypeStruct((rows, H_M * DV_M), F32),
            jax.ShapeDtypeStruct((batch, H_M, DK_M, DV_M), F32),
            jax.ShapeDtypeStruct((batch, H_M, DK_M), F32),
            jax.ShapeDtypeStruct((batch, 1, H_M), F32),
        ],
        scratch_shapes=scratch,
        compiler_params=_cparams("arbitrary", "arbitrary"),
        name="mlstm",
    )(main, main, main, gates, c0, n0, m0)


def _attn_prompt_kernel(q_ref, kp_ref, kc_ref, vp_ref, vc_ref, o_ref,
                        qq_scr, kk_scr, vv_scr, num_scr, m_scr, den_scr):
    sb = pl.program_id(0)
    SB = ATT_SUPER
    qq_scr[...] = q_ref[...].astype(F32)
    kk_scr[0:SB, :] = kp_ref[...].astype(F32)
    kk_scr[SB:, :] = kc_ref[...].astype(F32)
    vv_scr[0:SB, :] = vp_ref[...].astype(F32)
    vv_scr[SB:, :] = vc_ref[...].astype(F32)

    shape = (ATT_BLOCK, 2 * ATT_BLOCK)
    qi = _iota(shape, 0)
    ki = _iota(shape, 1)
    window = (ki >= qi) & (ki <= qi + ATT_BLOCK)
    window_first = window & ((ki >= ATT_BLOCK) | (sb > 0))
    ones_b = jnp.ones((2 * ATT_BLOCK, HEAD_DIM), BF16)

    for bi, (_, dil) in enumerate(DILATIONS):
        for n in range(SB // (ATT_BLOCK * dil)):
            valid = window_first if n == 0 else window

            for r in range(dil):
                q_rows = pl.ds(r + n * ATT_BLOCK * dil, ATT_BLOCK, stride=dil)
                k_rows = pl.ds(SB + r + (n - 1) * ATT_BLOCK * dil, 2 * ATT_BLOCK, stride=dil)
                q = qq_scr[q_rows, :].astype(BF16)
                k = kk_scr[k_rows, :].astype(BF16)
                v = vv_scr[k_rows, :].astype(BF16)
                s = jnp.where(valid, _dot_nt(q, k), -jnp.inf)
                m = jnp.max(s, axis=1, keepdims=True)
                p = jnp.exp(s - m).astype(BF16)
                r = _dot(p, jnp.concatenate([v, ones_b], axis=1))
                num_scr[bi, q_rows, :] = r[:, :HEAD_DIM]
                den_scr[bi, q_rows, :] = r[:, HEAD_DIM:]
                m_scr[bi, q_rows, :] = jnp.broadcast_to(m, (ATT_BLOCK, HEAD_DIM))

    rows_per_step = 2 * ATT_BLOCK

    def mix(i, carry):
        rows = pl.ds(pl.multiple_of(i * rows_per_step, rows_per_step), rows_per_step)
        ms = [m_scr[b, rows, :] for b in range(len(DILATIONS))]
        big = ms[0]
        for mm in ms[1:]:
            big = jnp.maximum(big, mm)
        w0 = jnp.exp(ms[0] - big)
        acc_num = w0 * num_scr[0, rows, :]
        acc_den = w0 * den_scr[0, rows, :]
        for b in range(1, len(DILATIONS)):
            w = jnp.exp(ms[b] - big)
            acc_num = acc_num + w * num_scr[b, rows, :]
            acc_den = acc_den + w * den_scr[b, rows, :]
        o_ref[rows, :] = acc_num / acc_den
        return carry

    lax.fori_loop(0, SB // rows_per_step, mix, 0)


def _attn_prompt(main):
    s, n_main = main.shape
    SB = ATT_SUPER
    cols = n_main // HEAD_DIM // 6
    blk = (SB, HEAD_DIM)
    prev = lambda i: jnp.maximum(i - 1, 0)
    nbr = len(DILATIONS)
    return pl.pallas_call(
        _attn_prompt_kernel,
        grid=(s // SB, H_A),
        in_specs=[
            pl.BlockSpec(blk, lambda i, h: (i, 3 * cols + h)),
            pl.BlockSpec(blk, lambda i, h: (prev(i), 4 * cols + h)),
            pl.BlockSpec(blk, lambda i, h: (i, 4 * cols + h)),
            pl.BlockSpec(blk, lambda i, h: (prev(i), 5 * cols + h)),
            pl.BlockSpec(blk, lambda i, h: (i, 5 * cols + h)),
        ],
        out_specs=pl.BlockSpec(blk, lambda i, h: (i, h)),
        out_shape=jax.ShapeDtypeStruct((s, W_HEADS), F32),
        scratch_shapes=[
            pltpu.VMEM((SB, HEAD_DIM), F32),
            pltpu.VMEM((2 * SB, HEAD_DIM), F32),
            pltpu.VMEM((2 * SB, HEAD_DIM), F32),
            pltpu.VMEM((nbr, SB, HEAD_DIM), F32),
            pltpu.VMEM((nbr, SB, HEAD_DIM), F32),
            pltpu.VMEM((nbr, SB, HEAD_DIM), F32),
        ],
        compiler_params=_cparams("arbitrary", "arbitrary"),
        name="attn_prompt",
    )(main, main, main, main, main)


def _branch_count(delta):
    cnt = jnp.zeros(delta.shape, F32)
    for window, dil in DILATIONS:
        hit = (delta >= 0) & (delta <= window) & ((delta & (dil - 1)) == 0)
        cnt = cnt + hit.astype(F32)
    return cnt


def _attn_sample_kernel(q_ref, kn_ref, vn_ref, kfar_ref, knear_ref, vfar_ref, vnear_ref, o_ref,
                        cnt_scr, s_scr, *, T, NB, chunk):
    R = H_A * T
    t_bits = T.bit_length() - 1
    h_bits = H_A.bit_length() - 1
    d_far = DILATIONS[-1][1]
    n_groups, keep = kfar_ref.shape[0], kfar_ref.shape[1]
    keep_bits = keep.bit_length() - 1
    near_pos = knear_ref.shape[0] // H_A
    gpc = chunk // (keep * H_A)
    far_chunks = n_groups // gpc
    n_chunks = far_chunks + near_pos * H_A // chunk
    chunks = [slice(c * chunk, (c + 1) * chunk) for c in range(n_chunks)]

    def rows(far_ref, near_ref, c):
        if c < far_chunks:
            return far_ref[c * gpc:(c + 1) * gpc].reshape(chunk, HEAD_DIM)
        return near_ref[chunks[c - far_chunks], :]

    @pl.when(pl.program_id(0) == 0)
    def _():
        for c, sl in enumerate(chunks):
            row = _iota((R, chunk), 0)
            col = _iota((R, chunk), 1)
            if c < far_chunks:
                pos = (c * gpc + (col >> (h_bits + keep_bits))) * d_far + ((col >> h_bits) & (keep - 1))
            else:
                pos = NB - near_pos + (c - far_chunks) * (chunk // H_A) + (col >> h_bits)
            same_head = (row >> t_bits) == (col & (H_A - 1))
            cnt_scr[:, sl] = jnp.where(same_head, _branch_count(NB + (row & (T - 1)) - pos), 0.0)

    heads = [slice(h * HEAD_DIM, (h + 1) * HEAD_DIM) for h in range(H_A)]
    q = jnp.concatenate([q_ref[:, sl] for sl in heads], axis=0).astype(BF16)
    kn = jnp.concatenate([kn_ref[:, sl] for sl in heads], axis=0).astype(BF16)
    vn = jnp.concatenate([vn_ref[:, sl] for sl in heads], axis=0).astype(BF16)
    row = _iota((R, R), 0)
    col = _iota((R, R), 1)
    cnt_n = jnp.where((row >> t_bits) == (col >> t_bits),
                      _branch_count((row & (T - 1)) - (col & (T - 1))), 0.0)

    s_n = jnp.where(cnt_n > 0, _dot_nt(q, kn), -jnp.inf)
    big = jnp.max(s_n, axis=1, keepdims=True)
    for c, sl in enumerate(chunks):
        s = jnp.where(cnt_scr[:, sl] > 0, _dot_nt(q, rows(kfar_ref, knear_ref, c).astype(BF16)), -jnp.inf)
        s_scr[:, sl] = s
        big = jnp.maximum(big, jnp.max(s, axis=1, keepdims=True))

    p_n = cnt_n * jnp.exp(s_n - big)
    den = jnp.sum(p_n, axis=1, keepdims=True)
    num = _dot(p_n.astype(BF16), vn)
    for c, sl in enumerate(chunks):
        p = cnt_scr[:, sl] * jnp.exp(s_scr[:, sl] - big)
        den = den + jnp.sum(p, axis=1, keepdims=True)
        num = num + _dot(p.astype(BF16), rows(vfar_ref, vnear_ref, c).astype(BF16))
    out = num / den
    for h, sl in enumerate(heads):
        o_ref[:, sl] = out[h * T:(h + 1) * T, :]


def _attn_sample(main, cache_k, cache_v, layer, batch, T, chunk=2048):
    depth, _, nb, n_heads, hd = cache_k.shape
    assert n_heads == H_A and hd == HEAD_DIM and T & (T - 1) == 0 and H_A & (H_A - 1) == 0
    d_far = DILATIONS[-1][1]
    near_pos = DILATIONS[-2][0]
    assert all(w <= near_pos for w, _ in DILATIONS[:-1]) and nb % d_far == 0 and T <= d_far
    assert (nb - near_pos) % d_far == 0 and near_pos <= nb
    n_groups = (nb - near_pos) // d_far
    nf = nb * H_A
    n_keys = (n_groups * T + near_pos) * H_A
    assert (n_groups * T * H_A) % chunk == 0 and (near_pos * H_A) % chunk == 0 and nf % (near_pos * H_A) == 0
    far_view = lambda c: c.reshape(depth, batch, nb // d_far, d_far, H_A, HEAD_DIM)
    near_view = lambda c: c.reshape(depth, batch, nf, HEAD_DIM)
    kern = functools.partial(_attn_sample_kernel, T=T, NB=nb, chunk=chunk)
    far_spec = pl.BlockSpec((None, None, n_groups, T, H_A, HEAD_DIM), lambda b: (layer, b, 0, 0, 0, 0))
    near_spec = pl.BlockSpec((None, None, near_pos * H_A, HEAD_DIM),
                             lambda b: (layer, b, nf // (near_pos * H_A) - 1, 0))
    return pl.pallas_call(
        kern,
        grid=(batch,),
        in_specs=[
            pl.BlockSpec((T, W_HEADS), lambda b: (b, 3)),
            pl.BlockSpec((T, W_HEADS), lambda b: (b, 4)),
            pl.BlockSpec((T, W_HEADS), lambda b: (b, 5)),
            far_spec,
            near_spec,
            far_spec,
            near_spec,
        ],
        out_specs=pl.BlockSpec((T, W_HEADS), lambda b: (b, 0)),
        out_shape=jax.ShapeDtypeStruct((batch * T, W_HEADS), F32),
        scratch_shapes=[pltpu.VMEM((H_A * T, n_keys), F32), pltpu.VMEM((H_A * T, n_keys), F32)],
        compiler_params=_cparams("arbitrary"),
        name="attn_sample",
    )(main, main, main, far_view(cache_k), near_view(cache_k), far_view(cache_v), near_view(cache_v))


def _layer_norm(y, g, b):
    mu = jnp.mean(y, axis=1, keepdims=True)
    yc = y - mu
    var = jnp.mean(yc * yc, axis=1, keepdims=True)
    return yc * lax.rsqrt(var + LN_EPS) * g + b


def _merge_kernel(h_ref, om_ref, att_ref, x_ref, gt_ref, gh_ref, w_ref, lg_ref, lb_ref,
                  o_ref, *rest, alpha, sub):
    w_copy_ref, cat_scr = rest if len(rest) == 2 else (None, rest[0])
    if w_copy_ref is not None:
        w_copy_ref[...] = w_ref[...].astype(BF16)
        w_ref = w_copy_ref
    for r0 in range(0, cat_scr.shape[0], sub):
        rs = slice(r0, r0 + sub)
        for h in range(H_M):
            sl = slice(h * DV_M, (h + 1) * DV_M)
            hh = h_ref[rs, sl]
            hn = hh * lax.rsqrt(jnp.mean(hh * hh, axis=1, keepdims=True) + HEAD_NORM_EPS) * gh_ref[:, sl]
            cat_scr[rs, sl] = (hn * jax.nn.sigmoid(om_ref[rs, sl].astype(F32))).astype(BF16)
        cat_scr[rs, H_M * DV_M:] = att_ref[rs, :].astype(BF16)
        mix = _dot(cat_scr[rs, :], w_ref[...])
        gt = gt_ref[...] if gt_ref.shape[0] == 1 else gt_ref[rs, :]
        y = alpha * x_ref[rs, :] + (1.0 + gt) * mix
        o_ref[rs, :] = _layer_norm(y, lg_ref[...], lb_ref[...])


def _merge(h, main, att, x, gt, g_head, w_out, ln_g, ln_b, alpha, tm, layer=None, sub=256):
    rows, d = x.shape
    rm = gt.shape[0]
    mod_block = (1, d) if rm == 1 else (tm, d)
    mod_map = (lambda i: (0, 0)) if rm == 1 else (lambda i: (i, 0))
    wide = pl.BlockSpec((tm, W_HEADS), lambda i: (i, 0))
    const = lambda shape: pl.BlockSpec(shape, lambda i: (0, 0))
    emit = w_out.ndim == 3
    assert not emit or rows == tm
    w_spec = pl.BlockSpec((None, d, d), lambda i: (layer, 0, 0)) if emit else const((d, d))
    row_out = (pl.BlockSpec((tm, d), lambda i: (i, 0)), jax.ShapeDtypeStruct((rows, d), F32))
    outs = [row_out, (const((d, d)), jax.ShapeDtypeStruct((d, d), BF16))] if emit else [row_out]
    return pl.pallas_call(
        functools.partial(_merge_kernel, alpha=alpha, sub=min(sub, tm)),
        grid=(rows // tm,),
        in_specs=[
            wide,
            pl.BlockSpec((tm, W_HEADS), lambda i: (i, 2)),
            wide,
            pl.BlockSpec((tm, d), lambda i: (i, 0)),
            pl.BlockSpec(mod_block, mod_map),
            const((1, W_HEADS)),
            w_spec,
            const((1, d)),
            const((1, d)),
        ],
        out_specs=[spec for spec, _ in outs],
        out_shape=[shape for _, shape in outs],
        scratch_shapes=[pltpu.VMEM((tm, d), BF16)],
        compiler_params=_cparams("arbitrary"),
        name="mixer_merge",
    )(h, main, att, x, gt, g_head, w_out, ln_g, ln_b)


def _ffn_kernel(*refs, alpha, seq_len, n_chunks, carry, sub):
    if carry:
        (x_ref, sc_ref, sh_ref, gt_ref, wg_ref, wv_ref, wd_ref, cw_ref, cb_ref, lg_ref, lb_ref,
         o_ref, tail_ref, u_scr, carry_scr) = refs
        w_copies = None
    else:
        (x_ref, sc_ref, sh_ref, gt_ref, wg_ref, wv_ref, wd_ref, cw_ref, cb_ref, lg_ref, lb_ref,
         h0_ref, h1_ref, o_ref, tail_ref, wgb_ref, wvb_ref, wdb_ref, u_scr) = refs
        w_copies = (wgb_ref, wvb_ref, wdb_ref)
    i = pl.program_id(0)
    j = pl.program_id(1)
    tm = u_scr.shape[0]
    assert n_chunks >= 2

    def rows_of(ref, rs):
        return ref[...] if ref.shape[0] == 1 else ref[rs, :]

    if carry:
        @pl.when(i == 0)
        def _():
            carry_scr[j] = jnp.zeros(carry_scr.shape[1:], F32)

    def step(first, last):
        cw = cw_ref[...]
        if carry:
            tail = carry_scr[j]
        if w_copies is not None:
            for copy_ref, w_ref in zip(w_copies, (wg_ref, wv_ref, wd_ref)):
                copy_ref[...] = w_ref[...].astype(BF16)
            wg, wv, wd = w_copies
        else:
            wg, wv, wd = wg_ref, wv_ref, wd_ref
        for r0 in range(0, tm, sub):
            rs = slice(r0, r0 + sub)
            if first:
                ub = (x_ref[rs, :] * (1.0 + rows_of(sc_ref, rs)) + rows_of(sh_ref, rs)).astype(BF16)
                u_scr[rs, :] = ub
            else:
                ub = u_scr[rs, :]
            g = _dot(ub, wg[...])
            v = _dot(ub, wv[...])
            row = _iota(g.shape, 0)
            if carry:
                hist0, hist1 = tail[6:7, :], tail[7:8, :]
                pos = row
                tail = g[sub - 8:, :]
            else:
                hist0, hist1 = h0_ref[rs, :], h1_ref[rs, :]
                pos = row & (seq_len - 1)
                tail_ref[rs, :] = g
            prev1 = jnp.where(pos == 0, hist1, pltpu.roll(g, 1, 0))
            prev2 = jnp.where(pos == 0, hist0, jnp.where(pos == 1, hist1, pltpu.roll(g, 2, 0)))
            a = prev2 * cw[0:1, :] + prev1 * cw[1:2, :] + g * cw[2:3, :] + cb_ref[...]
            a = a * jax.nn.sigmoid(a) * v
            acc = _dot(a.astype(BF16), wd[...])
            if not first:
                acc = o_ref[rs, :] + acc
            if last:
                y = alpha * x_ref[rs, :] + (1.0 + rows_of(gt_ref, rs)) * acc
                acc = _layer_norm(y, lg_ref[...], lb_ref[...])
            o_ref[rs, :] = acc
        if carry:
            carry_scr[j] = tail
            tail_ref[...] = tail

    pl.when(j == 0)(functools.partial(step, True, False))
    pl.when((j > 0) & (j < n_chunks - 1))(functools.partial(step, False, False))
    pl.when(j == n_chunks - 1)(functools.partial(step, False, True))


def _ffn(x, sc, sh, gt, weights, conv_w, conv_b, ln_g, ln_b, alpha, tm, layer=None,
         hist=None, seq_len=None, tf=512, sub=512):
    rows, d = x.shape
    carry = hist is None
    d_ff = weights[-1].shape[-2]
    nj = d_ff // tf
    rm = sc.shape[0]
    mod_block = (1, d) if rm == 1 else (tm, d)
    mod_map = (lambda i, j: (0, 0)) if rm == 1 else (lambda i, j: (i, 0))
    up_tile = pl.BlockSpec((d, tf), lambda i, j: (0, j))
    down_tile = pl.BlockSpec((tf, d), lambda i, j: (j, 0))
    if carry:
        w_specs = [up_tile, up_tile, down_tile]
        w_args = list(weights)
    else:
        assert rows == tm
        w_up, w_down = weights
        w_specs = [
            pl.BlockSpec((None, d, tf), lambda i, j: (layer, 0, j)),
            pl.BlockSpec((None, d, tf), lambda i, j: (layer, 0, nj + j)),
            pl.BlockSpec((None, tf, d), lambda i, j: (layer, j, 0)),
        ]
        w_args = [w_up, w_up, w_down]
    in_specs = [
        pl.BlockSpec((tm, d), lambda i, j: (i, 0)),
        pl.BlockSpec(mod_block, mod_map),
        pl.BlockSpec(mod_block, mod_map),
        pl.BlockSpec(mod_block, mod_map),
    ] + w_specs + [
        pl.BlockSpec((CONV_W, tf), lambda i, j: (0, j)),
        pl.BlockSpec((1, tf), lambda i, j: (0, j)),
        pl.BlockSpec((1, d), lambda i, j: (0, 0)),
        pl.BlockSpec((1, d), lambda i, j: (0, 0)),
    ]
    args = [x, sc, sh, gt] + w_args + [conv_w, conv_b, ln_g, ln_b]
    scratch = [pltpu.VMEM((tm, d), BF16)]
    out_specs = [pl.BlockSpec((tm, d), lambda i, j: (i, 0))]
    out_shape = [jax.ShapeDtypeStruct((rows, d), F32)]
    if carry:
        scratch.append(pltpu.VMEM((nj, 8, tf), F32))
        out_specs.append(pl.BlockSpec((8, tf), lambda i, j: (i, j)))
        out_shape.append(jax.ShapeDtypeStruct((rows // tm * 8, d_ff), F32))
    else:
        in_specs += [pl.BlockSpec((tm, tf), lambda i, j: (i, j))] * 2
        args += list(hist)
        out_specs += [pl.BlockSpec((tm, tf), lambda i, j: (i, j)), up_tile, up_tile, down_tile]
        out_shape += [jax.ShapeDtypeStruct((rows, d_ff), F32),
                      jax.ShapeDtypeStruct((d, d_ff), BF16), jax.ShapeDtypeStruct((d, d_ff), BF16),
                      jax.ShapeDtypeStruct((d_ff, d), BF16)]
    kern = functools.partial(_ffn_kernel, alpha=alpha, seq_len=seq_len, n_chunks=nj, carry=carry,
                             sub=min(sub, tm))
    return pl.pallas_call(
        kern,
        grid=(rows // tm, nj),
        in_specs=in_specs,
        out_specs=out_specs,
        out_shape=out_shape,
        scratch_shapes=scratch,
        compiler_params=_cparams("arbitrary", "arbitrary"),
        name="conv_ffn",
    )(*args)


def _rope_tables(pos):
    half = HEAD_DIM // 2
    inv = ROPE_THETA ** (-jnp.arange(half, dtype=F32) / half)
    ang = pos.astype(F32)[:, None] * inv[None, :]
    cos, sin = jnp.cos(ang), jnp.sin(ang)
    return jnp.concatenate([cos, cos], -1), jnp.concatenate([-sin, sin], -1)


def kernel(x_prompt, x_sample, cache_k_win, cache_v_win, state_C, state_n, state_m, state_conv,
           c_prompt, c_sample, w_ada, b_ada, w_in, b_gate, g_head, w_out, ln1_g, ln1_b,
           w_up, conv_w, conv_b, w_down, ln2_g, ln2_b):
    bp, s, d = x_prompt.shape
    bs, t, _ = x_sample.shape
    depth = w_in.shape[0]
    d_ff = w_down.shape[1]
    alpha = (2 * depth) ** 0.25
    assert bp == 1 and d == (H_M + H_A) * HEAD_DIM and s % ATT_SUPER == 0

    n_c = bp + bs
    pad = (-n_c) % 8
    c_all = jnp.concatenate([c_prompt, c_sample, jnp.zeros((pad, d), F32)], 0)
    mod = _modulation(c_all, w_ada, b_ada)

    n_m = 2 * H_M * DK_M + 2 * H_M * DV_M
    w_m = w_in[:, :, :n_m].astype(BF16)
    w_a = w_in[:, :, n_m + 2 * H_M:].astype(BF16)
    w_gate = jnp.pad(w_in[:, :, n_m:n_m + 2 * H_M], ((0, 0), (0, 0), (0, N_GATE_PAD - 2 * H_M))).astype(BF16)
    gate_bias = jnp.pad(b_gate.reshape(depth, 1, 2 * H_M), ((0, 0), (0, 0), (0, N_GATE_PAD - 2 * H_M)))

    cc_p, ss_p = _rope_tables(jnp.arange(s, dtype=jnp.int32))
    pos_s = PAST_LEN + jnp.arange(t, dtype=jnp.int32)
    cc_s, ss_s = (jnp.tile(a, (bs, 1)) for a in _rope_tables(pos_s))

    xp = x_prompt.reshape(s, d)
    xs = x_sample.reshape(bs * t, d)
    rows_s = bs * t
    tm_p = 1024
    outs =[[] for _ in range(12)]
    zeros_c = jnp.zeros((bp, H_M, DK_M, DV_M), F32)
    zeros_n = jnp.zeros((bp, H_M, DK_M), F32)
    zeros_m = jnp.zeros((bp, H_M), F32)
    row2 = lambda v: v.reshape(1, -1)

    for l in range(depth):
        mod_p = [mod[l, 0:1, k * d:(k + 1) * d] for k in range(6)]
        mod_s = [jnp.repeat(mod[l, bp:bp + bs, k * d:(k + 1) * d], t, axis=0) for k in range(6)]
        ln1 = (row2(ln1_g[l]), row2(ln1_b[l]))
        ffn_small = (conv_w[l], row2(conv_b[l]), row2(ln2_g[l]), row2(ln2_b[l]))

        main_s, kv_s, gates_s = _inproj(xs, mod_s[1], mod_s[0], w_m, w_a, w_gate, gate_bias, cc_s, ss_s, l,
                                        tm=rows_s, main_dtype=F32)
        h_s, c_s, n_s, m_s = _mlstm(main_s, gates_s, state_C, state_n, state_m, batch=bs, L=t, layer=l,
                                    nseq=MLSTM_SEQS_PER_STEP)
        att_s = _attn_sample(main_s, cache_k_win, cache_v_win, l, bs, t)
        x1_s, w_out_b = _merge(h_s, main_s, att_s, xs, mod_s[2], row2(g_head[l]), w_out, *ln1, alpha,
                               tm=rows_s, layer=l)
        hist = [jnp.repeat(state_conv[l][:, r, :], t, axis=0) for r in range(CONV_W - 1)]
        xs, g_s, *ffn_w = _ffn(x1_s, mod_s[4], mod_s[3], mod_s[5], (w_up, w_down), *ffn_small, alpha,
                               tm=rows_s, layer=l, hist=hist, seq_len=t)

        main_p, kv_p, gates_p = _inproj(xp, mod_p[1], mod_p[0], w_m, w_a, w_gate, gate_bias, cc_p, ss_p, l,
                                        tm=tm_p, main_dtype=BF16)
        h_p, c_p, n_p, m_p = _mlstm(main_p, gates_p, zeros_c, zeros_n, zeros_m, batch=bp, L=128)
        att_p = _attn_prompt(main_p)
        x1_p, = _merge(h_p, main_p, att_p, xp, mod_p[2], row2(g_head[l]), w_out_b, *ln1, alpha, tm=512)
        xp, tail_p = _ffn(x1_p, mod_p[4], mod_p[3], mod_p[5], ffn_w, *ffn_small, alpha, tm=tm_p)

        wp = min(DILATIONS[-1][0], s)
        k_cols = slice(0, W_HEADS)
        v_cols = slice(W_HEADS, 2 * W_HEADS)
        outs[0].append(kv_p[s - wp:, k_cols].reshape(bp, wp, H_A, HEAD_DIM))
        outs[1].append(kv_p[s - wp:, v_cols].reshape(bp, wp, H_A, HEAD_DIM))
        outs[2].append(c_p)
        outs[3].append(n_p)
        outs[4].append(m_p.reshape(bp, H_M))
        outs[5].append(tail_p[-8:][8 - (CONV_W - 1):].reshape(bp, CONV_W - 1, d_ff))
        outs[6].append(kv_s[:, k_cols].reshape(bs, t, H_A, HEAD_DIM))
        outs[7].append(kv_s[:, v_cols].reshape(bs, t, H_A, HEAD_DIM))
        outs[8].append(c_s)
        outs[9].append(n_s)
        outs[10].append(m_s.reshape(bs, H_M))
        outs[11].append(g_s.reshape(bs, t, d_ff)[:, t - (CONV_W - 1):])

    return (xp.reshape(bp, s, d), xs.reshape(bs, t, d)) + tuple(jnp.stack(o) for o in outs)
```
